```python
import jax, jax.numpy as jnp
from jax import lax
import numpy as np

D_MODEL = 1024
BATCH = 8
SEQ = 2048
DEPTH = 1
DEC_BATCH = 128
DEC_SEQ = 1
PAST_LEN = 16384
PAGE_SIZE = 128

POOL_WIDTH = 1024
POOL_WINDOWS = (2, 4, 8, 16)
POOL_GROUPS = len(POOL_WINDOWS)
POOL_GROUP_WIDTH = POOL_WIDTH // POOL_GROUPS
POOL_BUF = max(POOL_WINDOWS) - 1
RET_HEADS = 4
RET_QK_DIM = D_MODEL // RET_HEADS
RET_V_DIM = 2 * D_MODEL // RET_HEADS
RET_QK_WIDTH = RET_HEADS * RET_QK_DIM
RET_V_WIDTH = RET_HEADS * RET_V_DIM
RET_CHUNK = 128
ROPE_BASE = 10000.0
IN_SIZES = (POOL_WIDTH, RET_QK_WIDTH, RET_QK_WIDTH, RET_V_WIDTH, RET_V_WIDTH, D_MODEL, D_MODEL)
IN_WIDTH = sum(IN_SIZES)
IN_SPLITS = tuple(int(s) for s in np.cumsum(IN_SIZES)[:-1])
N_EXPERTS = 32
TOP_K = 4
D_EXPERT = D_MODEL
SWIGLU_LIMIT = 7.0
SWIGLU_ALPHA = 1.702
N_MOD = 6
EPS = 1e-6

kernel_name = "pool_retention_moe_adaln_decode_step"


def rmsnorm(x, g=None):
    xf = x.astype(jnp.float32)
    y = xf * lax.rsqrt(jnp.mean(xf * xf, axis=-1, keepdims=True) + EPS)
    if g is not None:
        y = y * g.astype(jnp.float32)
    return y.astype(x.dtype)


def retention_log_decay():
    return jnp.log(1.0 - jnp.power(2.0, -5.0 - jnp.arange(RET_HEADS, dtype=jnp.float32)))


def rotary(x, pos):
    half = x.shape[-1] // 2
    inv = jnp.power(ROPE_BASE, -jnp.arange(half, dtype=jnp.float32) / half)
    ang = pos.astype(jnp.float32)[:, None] * inv[None, :]
    cos, sin = jnp.cos(ang), jnp.sin(ang)
    x1, x2 = x[..., :half], x[..., half:]
    return jnp.concatenate([x1 * cos - x2 * sin, x2 * cos + x1 * sin], axis=-1)


def pool_mixer(u, buf, pos0, w_pool, pool_scale):
    T = u.shape[1]
    L = POOL_BUF
    ext = jnp.concatenate([buf.astype(jnp.float32), u.astype(jnp.float32)], axis=1)
    cs = jnp.concatenate([jnp.zeros_like(ext[:, :1]), jnp.cumsum(ext, axis=1)], axis=1)
    end = cs[:, L + 1:L + 1 + T]
    pos = pos0 + jnp.arange(T, dtype=jnp.int32)
    outs = []
    for gi, w in enumerate(POOL_WINDOWS):
        sl = slice(gi * POOL_GROUP_WIDTH, (gi + 1) * POOL_GROUP_WIDTH)
        start = cs[:, L + 1 - w:L + 1 - w + T, sl]
        cnt = jnp.minimum(w, pos + 1).astype(jnp.float32)[None, :, None]
        pooled = (end[..., sl] - start) / cnt - ext[:, L:, sl]
        outs.append(jnp.einsum('btc,cd->btd', pooled, w_pool[gi].astype(jnp.float32)))
    y = jnp.concatenate(outs, axis=-1) * pool_scale.astype(jnp.float32)
    new_buf = ext[:, -L:]
    return y.astype(u.dtype), new_buf


def retention_block(q, k, v, S, log_gamma):
    C = q.shape[2]
    i = jnp.arange(C, dtype=jnp.float32)
    diff = i[:, None] - i[None, :]
    lg = log_gamma[:, None, None]
    decay = jnp.where(diff[None] >= 0, jnp.exp(lg * jnp.maximum(diff, 0.0)[None]), 0.0)
    scores = jnp.einsum('bhid,bhjd->bhij', q, k) * decay[None]
    q_dec = q * jnp.exp(log_gamma[:, None] * (i + 1.0)[None])[None, :, :, None]
    o = jnp.einsum('bhij,bhjv->bhiv', scores, v) + jnp.einsum('bhid,bhdv->bhiv', q_dec, S)
    k_dec = k * jnp.exp(log_gamma[:, None] * (C - 1.0 - i)[None])[None, :, :, None]
    S_new = jnp.exp(log_gamma * C)[None, :, None, None] * S + jnp.einsum('bhjd,bhjv->bhdv', k_dec, v)
    return o, S_new


def retention(q, k, v, S0, log_gamma):
    B, H, T, _ = q.shape
    C = RET_CHUNK if T % RET_CHUNK == 0 else T
    NC = T // C

    def to_chunks(a):
        return jnp.moveaxis(a.reshape(B, H, NC, C, a.shape[-1]), 2, 0)

    def step(S, qkv):
        qc, kc, vc = qkv
        o, S = retention_block(qc, kc, vc, S, log_gamma)
        return S, o

    S, o = lax.scan(step, S0, (to_chunks(q), to_chunks(k), to_chunks(v)))
    o = jnp.moveaxis(o, 0, 2).reshape(B, H, T, -1)
    return o, S


def token_mixer(h, pool_buf, ret_state, pos0, w_in, w_pool, pool_scale, w_up_pool, w_up_ret, w_out):
    B, T, _ = h.shape
    proj = h @ w_in
    u, q, k, v, g_ret, g_a, g_b = jnp.split(proj, IN_SPLITS, axis=-1)
    y_a, new_buf = pool_mixer(u, pool_buf, pos0, w_pool, pool_scale)
    pos = pos0 + jnp.arange(T, dtype=jnp.int32)

    def heads(a, d):
        return jnp.transpose(a.astype(jnp.float32).reshape(B, T, RET_HEADS, d), (0, 2, 1, 3))

    qh = rotary(heads(q, RET_QK_DIM), pos)
    kh = rotary(heads(k, RET_QK_DIM), pos) * (RET_QK_DIM ** -0.5)
    vh = heads(v, RET_V_DIM)
    o, S_new = retention(qh, kh, vh, ret_state.astype(jnp.float32), retention_log_decay())
    o = rmsnorm(o)
    o = jnp.transpose(o, (0, 2, 1, 3)).reshape(B, T, RET_V_WIDTH)
    y_b = (jax.nn.silu(g_ret.astype(jnp.float32)) * o).astype(h.dtype)
    merged = jax.nn.sigmoid(g_a) * (y_a @ w_up_pool) + jax.nn.sigmoid(g_b) * (y_b @ w_up_ret)
    return merged @ w_out, new_buf, S_new


def moe(h, w_router, b_router, w_expert_in, b_expert_in, w_expert_out, b_expert_out):
    B, T, D = h.shape
    xf = h.reshape(B * T, D)
    logits = (xf @ w_router + b_router).astype(jnp.float32)
    top_val, top_idx = lax.top_k(logits, TOP_K)
    top_w = jax.nn.softmax(top_val, axis=-1)
    comb = jnp.sum(jax.nn.one_hot(top_idx, N_EXPERTS, dtype=jnp.float32) * top_w[..., None], axis=1)
    y = jnp.zeros((B * T, D), jnp.float32)
    for e in range(N_EXPERTS):
        hh = (xf @ w_expert_in[e] + b_expert_in[e]).astype(jnp.float32)
        gate = jnp.minimum(hh[:, :D_EXPERT], SWIGLU_LIMIT)
        lin = jnp.clip(hh[:, D_EXPERT:], -SWIGLU_LIMIT, SWIGLU_LIMIT)
        act = ((lin + 1.0) * gate * jax.nn.sigmoid(SWIGLU_ALPHA * gate)).astype(h.dtype)
        y = y + comb[:, e:e + 1] * (act @ w_expert_out[e] + b_expert_out[e]).astype(jnp.float32)
    return y.astype(h.dtype).reshape(B, T, D)


def layer(x, c, pool_buf, ret_state, pos0, w_ada, b_ada, norm1_g, w_in, w_pool, pool_scale,
          w_up_pool, w_up_ret, w_out, norm2_g, w_router, b_router,
          w_expert_in, b_expert_in, w_expert_out, b_expert_out):
    mod = jax.nn.silu(c) @ w_ada + b_ada
    sh1, sc1, gt1, sh2, sc2, gt2 = jnp.split(mod[:, None, :], N_MOD, axis=-1)
    h = rmsnorm(x, norm1_g) * (1.0 + sc1) + sh1
    mix, new_buf, new_state = token_mixer(h, pool_buf, ret_state, pos0, w_in, w_pool, pool_scale,
                                          w_up_pool, w_up_ret, w_out)
    x = x + gt1 * mix
    h = rmsnorm(x, norm2_g) * (1.0 + sc2) + sh2
    x = x + gt2 * moe(h, w_router, b_router, w_expert_in, b_expert_in, w_expert_out, b_expert_out)
    return x, new_buf, new_state


def setup_inputs(seed: int = 0) -> dict:
    key = jax.random.key(seed)
    ks = jax.random.split(key, 24)
    f32 = jnp.float32
    D, E, F = D_MODEL, N_EXPERTS, D_EXPERT

    def nrm(k, shape, scale):
        return jax.random.normal(k, shape, f32) * scale

    return {
        "x_prompt": nrm(ks[0], (BATCH, SEQ, D), 1.0),
        "x_sample": nrm(ks[1], (DEC_BATCH, DEC_SEQ, D), 1.0),
        "c_prompt": nrm(ks[2], (BATCH, D), 1.0),
        "c_sample": nrm(ks[3], (DEC_BATCH, D), 1.0),
        "state_pool": nrm(ks[4], (DEPTH, DEC_BATCH, POOL_BUF, POOL_WIDTH), 1.0),
        "state_ret": nrm(ks[5], (DEPTH, DEC_BATCH, RET_HEADS, RET_QK_DIM, RET_V_DIM), 1.0),
        "w_ada": nrm(ks[6], (DEPTH, D, N_MOD * D), 0.5 * D ** -0.5),
        "b_ada": nrm(ks[7], (DEPTH, N_MOD * D), 0.02),
        "norm1_g": 1.0 + nrm(ks[8], (DEPTH, D), 0.1),
        "w_in": nrm(ks[9], (DEPTH, D, IN_WIDTH), D ** -0.5),
        "w_pool": nrm(ks[10], (DEPTH, POOL_GROUPS, POOL_GROUP_WIDTH, POOL_GROUP_WIDTH), POOL_GROUP_WIDTH ** -0.5),
        "pool_scale": 1.0 + nrm(ks[11], (DEPTH, POOL_WIDTH), 0.1),
        "w_up_pool": nrm(ks[12], (DEPTH, POOL_WIDTH, D), POOL_WIDTH ** -0.5),
        "w_up_ret": nrm(ks[13], (DEPTH, RET_V_WIDTH, D), RET_V_WIDTH ** -0.5),
        "w_out": nrm(ks[14], (DEPTH, D, D), D ** -0.5),
        "norm2_g": 1.0 + nrm(ks[15], (DEPTH, D), 0.1),
        "w_router": nrm(ks[16], (DEPTH, D, E), D ** -0.5),
        "b_router": nrm(ks[17], (DEPTH, E), 0.01),
        "w_expert_in": nrm(ks[18], (DEPTH, E, D, 2 * F), D ** -0.5),
        "b_expert_in": nrm(ks[19], (DEPTH, E, 2 * F), 0.01),
        "w_expert_out": nrm(ks[20], (DEPTH, E, F, D), F ** -0.5),
        "b_expert_out": nrm(ks[21], (DEPTH, E, D), 0.01),
        "final_norm_g": 1.0 + nrm(ks[22], (D,), 0.1),
    }


def reference(x_prompt, x_sample, c_prompt, c_sample, state_pool, state_ret,
              w_ada, b_ada, norm1_g, w_in, w_pool, pool_scale, w_up_pool, w_up_ret, w_out,
              norm2_g, w_router, b_router, w_expert_in, b_expert_in, w_expert_out, b_expert_out,
              final_norm_g):
    xp, xs = x_prompt, x_sample
    pool_p, ret_p, pool_s, ret_s = [], [], [], []
    for l in range(DEPTH):
        weights = (w_ada[l], b_ada[l], norm1_g[l], w_in[l], w_pool[l], pool_scale[l],
                   w_up_pool[l], w_up_ret[l], w_out[l], norm2_g[l], w_router[l], b_router[l],
                   w_expert_in[l], b_expert_in[l], w_expert_out[l], b_expert_out[l])
        buf0 = jnp.zeros((xp.shape[0], POOL_BUF, POOL_WIDTH), jnp.float32)
        S0 = jnp.zeros((xp.shape[0], RET_HEADS, RET_QK_DIM, RET_V_DIM), jnp.float32)
        xp, nb_p, ns_p = layer(xp, c_prompt, buf0, S0, 0, *weights)
        xs, nb_s, ns_s = layer(xs, c_sample, state_pool[l], state_ret[l], PAST_LEN, *weights)
        pool_p.append(nb_p.astype(x_prompt.dtype))
        ret_p.append(ns_p.astype(x_prompt.dtype))
        pool_s.append(nb_s.astype(state_pool.dtype))
        ret_s.append(ns_s.astype(state_ret.dtype))
    y_prompt = rmsnorm(xp, final_norm_g)
    y_sample = rmsnorm(xs, final_norm_g)
    new_pool_prompt = jnp.stack(pool_p, axis=0)
    new_ret_prompt = jnp.stack(ret_p, axis=0)
    new_pool_sample = jnp.stack(pool_s, axis=0)
    new_ret_sample = jnp.stack(ret_s, axis=0)
    return (y_prompt, y_sample, new_pool_prompt, new_ret_prompt, new_pool_sample, new_ret_sample)
```

```python
import functools

import numpy as np
import jax
import jax.numpy as jnp
from jax import lax
from jax.experimental import pallas as pl
from jax.experimental.pallas import tpu as pltpu

F32 = jnp.float32
BF16 = jnp.bfloat16

POOL_WINDOWS = (2, 4, 8, 16)
POOL_BUF = max(POOL_WINDOWS) - 1
RET_HEADS = 4
RET_CHUNK = 128
ROPE_BASE = 10000.0
PAST_LEN = 16384
TOP_K = 4
SWIGLU_LIMIT = 7.0
SWIGLU_ALPHA = 1.702
N_MOD = 6
EPS = 1e-6
LANES = 128
MOE_ROW_TILE = 512
VMEM_LIMIT = 56 * 1024 * 1024


def _cparams(sem):
    return pltpu.CompilerParams(dimension_semantics=sem, vmem_limit_bytes=VMEM_LIMIT)


def _dot(a, b):
    return jnp.dot(a, b, preferred_element_type=F32)


def _dot3(a, b):
    a_hi = a.astype(BF16)
    a_lo = (a - a_hi.astype(F32)).astype(BF16)
    b_hi = b.astype(BF16)
    b_lo = (b - b_hi.astype(F32)).astype(BF16)
    return _dot(a_hi, b_hi) + _dot(a_hi, b_lo) + _dot(a_lo, b_hi)


def _sigmoid(x):
    return 1.0 / (1.0 + jnp.exp(-x))


def _rms(x):
    return x * lax.rsqrt(jnp.mean(x * x, axis=-1, keepdims=True) + EPS)


def _mod_kernel(c_ref, w_ref, b_ref, o_ref):
    c = c_ref[...]
    o_ref[...] = _dot3(c * _sigmoid(c), w_ref[...]) + b_ref[...]


def _modulation(c, w_ada, b_ada):
    rows, d = c.shape
    n = w_ada.shape[1]
    tn = d
    return pl.pallas_call(
        _mod_kernel,
        grid=(n // tn,),
        in_specs=[pl.BlockSpec((rows, d), lambda j: (0, 0)),
                  pl.BlockSpec((d, tn), lambda j: (0, j)),
                  pl.BlockSpec((1, tn), lambda j: (0, j))],
        out_specs=pl.BlockSpec((rows, tn), lambda j: (0, j)),
        out_shape=jax.ShapeDtypeStruct((rows, n), F32),
        compiler_params=_cparams(("arbitrary",)),
        name="adaln_modulation",
    )(c, w_ada, b_ada.reshape(1, n))


def _mod_rows(mod_ref, which, d, per_row, b):
    if per_row:
        return mod_ref[:, which * d:(which + 1) * d]
    return mod_ref[pl.ds(b, 1), which * d:(which + 1) * d]


def _in_proj_kernel(x_ref, mod_ref, g_ref, w_ref, o_ref, h_ref, *, per_row, tiles_per_seq):
    i = pl.program_id(0)
    j = pl.program_id(1)
    d = x_ref.shape[1]

    @pl.when(j == 0)
    def _():
        b = i // tiles_per_seq
        sh = _mod_rows(mod_ref, 0, d, per_row, b)
        sc = _mod_rows(mod_ref, 1, d, per_row, b)
        h = _rms(x_ref[...]) * g_ref[...] * (1.0 + sc) + sh
        h_ref[...] = h.astype(BF16)

    o_ref[...] = _dot(h_ref[...], w_ref[...])


def _in_proj(x, mod, g, w_bf, *, tm, per_row, tiles_per_seq):
    n, d = x.shape
    width = w_bf.shape[1]
    tn = d
    kern = functools.partial(_in_proj_kernel, per_row=per_row, tiles_per_seq=tiles_per_seq)
    mod_spec = (pl.BlockSpec((tm, mod.shape[1]), lambda i, j: (i, 0)) if per_row
                else pl.BlockSpec(mod.shape, lambda i, j: (0, 0)))
    return pl.pallas_call(
        kern,
        grid=(n // tm, width // tn),
        in_specs=[pl.BlockSpec((tm, d), lambda i, j: (i, 0)),
                  mod_spec,
                  pl.BlockSpec((1, d), lambda i, j: (0, 0)),
                  pl.BlockSpec((d, tn), lambda i, j: (0, j))],
        out_specs=pl.BlockSpec((tm, tn), lambda i, j: (i, j)),
        out_shape=jax.ShapeDtypeStruct((n, width), F32),
        scratch_shapes=[pltpu.VMEM((tm, d), BF16)],
        compiler_params=_cparams(("arbitrary", "arbitrary")),
        name="in_proj_per_row" if per_row else "in_proj",
    )(x, mod, g.reshape(1, d), w_bf)


def _pool_prompt_kernel(u_ref, w_ref, s_ref, y_ref, tail_ref, prev_ref, *, tt):
    t = pl.program_id(1)
    nt = pl.num_programs(1)
    hist = prev_ref.shape[0]
    gw = w_ref.shape[1]

    @pl.when(t == 0)
    def _():
        prev_ref[...] = jnp.zeros_like(prev_ref)

    u = u_ref[...]
    ext = jnp.concatenate([prev_ref[...], u], axis=0)
    pos = t * tt + lax.broadcasted_iota(jnp.int32, (tt, 1), 0)
    run = ext
    span = 1
    outs = []
    for gi, w in enumerate(POOL_WINDOWS):
        while span < w:
            run = run + pltpu.roll(run, span, 0)
            span *= 2
        cols = slice(gi * gw, (gi + 1) * gw)
        cnt = jnp.minimum(w, pos + 1).astype(F32)
        pooled = run[hist:, cols] / cnt - u[:, cols]
        outs.append(_dot(pooled.astype(BF16), w_ref[gi].astype(BF16)))
    y = jnp.concatenate(outs, axis=-1) * s_ref[...]
    y_ref[...] = y.astype(BF16)
    prev_ref[...] = u[tt - hist:, :]

    @pl.when(t == nt - 1)
    def _():
        tail_ref[0] = u[tt - hist:, :]


def _pool_prompt(proj, w_pool, pool_scale, *, batch, seq, tt):
    p = pool_scale.shape[0]
    hist = POOL_BUF + 1
    nt = seq // tt
    kern = functools.partial(_pool_prompt_kernel, tt=tt)
    return pl.pallas_call(
        kern,
        grid=(batch, nt),
        in_specs=[pl.BlockSpec((tt, p), lambda b, t: (b * nt + t, 0)),
                  pl.BlockSpec(w_pool.shape, lambda b, t: (0, 0, 0)),
                  pl.BlockSpec((1, p), lambda b, t: (0, 0))],
        out_specs=[pl.BlockSpec((tt, p), lambda b, t: (b * nt + t, 0)),
                   pl.BlockSpec((1, hist, p), lambda b, t: (b, 0, 0))],
        out_shape=[jax.ShapeDtypeStruct((batch * seq, p), BF16),
                   jax.ShapeDtypeStruct((batch, hist, p), F32)],
        scratch_shapes=[pltpu.VMEM((hist, p), F32)],
        compiler_params=_cparams(("arbitrary", "arbitrary")),
        name="pool_prompt",
    )(proj, w_pool, pool_scale.reshape(1, p))


def _pool_decode_kernel(st_ref, u_ref, w_ref, s_ref, y_ref, new_ref):
    buf = st_ref.shape[0]
    gw = w_ref.shape[1]
    u = u_ref[...]
    outs = []
    run = u
    used = 0
    for gi, w in enumerate(POOL_WINDOWS):
        while used < w - 1:
            run = run + st_ref[buf - 1 - used]
            used += 1
        cols = slice(gi * gw, (gi + 1) * gw)
        pooled = run[:, cols] / float(min(w, PAST_LEN + 1)) - u[:, cols]
        outs.append(_dot(pooled.astype(BF16), w_ref[gi].astype(BF16)))
    y_ref[...] = (jnp.concatenate(outs, axis=-1) * s_ref[...]).astype(BF16)
    for r in range(buf - 1):
        new_ref[r] = st_ref[r + 1]
    new_ref[buf - 1] = u


def _pool_decode(state_t, proj_s, w_pool, pool_scale):
    buf, n, p = state_t.shape
    return pl.pallas_call(
        _pool_decode_kernel,
        grid=(1,),
        in_specs=[pl.BlockSpec((buf, n, p), lambda i: (0, 0, 0)),
                  pl.BlockSpec((n, p), lambda i: (0, 0)),
                  pl.BlockSpec(w_pool.shape, lambda i: (0, 0, 0)),
                  pl.BlockSpec((1, p), lambda i: (0, 0))],
        out_specs=[pl.BlockSpec((n, p), lambda i: (0, 0)),
                   pl.BlockSpec((buf, n, p), lambda i: (0, 0, 0))],
        out_shape=[jax.ShapeDtypeStruct((n, p), BF16),
                   jax.ShapeDtypeStruct((buf, n, p), F32)],
        compiler_params=_cparams(("arbitrary",)),
        name="pool_decode",
    )(state_t, proj_s, w_pool, pool_scale.reshape(1, p))


def _log_gamma(h):
    return float(np.log(1.0 - 2.0 ** (-5.0 - h)))


def _rope_tables(pos, half):
    inv = np.power(ROPE_BASE, -np.arange(half, dtype=np.float64) / half)
    ang = np.asarray(pos, np.float64)[:, None] * inv[None, :]
    return jnp.asarray(np.cos(ang), F32), jnp.asarray(np.sin(ang), F32)


def _rotary(x, cos, sin):
    half = x.shape[-1] // 2
    x1, x2 = x[:, :half], x[:, half:]
    return jnp.concatenate([x1 * cos - x2 * sin, x2 * cos + x1 * sin], axis=-1)


def _ret_tables(c):
    i = np.arange(c, dtype=np.float64)
    diff = i[:, None] - i[None, :]
    dec, qs, ks = [], [], []
    for h in range(RET_HEADS):
        lg = _log_gamma(h)
        dec.append(np.where(diff >= 0, np.exp(lg * np.maximum(diff, 0.0)), 0.0))
        qs.append(np.broadcast_to(np.exp(lg * (i + 1.0))[:, None], (c, LANES)))
        ks.append(np.broadcast_to(np.exp(lg * (c - 1.0 - i))[:, None], (c, LANES)))
    return (jnp.asarray(np.stack(dec), F32), jnp.asarray(np.stack(qs), F32), jnp.asarray(np.stack(ks), F32))


def _ret_prompt_kernel(q_ref, k_ref, v_ref, g_ref, cos_ref, sin_ref, dec_ref, qs_ref, ks_ref,
                       y_ref, s_out_ref, s_ref, *, chunk_decay):
    h = pl.program_id(1)
    c = pl.program_id(2)
    nc = pl.num_programs(2)
    dk = q_ref.shape[1]

    @pl.when(c == 0)
    def _():
        s_ref[...] = jnp.zeros_like(s_ref)

    cos, sin = cos_ref[...], sin_ref[...]
    q = _rotary(q_ref[...], cos, sin)
    k = _rotary(k_ref[...], cos, sin) * (dk ** -0.5)
    v = v_ref[...].astype(BF16)
    reps = dk // LANES
    qs = jnp.concatenate([qs_ref[0]] * reps, axis=-1)
    ks = jnp.concatenate([ks_ref[0]] * reps, axis=-1)
    scores = lax.dot_general(q.astype(BF16), k.astype(BF16), (((1,), (1,)), ((), ())),
                             preferred_element_type=F32) * dec_ref[0]
    state = s_ref[...]
    o = _dot(scores.astype(BF16), v) + _dot((q * qs).astype(BF16), state.astype(BF16))
    gamma_c = jnp.where(h == 0, chunk_decay[0],
                        jnp.where(h == 1, chunk_decay[1], jnp.where(h == 2, chunk_decay[2], chunk_decay[3])))
    kv = lax.dot_general((k * ks).astype(BF16), v, (((0,), (0,)), ((), ())), preferred_element_type=F32)
    new_state = gamma_c * state + kv
    s_ref[...] = new_state
    g = g_ref[...]
    y_ref[...] = (g * _sigmoid(g) * _rms(o)).astype(BF16)

    @pl.when(c == nc - 1)
    def _():
        s_out_ref[0, 0] = new_state


def _ret_prompt(proj, *, batch, seq, d_model, chunk):
    heads = RET_HEADS
    dk = d_model // heads
    dv = 2 * d_model // heads
    nc = seq // chunk
    cos, sin = _rope_tables(np.arange(seq), dk // 2)
    dec, qs, ks = _ret_tables(chunk)
    chunk_decay = tuple(float(np.exp(_log_gamma(h) * chunk)) for h in range(heads))
    q0, k0, v0, g0 = d_model // dk, 2 * d_model // dk, 3 * d_model // dv, 5 * d_model // dv
    kern = functools.partial(_ret_prompt_kernel, chunk_decay=chunk_decay)
    return pl.pallas_call(
        kern,
        grid=(batch, heads, nc),
        in_specs=[pl.BlockSpec((chunk, dk), lambda b, h, c: (b * nc + c, q0 + h)),
                  pl.BlockSpec((chunk, dk), lambda b, h, c: (b * nc + c, k0 + h)),
                  pl.BlockSpec((chunk, dv), lambda b, h, c: (b * nc + c, v0 + h)),
                  pl.BlockSpec((chunk, dv), lambda b, h, c: (b * nc + c, g0 + h)),
                  pl.BlockSpec((chunk, dk // 2), lambda b, h, c: (c, 0)),
                  pl.BlockSpec((chunk, dk // 2), lambda b, h, c: (c, 0)),
                  pl.BlockSpec((1, chunk, chunk), lambda b, h, c: (h, 0, 0)),
                  pl.BlockSpec((1, chunk, LANES), lambda b, h, c: (h, 0, 0)),
                  pl.BlockSpec((1, chunk, LANES), lambda b, h, c: (h, 0, 0))],
        out_specs=[pl.BlockSpec((chunk, dv), lambda b, h, c: (b * nc + c, h)),
                   pl.BlockSpec((1, 1, dk, dv), lambda b, h, c: (b, h, 0, 0))],
        out_shape=[jax.ShapeDtypeStruct((batch * seq, heads * dv), BF16),
                   jax.ShapeDtypeStruct((batch, heads, dk, dv), F32)],
        scratch_shapes=[pltpu.VMEM((dk, dv), F32)],
        compiler_params=_cparams(("arbitrary", "arbitrary", "arbitrary")),
        name="retention_prompt",
    )(proj, proj, proj, proj, cos, sin, dec, qs, ks)


def _ret_decode_kernel(q_ref, k_ref, v_ref, g_ref, cos_ref, sin_ref, s_ref, y_ref, s_out_ref,
                       kt_ref, qd_ref, o_ref, *, gammas, bs):
    h = pl.program_id(0)
    blk = pl.program_id(1)
    nblk = pl.num_programs(1)
    n, dk = q_ref.shape
    gamma = jnp.where(h == 0, gammas[0], jnp.where(h == 1, gammas[1], jnp.where(h == 2, gammas[2], gammas[3])))

    @pl.when(blk == 0)
    def _():
        cos, sin = cos_ref[...], sin_ref[...]
        q = _rotary(q_ref[...], cos, sin)
        k = _rotary(k_ref[...], cos, sin) * (dk ** -0.5)
        kt_ref[...] = k.T
        qd_ref[...] = (q * gamma).astype(BF16)
        o_ref[...] = jnp.sum(q * k, axis=-1, keepdims=True) * v_ref[...]

    vb = v_ref[...].astype(BF16)
    rows = lax.broadcasted_iota(jnp.int32, (n, 1), 0)
    lanes = lax.broadcasted_iota(jnp.int32, (1, n), 1)
    for i in range(bs):
        tok = blk * bs + i
        state = s_ref[i, 0]
        o_ref[...] += jnp.where(rows == tok, _dot(qd_ref[...], state.astype(BF16)), 0.0)
        k_col = jnp.where(lanes == tok, kt_ref[...], 0.0).astype(BF16)
        s_out_ref[i, 0] = gamma * state + _dot(k_col, vb)

    @pl.when(blk == nblk - 1)
    def _():
        g = g_ref[...]
        y_ref[...] = (g * _sigmoid(g) * _rms(o_ref[...])).astype(BF16)


def _ret_decode(proj_s, state, *, d_model, bs):
    n = proj_s.shape[0]
    heads = RET_HEADS
    dk = d_model // heads
    dv = 2 * d_model // heads
    cos, sin = _rope_tables(np.array([PAST_LEN]), dk // 2)
    gammas = tuple(float(np.exp(_log_gamma(h))) for h in range(heads))
    q0, k0, v0, g0 = d_model // dk, 2 * d_model // dk, 3 * d_model // dv, 5 * d_model // dv
    kern = functools.partial(_ret_decode_kernel, gammas=gammas, bs=bs)
    return pl.pallas_call(
        kern,
        grid=(heads, n // bs),
        in_specs=[pl.BlockSpec((n, dk), lambda h, b: (0, q0 + h)),
                  pl.BlockSpec((n, dk), lambda h, b: (0, k0 + h)),
                  pl.BlockSpec((n, dv), lambda h, b: (0, v0 + h)),
                  pl.BlockSpec((n, dv), lambda h, b: (0, g0 + h)),
                  pl.BlockSpec((1, dk // 2), lambda h, b: (0, 0)),
                  pl.BlockSpec((1, dk // 2), lambda h, b: (0, 0)),
                  pl.BlockSpec((bs, 1, dk, dv), lambda h, b: (b, h, 0, 0))],
        out_specs=[pl.BlockSpec((n, dv), lambda h, b: (0, h)),
                   pl.BlockSpec((bs, 1, dk, dv), lambda h, b: (b, h, 0, 0))],
        out_shape=[jax.ShapeDtypeStruct((n, heads * dv), BF16),
                   jax.ShapeDtypeStruct(state.shape, F32)],
        scratch_shapes=[pltpu.VMEM((dk, n), F32), pltpu.VMEM((n, dk), BF16), pltpu.VMEM((n, dv), F32)],
        compiler_params=_cparams(("arbitrary", "arbitrary")),
        name="retention_decode",
    )(proj_s, proj_s, proj_s, proj_s, cos, sin, state)


def _mix_kernel(x_ref, ya_ref, yb_ref, ga_ref, gb_ref, mod_ref, g2_ref, wa_ref, wb_ref, wo_ref,
                wr_ref, br_ref, cnt_in_ref,
                x1_ref, h2_ref, route_ref, cnt_ref, *, per_row, tiles_per_seq, n_experts):
    i = pl.program_id(0)
    tm, d = x_ref.shape
    b = i // tiles_per_seq

    @pl.when(i == 0)
    def _():
        cnt_ref[...] = cnt_in_ref[...]

    merged = (_sigmoid(ga_ref[...]) * _dot(ya_ref[...], wa_ref[...])
              + _sigmoid(gb_ref[...]) * _dot(yb_ref[...], wb_ref[...]))
    mix = _dot(merged.astype(BF16), wo_ref[...])
    x1 = x_ref[...] + _mod_rows(mod_ref, 2, d, per_row, b) * mix
    x1_ref[...] = x1
    h2 = (_rms(x1) * g2_ref[...] * (1.0 + _mod_rows(mod_ref, 4, d, per_row, b))
          + _mod_rows(mod_ref, 3, d, per_row, b))
    h2_ref[...] = h2

    logits = _dot3(h2, wr_ref[...]) + br_ref[...]
    lane = lax.broadcasted_iota(jnp.int32, (tm, LANES), 1)
    neg = jnp.float32(-jnp.inf)
    work = jnp.where(lane < n_experts, logits, neg)
    vals, idxs = [], []
    for _ in range(TOP_K):
        m = jnp.max(work, axis=-1, keepdims=True)
        idx = jnp.min(jnp.where(work == m, lane, LANES), axis=-1, keepdims=True)
        vals.append(m)
        idxs.append(idx)
        work = jnp.where(lane == idx, neg, work)
    exps = [jnp.exp(v - vals[0]) for v in vals]
    denom = exps[0] + exps[1] + exps[2] + exps[3]
    sel = jnp.zeros((tm, LANES), F32)
    for idx in idxs:
        sel = sel + (lane == idx).astype(F32)
    r_i = lax.broadcasted_iota(jnp.int32, (tm, tm), 0)
    c_i = lax.broadcasted_iota(jnp.int32, (tm, tm), 1)
    lower = (c_i < r_i).astype(BF16)
    before = _dot(lower, sel.astype(BF16)) + cnt_ref[...]
    route = jnp.zeros((tm, LANES), F32)
    for kk in range(TOP_K):
        rank = jnp.sum(jnp.where(lane == idxs[kk], before, 0.0), axis=-1, keepdims=True)
        route = jnp.where(lane == kk, idxs[kk].astype(F32), route)
        route = jnp.where(lane == TOP_K + kk, rank, route)
        route = jnp.where(lane == 2 * TOP_K + kk, exps[kk] / denom, route)
    route_ref[...] = route
    cnt_ref[...] = cnt_ref[...] + jnp.sum(sel, axis=0, keepdims=True)


def _mix(x, ya, yb, proj, mod, g2, wa_bf, wb_bf, wo_bf, wr_pad, br_pad, cnt_in, *, tm, per_row,
         tiles_per_seq, n_experts):
    n, d = x.shape
    ga_blk = 7 * d // d
    gb_blk = 8 * d // d
    kern = functools.partial(_mix_kernel, per_row=per_row, tiles_per_seq=tiles_per_seq, n_experts=n_experts)
    mod_spec = (pl.BlockSpec((tm, mod.shape[1]), lambda i: (i, 0)) if per_row
                else pl.BlockSpec(mod.shape, lambda i: (0, 0)))
    full = lambda a: pl.BlockSpec(a.shape, lambda i: (0,) * a.ndim)
    return pl.pallas_call(
        kern,
        grid=(n // tm,),
        in_specs=[pl.BlockSpec((tm, d), lambda i: (i, 0)),
                  pl.BlockSpec((tm, ya.shape[1]), lambda i: (i, 0)),
                  pl.BlockSpec((tm, yb.shape[1]), lambda i: (i, 0)),
                  pl.BlockSpec((tm, d), lambda i: (i, ga_blk)),
                  pl.BlockSpec((tm, d), lambda i: (i, gb_blk)),
                  mod_spec,
                  pl.BlockSpec((1, d), lambda i: (0, 0)),
                  full(wa_bf), full(wb_bf), full(wo_bf), full(wr_pad), full(br_pad), full(cnt_in)],
        out_specs=[pl.BlockSpec((tm, d), lambda i: (i, 0)),
                   pl.BlockSpec((tm, d), lambda i: (i, 0)),
                   pl.BlockSpec((tm, LANES), lambda i: (i, 0)),
                   pl.BlockSpec((1, LANES), lambda i: (0, 0))],
        out_shape=[jax.ShapeDtypeStruct((n, d), F32),
                   jax.ShapeDtypeStruct((n, d), F32),
                   jax.ShapeDtypeStruct((n, LANES), F32),
                   jax.ShapeDtypeStruct((1, LANES), F32)],
        compiler_params=_cparams(("arbitrary",)),
        name="mix_route_per_row" if per_row else "mix_route",
    )(x, ya, yb, proj, proj, mod, g2.reshape(1, d), wa_bf, wb_bf, wo_bf, wr_pad, br_pad, cnt_in)


def _moe_kernel(tile_expert_ref, n_active_ref, src_ref, h_hbm, wi_ref, bi_ref, wo_ref, bo_ref,
                y_ref, xbuf, wi_bf, wo_bf, sem):
    j = pl.program_id(0)
    tm, d = xbuf.shape
    f = wo_ref.shape[1]

    @pl.when(j < n_active_ref[0])
    def _():
        def row_copy(r, tok):
            return pltpu.make_async_copy(h_hbm.at[pl.ds(tok, 1), :], xbuf.at[pl.ds(r, 1), :], sem)

        def issue(r, carry):
            row_copy(r, src_ref[0, 0, r]).start()
            return carry

        lax.fori_loop(0, tm, issue, 0)
        prev = tile_expert_ref[jnp.maximum(j - 1, 0)]

        @pl.when((j == 0) | (tile_expert_ref[j] != prev))
        def _():
            wi_bf[...] = wi_ref[0].astype(BF16)
            wo_bf[...] = wo_ref[0].astype(BF16)

        def drain(r, carry):
            row_copy(r, 0).wait()
            return carry

        lax.fori_loop(0, tm, drain, 0)
        hh = _dot(xbuf[...].astype(BF16), wi_bf[...]) + bi_ref[0]
        gate = jnp.minimum(hh[:, :f], SWIGLU_LIMIT)
        lin = jnp.clip(hh[:, f:], -SWIGLU_LIMIT, SWIGLU_LIMIT)
        act = (lin + 1.0) * gate * _sigmoid(SWIGLU_ALPHA * gate)
        y_ref[...] = _dot(act.astype(BF16), wo_bf[...]) + bo_ref[0]

    @pl.when(j >= n_active_ref[0])
    def _():
        y_ref[...] = jnp.zeros_like(y_ref)


def _moe(h_all, src_tok, tile_expert, n_active, w_in, b_in, w_out, b_out, *, tm):
    e, d, f2 = w_in.shape
    f = w_out.shape[1]
    rows = src_tok.shape[0]
    n_tiles = rows // tm
    grid_spec = pltpu.PrefetchScalarGridSpec(
        num_scalar_prefetch=2,
        grid=(n_tiles,),
        in_specs=[pl.BlockSpec((1, 1, tm), lambda j, te, na: (j, 0, 0), memory_space=pltpu.SMEM),
                  pl.BlockSpec(memory_space=pl.ANY),
                  pl.BlockSpec((1, d, f2), lambda j, te, na: (te[j], 0, 0)),
                  pl.BlockSpec((1, 1, f2), lambda j, te, na: (te[j], 0, 0)),
                  pl.BlockSpec((1, f, d), lambda j, te, na: (te[j], 0, 0)),
                  pl.BlockSpec((1, 1, d), lambda j, te, na: (te[j], 0, 0))],
        out_specs=pl.BlockSpec((tm, d), lambda j, te, na: (j, 0)),
        scratch_shapes=[pltpu.VMEM((tm, d), F32),
                        pltpu.VMEM((d, f2), BF16),
                        pltpu.VMEM((f, d), BF16),
                        pltpu.SemaphoreType.DMA],
    )
    return pl.pallas_call(
        _moe_kernel,
        grid_spec=grid_spec,
        out_shape=jax.ShapeDtypeStruct((rows, d), F32),
        compiler_params=_cparams(("arbitrary",)),
        name="moe_experts",
    )(tile_expert, n_active, src_tok.reshape(n_tiles, 1, tm), h_all, w_in, b_in.reshape(e, 1, f2),
      w_out, b_out.reshape(e, 1, d))


def _final_kernel(dest_ref, x1_ref, route_ref, mod_ref, gf_ref, y_hbm, o_ref, ybuf, sem,
                  *, per_row, tiles_per_seq):
    i = pl.program_id(0)
    tm, d = x1_ref.shape
    b = i // tiles_per_seq

    def row_copy(kk, r, row):
        return pltpu.make_async_copy(y_hbm.at[pl.ds(row, 1), :], ybuf.at[kk, pl.ds(r, 1), :], sem)

    def issue(r, carry):
        for kk in range(TOP_K):
            row_copy(kk, r, dest_ref[0, kk, r]).start()
        return carry

    lax.fori_loop(0, tm, issue, 0)

    def drain(r, carry):
        for kk in range(TOP_K):
            row_copy(kk, r, 0).wait()
        return carry

    lax.fori_loop(0, tm, drain, 0)
    route = route_ref[...]
    moe = jnp.zeros((tm, d), F32)
    for kk in range(TOP_K):
        moe = moe + route[:, 2 * TOP_K + kk:2 * TOP_K + kk + 1] * ybuf[kk]
    x2 = x1_ref[...] + _mod_rows(mod_ref, 5, d, per_row, b) * moe
    o_ref[...] = _rms(x2) * gf_ref[...]


def _final(dest, x1, route, mod, gf, y_sorted, *, tm, per_row, tiles_per_seq):
    n, d = x1.shape
    n_tiles = n // tm
    kern = functools.partial(_final_kernel, per_row=per_row, tiles_per_seq=tiles_per_seq)
    mod_spec = (pl.BlockSpec((tm, mod.shape[1]), lambda i: (i, 0)) if per_row
                else pl.BlockSpec(mod.shape, lambda i: (0, 0)))
    dest_t = dest.reshape(n_tiles, tm, TOP_K).transpose(0, 2, 1)
    return pl.pallas_call(
        kern,
        grid=(n_tiles,),
        in_specs=[pl.BlockSpec((1, TOP_K, tm), lambda i: (i, 0, 0), memory_space=pltpu.SMEM),
                  pl.BlockSpec((tm, d), lambda i: (i, 0)),
                  pl.BlockSpec((tm, LANES), lambda i: (i, 0)),
                  mod_spec,
                  pl.BlockSpec((1, d), lambda i: (0, 0)),
                  pl.BlockSpec(memory_space=pl.ANY)],
        out_specs=pl.BlockSpec((tm, d), lambda i: (i, 0)),
        out_shape=jax.ShapeDtypeStruct((n, d), F32),
        scratch_shapes=[pltpu.VMEM((TOP_K, tm, d), F32), pltpu.SemaphoreType.DMA],
        compiler_params=_cparams(("arbitrary",)),
        name="combine_final_per_row" if per_row else "combine_final",
    )(dest_t, x1, route, mod, gf.reshape(1, d), y_sorted)


def kernel(x_prompt, x_sample, c_prompt, c_sample, state_pool, state_ret, w_ada, b_ada, norm1_g, w_in,
           w_pool, pool_scale, w_up_pool, w_up_ret, w_out, norm2_g, w_router, b_router, w_expert_in,
           b_expert_in, w_expert_out, b_expert_out, final_norm_g):
    batch, seq, d = x_prompt.shape
    n_dec = x_sample.shape[0]
    n_prompt = batch * seq
    n_experts = w_router.shape[-1]
    xp = x_prompt.reshape(n_prompt, d)
    xs = x_sample.reshape(n_dec, d)

    w_in_bf = w_in[0].astype(BF16)
    wa_bf = w_up_pool[0].astype(BF16)
    wb_bf = w_up_ret[0].astype(BF16)
    wo_bf = w_out[0].astype(BF16)
    wr_pad = jnp.zeros((d, LANES), F32).at[:, :n_experts].set(w_router[0])
    br_pad = jnp.zeros((1, LANES), F32).at[0, :n_experts].set(b_router[0])

    c_all = jnp.concatenate([c_prompt, c_sample], axis=0)
    c_all = jnp.pad(c_all, ((0, -c_all.shape[0] % 16), (0, 0)))
    mod = _modulation(c_all, w_ada[0], b_ada[0])
    mod_p, mod_s = mod[:batch], mod[batch:batch + n_dec]

    tm = min(1024, seq)
    proj_p = _in_proj(xp, mod_p, norm1_g[0], w_in_bf, tm=tm, per_row=False, tiles_per_seq=seq // tm)
    proj_s = _in_proj(xs, mod_s, norm1_g[0], w_in_bf, tm=n_dec, per_row=True, tiles_per_seq=1)

    ya_p, tail_p = _pool_prompt(proj_p, w_pool[0], pool_scale[0], batch=batch, seq=seq, tt=min(256, seq))
    ya_s, pool_s_t = _pool_decode(jnp.transpose(state_pool[0], (1, 0, 2)), proj_s, w_pool[0], pool_scale[0])
    yb_p, ret_p = _ret_prompt(proj_p, batch=batch, seq=seq, d_model=d, chunk=RET_CHUNK)
    yb_s, ret_s = _ret_decode(proj_s, state_ret[0], d_model=d, bs=8)

    tmix = min(256, seq)
    cnt0 = jnp.zeros((1, LANES), F32)
    x1_p, h2_p, route_p, cnt_p = _mix(xp, ya_p, yb_p, proj_p, mod_p, norm2_g[0], wa_bf, wb_bf, wo_bf,
                                      wr_pad, br_pad, cnt0, tm=tmix, per_row=False,
                                      tiles_per_seq=seq // tmix, n_experts=n_experts)
    x1_s, h2_s, route_s, cnt_all = _mix(xs, ya_s, yb_s, proj_s, mod_s, norm2_g[0], wa_bf, wb_bf, wo_bf,
                                        wr_pad, br_pad, cnt_p, tm=n_dec, per_row=True,
                                        tiles_per_seq=1, n_experts=n_experts)

    tmo = MOE_ROW_TILE
    n_tok = n_prompt + n_dec
    route = jnp.concatenate([route_p, route_s], axis=0)
    e_idx = route[:, :TOP_K].astype(jnp.int32)
    rank = route[:, TOP_K:2 * TOP_K].astype(jnp.int32)
    counts = cnt_all[0, :n_experts].astype(jnp.int32)
    tiles_per_e = (counts + tmo - 1) // tmo
    tile_end = jnp.cumsum(tiles_per_e)
    offs = (tile_end - tiles_per_e) * tmo
    dest = offs[e_idx] + rank
    n_tiles = (n_tok * TOP_K) // tmo + n_experts
    n_active = tile_end[-1:]
    tile_ids = jnp.minimum(jnp.arange(n_tiles, dtype=jnp.int32), n_active[0] - 1)
    tile_expert = jnp.searchsorted(tile_end, tile_ids, side="right").astype(jnp.int32)
    tok_ids = jnp.broadcast_to(jnp.arange(n_tok, dtype=jnp.int32)[:, None], (n_tok, TOP_K))
    src_tok = jnp.zeros((n_tiles * tmo,), jnp.int32).at[dest.reshape(-1)].set(tok_ids.reshape(-1))

    h_all = jnp.concatenate([h2_p, h2_s], axis=0)
    y_sorted = _moe(h_all, src_tok, tile_expert, n_active.astype(jnp.int32), w_expert_in[0], b_expert_in[0],
                    w_expert_out[0], b_expert_out[0], tm=tmo)

    tf = min(256, seq)
    y_p = _final(dest[:n_prompt], x1_p, route_p, mod_p, final_norm_g, y_sorted, tm=tf, per_row=False,
                 tiles_per_seq=seq // tf)
    y_s = _final(dest[n_prompt:], x1_s, route_s, mod_s, final_norm_g, y_sorted, tm=n_dec, per_row=True,
                 tiles_per_seq=1)

    return (y_p.reshape(batch, seq, d),
            y_s.reshape(n_dec, 1, d),
            tail_p[None, :, 1:, :],
            ret_p[None],
            jnp.transpose(pool_s_t, (1, 0, 2))[None],
            ret_s[None])
```

```python
import functools

import numpy as np
import jax
import jax.numpy as jnp
from jax import lax
from jax.experimental import pallas as pl
from jax.experimental.pallas import tpu as pltpu

F32 = jnp.float32
BF16 = jnp.bfloat16

POOL_WINDOWS = (2, 4, 8, 16)
POOL_BUF = max(POOL_WINDOWS) - 1
RET_HEADS = 4
RET_CHUNK = 128
ROPE_BASE = 10000.0
PAST_LEN = 16384
TOP_K = 4
SWIGLU_LIMIT = 7.0
SWIGLU_ALPHA = 1.702
N_MOD = 6
EPS = 1e-6
LANES = 128
ROW_ALIGN = 16
SORT_BLOCK = 256
TOKEN_TILE = 512
MOE_ROW_TILE = 512
VMEM_LIMIT = 56 * 1024 * 1024


def _cparams(sem):
    return pltpu.CompilerParams(dimension_semantics=sem, vmem_limit_bytes=VMEM_LIMIT)


def _dot(a, b):
    return jnp.dot(a, b, preferred_element_type=F32)


def _dot3(a, b):
    a_hi = a.astype(BF16)
    a_lo = (a - a_hi.astype(F32)).astype(BF16)
    b_hi = b.astype(BF16)
    b_lo = (b - b_hi.astype(F32)).astype(BF16)
    return _dot(a_hi, b_hi) + _dot(a_hi, b_lo) + _dot(a_lo, b_hi)


def _sigmoid(x):
    return 1.0 / (1.0 + jnp.exp(-x))


def _rms(x):
    return x * lax.rsqrt(jnp.mean(x * x, axis=-1, keepdims=True) + EPS)


def _round_up(x, m):
    return (x + m - 1) // m * m


def _mod_kernel(c_ref, w_ref, b_ref, o_ref):
    c = c_ref[...]
    o_ref[...] = _dot3(c * _sigmoid(c), w_ref[...]) + b_ref[...]


def _modulation(c, w_ada, b_ada):
    rows, d = c.shape
    n = w_ada.shape[1]
    tn = d
    return pl.pallas_call(
        _mod_kernel,
        grid=(n // tn,),
        in_specs=[pl.BlockSpec((rows, d), lambda j: (0, 0)),
                  pl.BlockSpec((d, tn), lambda j: (0, j)),
                  pl.BlockSpec((1, tn), lambda j: (0, j))],
        out_specs=pl.BlockSpec((rows, tn), lambda j: (0, j)),
        out_shape=jax.ShapeDtypeStruct((rows, n), F32),
        compiler_params=_cparams(("arbitrary",)),
        name="adaln_modulation",
    )(c, w_ada, b_ada.reshape(1, n))


def _mod_rows(mod_ref, which, d, per_row, b):
    if per_row:
        return mod_ref[:, which * d:(which + 1) * d]
    return mod_ref[pl.ds(b, 1), which * d:(which + 1) * d]


def _in_proj_kernel(x_ref, mod_ref, g_ref, w_ref, o_ref, h_ref, *, per_row, tiles_per_seq):
    i = pl.program_id(0)
    j = pl.program_id(1)
    d = x_ref.shape[1]

    @pl.when(j == 0)
    def _():
        b = i // tiles_per_seq
        sh = _mod_rows(mod_ref, 0, d, per_row, b)
        sc = _mod_rows(mod_ref, 1, d, per_row, b)
        h = _rms(x_ref[...]) * g_ref[...] * (1.0 + sc) + sh
        h_ref[...] = h.astype(BF16)

    o_ref[...] = _dot(h_ref[...], w_ref[...])


def _in_proj(x, mod, g, w_bf, *, tm, per_row, tiles_per_seq):
    n, d = x.shape
    width = w_bf.shape[1]
    tn = d
    kern = functools.partial(_in_proj_kernel, per_row=per_row, tiles_per_seq=tiles_per_seq)
    mod_spec = (pl.BlockSpec((tm, mod.shape[1]), lambda i, j: (i, 0)) if per_row
                else pl.BlockSpec(mod.shape, lambda i, j: (0, 0)))
    return pl.pallas_call(
        kern,
        grid=(n // tm, width // tn),
        in_specs=[pl.BlockSpec((tm, d), lambda i, j: (i, 0)),
                  mod_spec,
                  pl.BlockSpec((1, d), lambda i, j: (0, 0)),
                  pl.BlockSpec((d, tn), lambda i, j: (0, j))],
        out_specs=pl.BlockSpec((tm, tn), lambda i, j: (i, j)),
        out_shape=jax.ShapeDtypeStruct((n, width), F32),
        scratch_shapes=[pltpu.VMEM((tm, d), BF16)],
        compiler_params=_cparams(("arbitrary", "arbitrary")),
        name="in_proj_per_row" if per_row else "in_proj",
    )(x, mod, g.reshape(1, d), w_bf)


def _pool_prompt_kernel(u_ref, w_ref, s_ref, y_ref, tail_ref, prev_ref, *, tt):
    t = pl.program_id(1)
    nt = pl.num_programs(1)
    hist = prev_ref.shape[0]
    gw = w_ref.shape[1]

    @pl.when(t == 0)
    def _():
        prev_ref[...] = jnp.zeros_like(prev_ref)

    u = u_ref[...]
    ext = jnp.concatenate([prev_ref[...], u], axis=0)
    pos = t * tt + lax.broadcasted_iota(jnp.int32, (tt, 1), 0)
    run = ext
    span = 1
    outs = []
    for gi, w in enumerate(POOL_WINDOWS):
        while span < w:
            run = run + pltpu.roll(run, span, 0)
            span *= 2
        cols = slice(gi * gw, (gi + 1) * gw)
        cnt = jnp.minimum(w, pos + 1).astype(F32)
        pooled = run[hist:, cols] / cnt - u[:, cols]
        outs.append(_dot(pooled.astype(BF16), w_ref[gi].astype(BF16)))
    y = jnp.concatenate(outs, axis=-1) * s_ref[...]
    y_ref[...] = y.astype(BF16)
    prev_ref[...] = u[tt - hist:, :]

    @pl.when(t == nt - 1)
    def _():
        tail_ref[0] = u[tt - hist:, :]


def _pool_prompt(proj, w_pool, pool_scale, *, batch, seq, tt):
    p = pool_scale.shape[0]
    hist = POOL_BUF + 1
    nt = seq // tt
    kern = functools.partial(_pool_prompt_kernel, tt=tt)
    return pl.pallas_call(
        kern,
        grid=(batch, nt),
        in_specs=[pl.BlockSpec((tt, p), lambda b, t: (b * nt + t, 0)),
                  pl.BlockSpec(w_pool.shape, lambda b, t: (0, 0, 0)),
                  pl.BlockSpec((1, p), lambda b, t: (0, 0))],
        out_specs=[pl.BlockSpec((tt, p), lambda b, t: (b * nt + t, 0)),
                   pl.BlockSpec((1, hist, p), lambda b, t: (b, 0, 0))],
        out_shape=[jax.ShapeDtypeStruct((batch * seq, p), BF16),
                   jax.ShapeDtypeStruct((batch, hist, p), F32)],
        scratch_shapes=[pltpu.VMEM((hist, p), F32)],
        compiler_params=_cparams(("arbitrary", "arbitrary")),
        name="pool_prompt",
    )(proj, w_pool, pool_scale.reshape(1, p))


def _pool_decode_kernel(st_ref, u_ref, w_ref, s_ref, y_ref, new_ref):
    buf = st_ref.shape[0]
    gw = w_ref.shape[1]
    u = u_ref[...]
    outs = []
    run = u
    used = 0
    for gi, w in enumerate(POOL_WINDOWS):
        while used < w - 1:
            run = run + st_ref[buf - 1 - used]
            used += 1
        cols = slice(gi * gw, (gi + 1) * gw)
        pooled = run[:, cols] / float(min(w, PAST_LEN + 1)) - u[:, cols]
        outs.append(_dot(pooled.astype(BF16), w_ref[gi].astype(BF16)))
    y_ref[...] = (jnp.concatenate(outs, axis=-1) * s_ref[...]).astype(BF16)
    for r in range(buf - 1):
        new_ref[r] = st_ref[r + 1]
    new_ref[buf - 1] = u


def _pool_decode(state_t, proj_s, w_pool, pool_scale):
    buf, n, p = state_t.shape
    return pl.pallas_call(
        _pool_decode_kernel,
        grid=(1,),
        in_specs=[pl.BlockSpec((buf, n, p), lambda i: (0, 0, 0)),
                  pl.BlockSpec((n, p), lambda i: (0, 0)),
                  pl.BlockSpec(w_pool.shape, lambda i: (0, 0, 0)),
                  pl.BlockSpec((1, p), lambda i: (0, 0))],
        out_specs=[pl.BlockSpec((n, p), lambda i: (0, 0)),
                   pl.BlockSpec((buf, n, p), lambda i: (0, 0, 0))],
        out_shape=[jax.ShapeDtypeStruct((n, p), BF16),
                   jax.ShapeDtypeStruct((buf, n, p), F32)],
        compiler_params=_cparams(("arbitrary",)),
        name="pool_decode",
    )(state_t, proj_s, w_pool, pool_scale.reshape(1, p))


def _log_gamma(h):
    return float(np.log(1.0 - 2.0 ** (-5.0 - h)))


def _rope_tables(pos, half):
    inv = np.power(ROPE_BASE, -np.arange(half, dtype=np.float64) / half)
    ang = np.asarray(pos, np.float64)[:, None] * inv[None, :]
    return jnp.asarray(np.cos(ang), F32), jnp.asarray(np.sin(ang), F32)


def _rotary(x, cos, sin):
    half = x.shape[-1] // 2
    x1, x2 = x[:, :half], x[:, half:]
    return jnp.concatenate([x1 * cos - x2 * sin, x2 * cos + x1 * sin], axis=-1)


def _ret_tables(c):
    i = np.arange(c, dtype=np.float64)
    diff = i[:, None] - i[None, :]
    dec, qs, ks = [], [], []
    for h in range(RET_HEADS):
        lg = _log_gamma(h)
        dec.append(np.where(diff >= 0, np.exp(lg * np.maximum(diff, 0.0)), 0.0))
        qs.append(np.broadcast_to(np.exp(lg * (i + 1.0))[:, None], (c, LANES)))
        ks.append(np.broadcast_to(np.exp(lg * (c - 1.0 - i))[:, None], (c, LANES)))
    return (jnp.asarray(np.stack(dec), F32), jnp.asarray(np.stack(qs), F32), jnp.asarray(np.stack(ks), F32))


def _ret_prompt_kernel(q_ref, k_ref, v_ref, g_ref, cos_ref, sin_ref, dec_ref, qs_ref, ks_ref,
                       y_ref, s_out_ref, s_ref, *, chunk_decay):
    h = pl.program_id(1)
    c = pl.program_id(2)
    nc = pl.num_programs(2)
    dk = q_ref.shape[1]

    @pl.when(c == 0)
    def _():
        s_ref[...] = jnp.zeros_like(s_ref)

    cos, sin = cos_ref[...], sin_ref[...]
    q = _rotary(q_ref[...], cos, sin)
    k = _rotary(k_ref[...], cos, sin) * (dk ** -0.5)
    v = v_ref[...].astype(BF16)
    reps = dk // LANES
    qs = jnp.concatenate([qs_ref[0]] * reps, axis=-1)
    ks = jnp.concatenate([ks_ref[0]] * reps, axis=-1)
    scores = lax.dot_general(q.astype(BF16), k.astype(BF16), (((1,), (1,)), ((), ())),
                             preferred_element_type=F32) * dec_ref[0]
    state = s_ref[...]
    o = _dot(scores.astype(BF16), v) + _dot((q * qs).astype(BF16), state.astype(BF16))
    gamma_c = jnp.where(h == 0, chunk_decay[0],
                        jnp.where(h == 1, chunk_decay[1], jnp.where(h == 2, chunk_decay[2], chunk_decay[3])))
    kv = lax.dot_general((k * ks).astype(BF16), v, (((0,), (0,)), ((), ())), preferred_element_type=F32)
    new_state = gamma_c * state + kv
    s_ref[...] = new_state
    g = g_ref[...]
    y_ref[...] = (g * _sigmoid(g) * _rms(o)).astype(BF16)

    @pl.when(c == nc - 1)
    def _():
        s_out_ref[0, 0] = new_state


def _ret_prompt(proj, *, batch, seq, d_model, chunk):
    heads = RET_HEADS
    dk = d_model // heads
    dv = 2 * d_model // heads
    nc = seq // chunk
    cos, sin = _rope_tables(np.arange(seq), dk // 2)
    dec, qs, ks = _ret_tables(chunk)
    chunk_decay = tuple(float(np.exp(_log_gamma(h) * chunk)) for h in range(heads))
    q0, k0, v0, g0 = d_model // dk, 2 * d_model // dk, 3 * d_model // dv, 5 * d_model // dv
    kern = functools.partial(_ret_prompt_kernel, chunk_decay=chunk_decay)
    return pl.pallas_call(
        kern,
        grid=(batch, heads, nc),
        in_specs=[pl.BlockSpec((chunk, dk), lambda b, h, c: (b * nc + c, q0 + h)),
                  pl.BlockSpec((chunk, dk), lambda b, h, c: (b * nc + c, k0 + h)),
                  pl.BlockSpec((chunk, dv), lambda b, h, c: (b * nc + c, v0 + h)),
                  pl.BlockSpec((chunk, dv), lambda b, h, c: (b * nc + c, g0 + h)),
                  pl.BlockSpec((chunk, dk // 2), lambda b, h, c: (c, 0)),
                  pl.BlockSpec((chunk, dk // 2), lambda b, h, c: (c, 0)),
                  pl.BlockSpec((1, chunk, chunk), lambda b, h, c: (h, 0, 0)),
                  pl.BlockSpec((1, chunk, LANES), lambda b, h, c: (h, 0, 0)),
                  pl.BlockSpec((1, chunk, LANES), lambda b, h, c: (h, 0, 0))],
        out_specs=[pl.BlockSpec((chunk, dv), lambda b, h, c: (b * nc + c, h)),
                   pl.BlockSpec((1, 1, dk, dv), lambda b, h, c: (b, h, 0, 0))],
        out_shape=[jax.ShapeDtypeStruct((batch * seq, heads * dv), BF16),
                   jax.ShapeDtypeStruct((batch, heads, dk, dv), F32)],
        scratch_shapes=[pltpu.VMEM((dk, dv), F32)],
        compiler_params=_cparams(("arbitrary", "arbitrary", "arbitrary")),
        name="retention_prompt",
    )(proj, proj, proj, proj, cos, sin, dec, qs, ks)


def _ret_decode_kernel(q_ref, k_ref, v_ref, g_ref, cos_ref, sin_ref, s_ref, y_ref, s_out_ref,
                       kt_ref, qd_ref, o_ref, *, gammas, bs):
    h = pl.program_id(0)
    blk = pl.program_id(1)
    nblk = pl.num_programs(1)
    n, dk = q_ref.shape
    gamma = jnp.where(h == 0, gammas[0], jnp.where(h == 1, gammas[1], jnp.where(h == 2, gammas[2], gammas[3])))

    @pl.when(blk == 0)
    def _():
        cos, sin = cos_ref[...], sin_ref[...]
        q = _rotary(q_ref[...], cos, sin)
        k = _rotary(k_ref[...], cos, sin) * (dk ** -0.5)
        kt_ref[...] = k.T
        qd_ref[...] = (q * gamma).astype(BF16)
        o_ref[...] = jnp.sum(q * k, axis=-1, keepdims=True) * v_ref[...]

    vb = v_ref[...].astype(BF16)
    rows = lax.broadcasted_iota(jnp.int32, (n, 1), 0)
    lanes = lax.broadcasted_iota(jnp.int32, (1, n), 1)
    for i in range(bs):
        tok = blk * bs + i
        state = s_ref[i, 0]
        o_ref[...] += jnp.where(rows == tok, _dot(qd_ref[...], state.astype(BF16)), 0.0)
        k_col = jnp.where(lanes == tok, kt_ref[...], 0.0).astype(BF16)
        s_out_ref[i, 0] = gamma * state + _dot(k_col, vb)

    @pl.when(blk == nblk - 1)
    def _():
        g = g_ref[...]
        y_ref[...] = (g * _sigmoid(g) * _rms(o_ref[...])).astype(BF16)


def _ret_decode(proj_s, state, *, d_model, bs):
    n = proj_s.shape[0]
    heads = RET_HEADS
    dk = d_model // heads
    dv = 2 * d_model // heads
    cos, sin = _rope_tables(np.array([PAST_LEN]), dk // 2)
    gammas = tuple(float(np.exp(_log_gamma(h))) for h in range(heads))
    q0, k0, v0, g0 = d_model // dk, 2 * d_model // dk, 3 * d_model // dv, 5 * d_model // dv
    kern = functools.partial(_ret_decode_kernel, gammas=gammas, bs=bs)
    return pl.pallas_call(
        kern,
        grid=(heads, n // bs),
        in_specs=[pl.BlockSpec((n, dk), lambda h, b: (0, q0 + h)),
                  pl.BlockSpec((n, dk), lambda h, b: (0, k0 + h)),
                  pl.BlockSpec((n, dv), lambda h, b: (0, v0 + h)),
                  pl.BlockSpec((n, dv), lambda h, b: (0, g0 + h)),
                  pl.BlockSpec((1, dk // 2), lambda h, b: (0, 0)),
                  pl.BlockSpec((1, dk // 2), lambda h, b: (0, 0)),
                  pl.BlockSpec((bs, 1, dk, dv), lambda h, b: (b, h, 0, 0))],
        out_specs=[pl.BlockSpec((n, dv), lambda h, b: (0, h)),
                   pl.BlockSpec((bs, 1, dk, dv), lambda h, b: (b, h, 0, 0))],
        out_shape=[jax.ShapeDtypeStruct((n, heads * dv), BF16),
                   jax.ShapeDtypeStruct(state.shape, F32)],
        scratch_shapes=[pltpu.VMEM((dk, n), F32), pltpu.VMEM((n, dk), BF16), pltpu.VMEM((n, dv), F32)],
        compiler_params=_cparams(("arbitrary", "arbitrary")),
        name="retention_decode",
    )(proj_s, proj_s, proj_s, proj_s, cos, sin, state)


def _mix_kernel(x_ref, ya_ref, yb_ref, ga_ref, gb_ref, mod_ref, g2_ref, wa_ref, wb_ref, wo_ref,
                wr_ref, br_ref, x1_ref, h2_ref, route_ref, *, per_row, tiles_per_seq, n_experts):
    i = pl.program_id(0)
    tm, d = x_ref.shape
    b = i // tiles_per_seq

    merged = (_sigmoid(ga_ref[...]) * _dot(ya_ref[...], wa_ref[...])
              + _sigmoid(gb_ref[...]) * _dot(yb_ref[...], wb_ref[...]))
    mix = _dot(merged.astype(BF16), wo_ref[...])
    x1 = x_ref[...] + _mod_rows(mod_ref, 2, d, per_row, b) * mix
    x1_ref[...] = x1
    h2 = (_rms(x1) * g2_ref[...] * (1.0 + _mod_rows(mod_ref, 4, d, per_row, b))
          + _mod_rows(mod_ref, 3, d, per_row, b))
    h2_ref[...] = h2.astype(BF16)

    logits = _dot3(h2, wr_ref[...]) + br_ref[...]
    lane = lax.broadcasted_iota(jnp.int32, (tm, LANES), 1)
    neg = jnp.float32(-jnp.inf)
    work = jnp.where(lane < n_experts, logits, neg)
    vals, idxs = [], []
    for _ in range(TOP_K):
        m = jnp.max(work, axis=-1, keepdims=True)
        idx = jnp.min(jnp.where(work == m, lane, LANES), axis=-1, keepdims=True)
        vals.append(m)
        idxs.append(idx)
        work = jnp.where(lane == idx, neg, work)
    exps = [jnp.exp(v - vals[0]) for v in vals]
    denom = exps[0] + exps[1] + exps[2] + exps[3]
    route = jnp.zeros((tm, LANES), F32)
    for kk in range(TOP_K):
        route = jnp.where(lane == kk, idxs[kk].astype(F32), route)
        route = jnp.where(lane == TOP_K + kk, exps[kk] / denom, route)
    route_ref[...] = route


def _mix(x, ya, yb, proj, mod, g2, wa_bf, wb_bf, wo_bf, wr_pad, br_pad, *, tm, per_row, tiles_per_seq,
         n_experts):
    n, d = x.shape
    ga_blk = 7
    gb_blk = 8
    kern = functools.partial(_mix_kernel, per_row=per_row, tiles_per_seq=tiles_per_seq, n_experts=n_experts)
    mod_spec = (pl.BlockSpec((tm, mod.shape[1]), lambda i: (i, 0)) if per_row
                else pl.BlockSpec(mod.shape, lambda i: (0, 0)))
    full = lambda a: pl.BlockSpec(a.shape, lambda i: (0,) * a.ndim)
    return pl.pallas_call(
        kern,
        grid=(n // tm,),
        in_specs=[pl.BlockSpec((tm, d), lambda i: (i, 0)),
                  pl.BlockSpec((tm, ya.shape[1]), lambda i: (i, 0)),
                  pl.BlockSpec((tm, yb.shape[1]), lambda i: (i, 0)),
                  pl.BlockSpec((tm, d), lambda i: (i, ga_blk)),
                  pl.BlockSpec((tm, d), lambda i: (i, gb_blk)),
                  mod_spec,
                  pl.BlockSpec((1, d), lambda i: (0, 0)),
                  full(wa_bf), full(wb_bf), full(wo_bf), full(wr_pad), full(br_pad)],
        out_specs=[pl.BlockSpec((tm, d), lambda i: (i, 0)),
                   pl.BlockSpec((tm, d), lambda i: (i, 0)),
                   pl.BlockSpec((tm, LANES), lambda i: (i, 0))],
        out_shape=[jax.ShapeDtypeStruct((n, d), F32),
                   jax.ShapeDtypeStruct((n, d), BF16),
                   jax.ShapeDtypeStruct((n, LANES), F32)],
        compiler_params=_cparams(("arbitrary",)),
        name="mix_route_per_row" if per_row else "mix_route",
    )(x, ya, yb, proj, proj, mod, g2.reshape(1, d), wa_bf, wb_bf, wo_bf, wr_pad, br_pad)


def _sorted_rows(tt, n_experts):
    return _round_up(tt * TOP_K + n_experts * (ROW_ALIGN - 1) + ROW_ALIGN, SORT_BLOCK)


def _dispatch_kernel(hp_ref, rp_ref, hd_ref, rd_ref, xs_ref, r2_ref, cnt_ref, *, lmax):
    tt = hp_ref.shape[0]
    is_dec = pl.program_id(0) == pl.num_programs(0) - 1
    route = jnp.where(is_dec, rd_ref[...], rp_ref[...])
    lane = lax.broadcasted_iota(jnp.int32, (tt, LANES), 1)
    lane_f = lane.astype(F32)
    hits = [lane_f == route[:, kk:kk + 1] for kk in range(TOP_K)]
    sel = jnp.zeros((tt, LANES), F32)
    for hit in hits:
        sel = sel + jnp.where(hit, 1.0, 0.0)
    cnt = jnp.sum(sel, axis=0, keepdims=True)
    seg = jnp.ceil(cnt / ROW_ALIGN) * ROW_ALIGN
    a_i = lax.broadcasted_iota(jnp.int32, (LANES, LANES), 0)
    b_i = lax.broadcasted_iota(jnp.int32, (LANES, LANES), 1)
    upper = jnp.where(a_i < b_i, 1.0, 0.0)
    off = _dot3(jnp.broadcast_to(seg, (8, LANES)), upper)[0:1]
    r_i = lax.broadcasted_iota(jnp.int32, (tt, tt), 0)
    c_i = lax.broadcasted_iota(jnp.int32, (tt, tt), 1)
    lower = jnp.where(c_i < r_i, 1.0, 0.0).astype(BF16)
    place = _dot(lower, sel.astype(BF16)) + off
    r2 = jnp.zeros((tt, LANES), F32)
    for kk in range(TOP_K):
        pos = jnp.sum(jnp.where(hits[kk], place, 0.0), axis=-1, keepdims=True)
        r2 = jnp.where(lane == kk, pos, r2)
        r2 = jnp.where(lane == TOP_K + kk, route[:, TOP_K + kk:TOP_K + kk + 1], r2)
    r2_ref[...] = r2
    cnt_ref[...] = jnp.broadcast_to(cnt, cnt_ref.shape)
    r2t = r2.T
    h = jnp.where(is_dec, hd_ref[...], hp_ref[...])
    for rb in range(lmax // SORT_BLOCK):
        rio = (lax.broadcasted_iota(jnp.int32, (SORT_BLOCK, tt), 0) + rb * SORT_BLOCK).astype(F32)
        hit = rio == r2t[0:1, :]
        for kk in range(1, TOP_K):
            hit = hit | (rio == r2t[kk:kk + 1, :])
        perm = jnp.where(hit, 1.0, 0.0).astype(BF16)
        xs_ref[rb * SORT_BLOCK:(rb + 1) * SORT_BLOCK, :] = _dot(perm, h).astype(BF16)


def _dispatch(h2_p, route_p, h2_d, route_d, *, tt, lmax):
    n, d = h2_p.shape
    p_tiles = n // tt
    tiles = p_tiles + 1
    pad = tt - h2_d.shape[0]
    h2_d = jnp.pad(h2_d, ((0, pad), (0, 0)))
    route_d = jnp.pad(route_d, ((0, pad), (0, 0)), constant_values=-1.0)
    kern = functools.partial(_dispatch_kernel, lmax=lmax)
    clamp = lambda i: (jnp.minimum(i, p_tiles - 1), 0)
    return pl.pallas_call(
        kern,
        grid=(tiles,),
        in_specs=[pl.BlockSpec((tt, d), clamp),
                  pl.BlockSpec((tt, LANES), clamp),
                  pl.BlockSpec((tt, d), lambda i: (0, 0)),
                  pl.BlockSpec((tt, LANES), lambda i: (0, 0))],
        out_specs=[pl.BlockSpec((lmax, d), lambda i: (i, 0)),
                   pl.BlockSpec((tt, LANES), lambda i: (i, 0)),
                   pl.BlockSpec((8, LANES), lambda i: (i, 0))],
        out_shape=[jax.ShapeDtypeStruct((tiles * lmax, d), BF16),
                   jax.ShapeDtypeStruct((tiles * tt, LANES), F32),
                   jax.ShapeDtypeStruct((tiles * 8, LANES), F32)],
        compiler_params=_cparams(("arbitrary",)),
        name="dispatch_sort",
    )(h2_p, route_p, h2_d, route_d)


def _routing_tables(cnt, tile_base, *, tmo, n_tiles, chunks_per_tile_max, zero_row):
    tiles, n_e = cnt.shape
    seg = (cnt + ROW_ALIGN - 1) // ROW_ALIGN * ROW_ALIGN
    segch = seg // ROW_ALIGN
    loc_end = jnp.cumsum(seg, axis=1)
    loc_off = loc_end - seg
    cum_incl = jnp.cumsum(segch, axis=0)
    cum_excl = cum_incl - segch
    rch = cum_incl[-1]
    cpt = tmo // ROW_ALIGN
    tpe = (rch + cpt - 1) // cpt
    tile_end = jnp.cumsum(tpe)
    tile_start = tile_end - tpe
    n_active = tile_end[-1:]
    jj = jnp.minimum(jnp.arange(n_tiles, dtype=jnp.int32), n_active[0] - 1)
    tile_expert = jnp.sum((tile_end[None, :] <= jj[:, None]).astype(jnp.int32), axis=1)
    q = (jj - tile_start[tile_expert])[:, None] * cpt + jnp.arange(cpt, dtype=jnp.int32)[None, :]
    ci = cum_incl.T[tile_expert]
    src_tile = jnp.minimum(jnp.sum((ci[:, None, :] <= q[:, :, None]).astype(jnp.int32), axis=-1), tiles - 1)
    e_b = jnp.broadcast_to(tile_expert[:, None], q.shape)
    src = tile_base[src_tile] + loc_off[src_tile, e_b] + (q - cum_excl[src_tile, e_b]) * ROW_ALIGN
    moe_src = jnp.where(q < rch[tile_expert][:, None], src, zero_row).astype(jnp.int32)
    c = jnp.arange(chunks_per_tile_max, dtype=jnp.int32)
    loc_end_ch = loc_end // ROW_ALIGN
    e_c = jnp.sum((loc_end_ch[:, None, :] <= c[None, :, None]).astype(jnp.int32), axis=-1)
    e_cc = jnp.minimum(e_c, n_e - 1)
    t_b = jnp.broadcast_to(jnp.arange(tiles, dtype=jnp.int32)[:, None], e_cc.shape)
    rows = tile_start[e_cc] * tmo + (cum_excl[t_b, e_cc] + c[None, :] - loc_off[t_b, e_cc] // ROW_ALIGN) * ROW_ALIGN
    comb_src = jnp.where(e_c < n_e, rows, 0).astype(jnp.int32)
    return tile_expert.astype(jnp.int32), n_active.astype(jnp.int32), moe_src.reshape(-1), comb_src.reshape(-1)


def _moe_kernel(tile_expert_ref, n_active_ref, src_ref, xs_hbm, wi_ref, bi_ref, wo_ref, bo_ref,
                y_ref, xbuf, wi_bf, wo_bf, sem):
    j = pl.program_id(0)
    tm = xbuf.shape[1]
    f = wo_ref.shape[1]
    cpt = tm // ROW_ALIGN
    slot = j % 2
    n_active = n_active_ref[0]

    def block_copy(tile, c, s):
        row = pl.multiple_of(src_ref[tile * cpt + c], ROW_ALIGN)
        return pltpu.make_async_copy(xs_hbm.at[pl.ds(row, ROW_ALIGN), :],
                                     xbuf.at[s, pl.ds(c * ROW_ALIGN, ROW_ALIGN), :], sem.at[s])

    @pl.when(j == 0)
    def _():
        for c in range(cpt):
            block_copy(0, c, 0).start()

    @pl.when(j + 1 < n_active)
    def _():
        for c in range(cpt):
            block_copy(j + 1, c, 1 - slot).start()

    @pl.when(j < n_active)
    def _():
        prev = tile_expert_ref[jnp.maximum(j - 1, 0)]

        @pl.when((j == 0) | (tile_expert_ref[j] != prev))
        def _():
            wi_bf[...] = wi_ref[0].astype(BF16)
            wo_bf[...] = wo_ref[0].astype(BF16)

        for c in range(cpt):
            block_copy(0, c, slot).wait()
        hh = _dot(xbuf[slot], wi_bf[...]) + bi_ref[0]
        gate = jnp.minimum(hh[:, :f], SWIGLU_LIMIT)
        lin = jnp.clip(hh[:, f:], -SWIGLU_LIMIT, SWIGLU_LIMIT)
        act = (lin + 1.0) * gate * _sigmoid(SWIGLU_ALPHA * gate)
        y_ref[...] = (_dot(act.astype(BF16), wo_bf[...]) + bo_ref[0]).astype(BF16)

    @pl.when(j >= n_active)
    def _():
        y_ref[...] = jnp.zeros_like(y_ref)


def _moe(xs, moe_src, tile_expert, n_active, w_in, b_in, w_out, b_out, *, tm, n_tiles):
    e, d, f2 = w_in.shape
    f = w_out.shape[1]
    grid_spec = pltpu.PrefetchScalarGridSpec(
        num_scalar_prefetch=3,
        grid=(n_tiles,),
        in_specs=[pl.BlockSpec(memory_space=pl.ANY),
                  pl.BlockSpec((1, d, f2), lambda j, te, na, src: (te[j], 0, 0)),
                  pl.BlockSpec((1, 1, f2), lambda j, te, na, src: (te[j], 0, 0)),
                  pl.BlockSpec((1, f, d), lambda j, te, na, src: (te[j], 0, 0)),
                  pl.BlockSpec((1, 1, d), lambda j, te, na, src: (te[j], 0, 0))],
        out_specs=pl.BlockSpec((tm, d), lambda j, te, na, src: (j, 0)),
        scratch_shapes=[pltpu.VMEM((2, tm, d), BF16),
                        pltpu.VMEM((d, f2), BF16),
                        pltpu.VMEM((f, d), BF16),
                        pltpu.SemaphoreType.DMA((2,))],
    )
    return pl.pallas_call(
        _moe_kernel,
        grid_spec=grid_spec,
        out_shape=jax.ShapeDtypeStruct((n_tiles * tm, d), BF16),
        compiler_params=_cparams(("arbitrary",)),
        name="moe_experts",
    )(tile_expert, n_active, moe_src, xs, w_in, b_in.reshape(e, 1, f2), w_out, b_out.reshape(e, 1, d))


def _combine_kernel(src_ref, x1_ref, r2_ref, mod_ref, gf_ref, y_hbm, o_ref, ybuf, sem,
                    *, per_row, tiles_per_seq, tile0, stride):
    i = pl.program_id(0)
    nt = pl.num_programs(0)
    tt, d = x1_ref.shape
    lmax = ybuf.shape[1]
    nch = lmax // ROW_ALIGN
    slot = i % 2
    b = i // tiles_per_seq

    def block_copy(tile, c, s):
        row = pl.multiple_of(src_ref[(tile0 + tile) * stride + c], ROW_ALIGN)
        return pltpu.make_async_copy(y_hbm.at[pl.ds(row, ROW_ALIGN), :],
                                     ybuf.at[s, pl.ds(c * ROW_ALIGN, ROW_ALIGN), :], sem.at[s])

    @pl.when(i == 0)
    def _():
        for c in range(nch):
            block_copy(0, c, 0).start()

    @pl.when(i + 1 < nt)
    def _():
        for c in range(nch):
            block_copy(i + 1, c, 1 - slot).start()

    for c in range(nch):
        block_copy(0, c, slot).wait()
    r2 = r2_ref[...]
    moe = jnp.zeros((tt, d), F32)
    for cb in range(lmax // SORT_BLOCK):
        col = (lax.broadcasted_iota(jnp.int32, (tt, SORT_BLOCK), 1) + cb * SORT_BLOCK).astype(F32)
        pw = jnp.zeros((tt, SORT_BLOCK), F32)
        for kk in range(TOP_K):
            pw = pw + jnp.where(col == r2[:, kk:kk + 1], r2[:, TOP_K + kk:TOP_K + kk + 1], 0.0)
        hi = pw.astype(BF16)
        lo = (pw - hi.astype(F32)).astype(BF16)
        yb = ybuf[slot, cb * SORT_BLOCK:(cb + 1) * SORT_BLOCK, :]
        moe = moe + _dot(hi, yb) + _dot(lo, yb)
    x2 = x1_ref[...] + _mod_rows(mod_ref, 5, d, per_row, b) * moe
    o_ref[...] = _rms(x2) * gf_ref[...]


def _combine(comb_src, x1, r2, mod, gf, ys, *, tt, lmax, per_row, tiles_per_seq, tile0, stride, r2_block0):
    n, d = x1.shape
    kern = functools.partial(_combine_kernel, per_row=per_row, tiles_per_seq=tiles_per_seq, tile0=tile0,
                             stride=stride)
    mod_spec = (pl.BlockSpec((tt, mod.shape[1]), lambda i, s: (i, 0)) if per_row
                else pl.BlockSpec(mod.shape, lambda i, s: (0, 0)))
    grid_spec = pltpu.PrefetchScalarGridSpec(
        num_scalar_prefetch=1,
        grid=(n // tt,),
        in_specs=[pl.BlockSpec((tt, d), lambda i, s: (i, 0)),
                  pl.BlockSpec((tt, LANES), lambda i, s: (r2_block0 + i, 0)),
                  mod_spec,
                  pl.BlockSpec((1, d), lambda i, s: (0, 0)),
                  pl.BlockSpec(memory_space=pl.ANY)],
        out_specs=pl.BlockSpec((tt, d), lambda i, s: (i, 0)),
        scratch_shapes=[pltpu.VMEM((2, lmax, d), BF16), pltpu.SemaphoreType.DMA((2,))],
    )
    return pl.pallas_call(
        kern,
        grid_spec=grid_spec,
        out_shape=jax.ShapeDtypeStruct((n, d), F32),
        compiler_params=_cparams(("arbitrary",)),
        name="combine_final_per_row" if per_row else "combine_final",
    )(comb_src, x1, r2, mod, gf.reshape(1, d), ys)


def kernel(x_prompt, x_sample, c_prompt, c_sample, state_pool, state_ret, w_ada, b_ada, norm1_g, w_in,
           w_pool, pool_scale, w_up_pool, w_up_ret, w_out, norm2_g, w_router, b_router, w_expert_in,
           b_expert_in, w_expert_out, b_expert_out, final_norm_g):
    batch, seq, d = x_prompt.shape
    n_dec = x_sample.shape[0]
    n_prompt = batch * seq
    n_experts = w_router.shape[-1]
    xp = x_prompt.reshape(n_prompt, d)
    xs = x_sample.reshape(n_dec, d)

    w_in_bf = w_in[0].astype(BF16)
    wa_bf = w_up_pool[0].astype(BF16)
    wb_bf = w_up_ret[0].astype(BF16)
    wo_bf = w_out[0].astype(BF16)
    wr_pad = jnp.zeros((d, LANES), F32).at[:, :n_experts].set(w_router[0])
    br_pad = jnp.zeros((1, LANES), F32).at[0, :n_experts].set(b_router[0])

    c_all = jnp.concatenate([c_prompt, c_sample], axis=0)
    c_all = jnp.pad(c_all, ((0, -c_all.shape[0] % 16), (0, 0)))
    mod = _modulation(c_all, w_ada[0], b_ada[0])
    mod_p, mod_s = mod[:batch], mod[batch:batch + n_dec]

    tm = min(1024, seq)
    proj_p = _in_proj(xp, mod_p, norm1_g[0], w_in_bf, tm=tm, per_row=False, tiles_per_seq=seq // tm)
    proj_s = _in_proj(xs, mod_s, norm1_g[0], w_in_bf, tm=n_dec, per_row=True, tiles_per_seq=1)

    ya_p, tail_p = _pool_prompt(proj_p, w_pool[0], pool_scale[0], batch=batch, seq=seq, tt=min(256, seq))
    ya_s, pool_s_t = _pool_decode(jnp.transpose(state_pool[0], (1, 0, 2)), proj_s, w_pool[0], pool_scale[0])
    yb_p, ret_p = _ret_prompt(proj_p, batch=batch, seq=seq, d_model=d, chunk=RET_CHUNK)
    yb_s, ret_s = _ret_decode(proj_s, state_ret[0], d_model=d, bs=8)

    tmix = min(256, seq)
    x1_p, h2_p, route_p = _mix(xp, ya_p, yb_p, proj_p, mod_p, norm2_g[0], wa_bf, wb_bf, wo_bf, wr_pad, br_pad,
                               tm=tmix, per_row=False, tiles_per_seq=seq // tmix, n_experts=n_experts)
    x1_s, h2_s, route_s = _mix(xs, ya_s, yb_s, proj_s, mod_s, norm2_g[0], wa_bf, wb_bf, wo_bf, wr_pad, br_pad,
                               tm=n_dec, per_row=True, tiles_per_seq=1, n_experts=n_experts)

    tt = min(TOKEN_TILE, seq)
    assert n_dec <= tt
    p_tiles = n_prompt // tt
    lmax = _sorted_rows(tt, n_experts)
    xs_buf, r2, cnt_t = _dispatch(h2_p, route_p, h2_s, route_s, tt=tt, lmax=lmax)

    tmo = MOE_ROW_TILE
    cnt = cnt_t.reshape(p_tiles + 1, 8, LANES)[:, 0, :n_experts].astype(jnp.int32)
    tile_base = jnp.arange(p_tiles + 1, dtype=jnp.int32) * lmax
    max_rows = (n_prompt + n_dec) * TOP_K + (p_tiles + 1) * n_experts * (ROW_ALIGN - 1)
    n_tiles = max_rows // tmo + n_experts
    stride = lmax // ROW_ALIGN
    tile_expert, n_active, moe_src, comb_src = _routing_tables(
        cnt, tile_base, tmo=tmo, n_tiles=n_tiles, chunks_per_tile_max=stride, zero_row=lmax - ROW_ALIGN)

    ys = _moe(xs_buf, moe_src, tile_expert, n_active, w_expert_in[0], b_expert_in[0], w_expert_out[0],
              b_expert_out[0], tm=tmo, n_tiles=n_tiles)

    assert n_prompt % n_dec == 0
    y_p = _combine(comb_src, x1_p, r2, mod_p, final_norm_g, ys, tt=tt, lmax=lmax, per_row=False,
                   tiles_per_seq=seq // tt, tile0=0, stride=stride, r2_block0=0)
    y_s = _combine(comb_src, x1_s, r2, mod_s, final_norm_g, ys, tt=n_dec, lmax=lmax, per_row=True,
                   tiles_per_seq=1, tile0=p_tiles, stride=stride, r2_block0=n_prompt // n_dec)

    return (y_p.reshape(batch, seq, d),
            y_s.reshape(n_dec, 1, d),
            tail_p[None, :, 1:, :],
            ret_p[None],
            jnp.transpose(pool_s_t, (1, 0, 2))[None],
            ret_s[None])
```

```python
import functools

import numpy as np
import jax
import jax.numpy as jnp
from jax import lax
from jax.experimental import pallas as pl
from jax.experimental.pallas import tpu as pltpu

F32 = jnp.float32
BF16 = jnp.bfloat16

POOL_WINDOWS = (2, 4, 8, 16)
POOL_BUF = max(POOL_WINDOWS) - 1
RET_HEADS = 4
RET_CHUNK = 256
ROPE_BASE = 10000.0
PAST_LEN = 16384
TOP_K = 4
SWIGLU_LIMIT = 7.0
SWIGLU_ALPHA = 1.702
N_MOD = 6
EPS = 1e-6
LANES = 128
ROW_ALIGN = 16
SORT_BLOCK = 256
TOKEN_TILE = 512
MOE_ROW_TILE = 512
VMEM_LIMIT = 56 * 1024 * 1024


def _cparams(sem):
    return pltpu.CompilerParams(dimension_semantics=sem, vmem_limit_bytes=VMEM_LIMIT)


def _dot(a, b):
    return jnp.dot(a, b, preferred_element_type=F32)


def _dot3(a, b):
    a_hi = a.astype(BF16)
    a_lo = (a - a_hi.astype(F32)).astype(BF16)
    b_hi = b.astype(BF16)
    b_lo = (b - b_hi.astype(F32)).astype(BF16)
    return _dot(a_hi, b_hi) + _dot(a_hi, b_lo) + _dot(a_lo, b_hi)


def _sigmoid(x):
    return 1.0 / (1.0 + jnp.exp(-x))


def _rms(x):
    return x * lax.rsqrt(jnp.mean(x * x, axis=-1, keepdims=True) + EPS)


def _round_up(x, m):
    return (x + m - 1) // m * m


def _mod_kernel(c_ref, w_ref, b_ref, o_ref):
    c = c_ref[...]
    o_ref[...] = _dot3(c * _sigmoid(c), w_ref[...]) + b_ref[...]


def _modulation(c, w_ada, b_ada):
    rows, d = c.shape
    n = w_ada.shape[1]
    tn = d
    return pl.pallas_call(
        _mod_kernel,
        grid=(n // tn,),
        in_specs=[pl.BlockSpec((rows, d), lambda j: (0, 0)),
                  pl.BlockSpec((d, tn), lambda j: (0, j)),
                  pl.BlockSpec((1, tn), lambda j: (0, j))],
        out_specs=pl.BlockSpec((rows, tn), lambda j: (0, j)),
        out_shape=jax.ShapeDtypeStruct((rows, n), F32),
        compiler_params=_cparams(("arbitrary",)),
        name="adaln_modulation",
    )(c, w_ada, b_ada.reshape(1, n))


def _mod_rows(mod_ref, which, d, per_row, b):
    if per_row:
        return mod_ref[:, which * d:(which + 1) * d]
    return mod_ref[pl.ds(b, 1), which * d:(which + 1) * d]


def _in_proj_kernel(x_ref, mod_ref, g_ref, w_ref, *rest, per_row, tiles_per_seq, split):
    i = pl.program_id(0)
    j = pl.program_id(1)
    d = x_ref.shape[1]
    h_ref = rest[-1]

    @pl.when(j == 0)
    def _():
        b = i // tiles_per_seq
        sh = _mod_rows(mod_ref, 0, d, per_row, b)
        sc = _mod_rows(mod_ref, 1, d, per_row, b)
        h = _rms(x_ref[...]) * g_ref[...] * (1.0 + sc) + sh
        h_ref[...] = h.astype(BF16)

    acc = _dot(h_ref[...], w_ref[...])
    if split:
        u_ref, o_ref = rest[:2]

        @pl.when(j == 0)
        def _():
            u_ref[...] = acc

        @pl.when(j > 0)
        def _():
            o_ref[...] = acc.astype(BF16)
    else:
        rest[0][...] = acc


def _in_proj(x, mod, g, w_bf, *, tm, per_row, tiles_per_seq, split):
    n, d = x.shape
    width = w_bf.shape[1]
    tn = d
    kern = functools.partial(_in_proj_kernel, per_row=per_row, tiles_per_seq=tiles_per_seq, split=split)
    mod_spec = (pl.BlockSpec((tm, mod.shape[1]), lambda i, j: (i, 0)) if per_row
                else pl.BlockSpec(mod.shape, lambda i, j: (0, 0)))
    if split:
        out_specs = [pl.BlockSpec((tm, tn), lambda i, j: (i, 0)),
                     pl.BlockSpec((tm, tn), lambda i, j: (i, jnp.maximum(j - 1, 0)))]
        out_shape = [jax.ShapeDtypeStruct((n, tn), F32), jax.ShapeDtypeStruct((n, width - tn), BF16)]
    else:
        out_specs = pl.BlockSpec((tm, tn), lambda i, j: (i, j))
        out_shape = jax.ShapeDtypeStruct((n, width), F32)
    return pl.pallas_call(
        kern,
        grid=(n // tm, width // tn),
        in_specs=[pl.BlockSpec((tm, d), lambda i, j: (i, 0)),
                  mod_spec,
                  pl.BlockSpec((1, d), lambda i, j: (0, 0)),
                  pl.BlockSpec((d, tn), lambda i, j: (0, j))],
        out_specs=out_specs,
        out_shape=out_shape,
        scratch_shapes=[pltpu.VMEM((tm, d), BF16)],
        compiler_params=_cparams(("arbitrary", "arbitrary")),
        name="in_proj_per_row" if per_row else "in_proj",
    )(x, mod, g.reshape(1, d), w_bf)


def _pool_prompt_kernel(u_ref, w_ref, s_ref, y_ref, tail_ref, prev_ref, *, tt):
    t = pl.program_id(1)
    nt = pl.num_programs(1)
    hist = prev_ref.shape[0]
    gw = w_ref.shape[1]

    @pl.when(t == 0)
    def _():
        prev_ref[...] = jnp.zeros_like(prev_ref)

    u = u_ref[...]
    ext = jnp.concatenate([prev_ref[...], u], axis=0)
    pos = t * tt + lax.broadcasted_iota(jnp.int32, (tt, 1), 0)
    run = ext
    span = 1
    outs = []
    for gi, w in enumerate(POOL_WINDOWS):
        while span < w:
            run = run + pltpu.roll(run, span, 0)
            span *= 2
        cols = slice(gi * gw, (gi + 1) * gw)
        cnt = jnp.minimum(w, pos + 1).astype(F32)
        pooled = run[hist:, cols] / cnt - u[:, cols]
        outs.append(_dot(pooled.astype(BF16), w_ref[gi].astype(BF16)))
    y = jnp.concatenate(outs, axis=-1) * s_ref[...]
    y_ref[...] = y.astype(BF16)
    prev_ref[...] = u[tt - hist:, :]

    @pl.when(t == nt - 1)
    def _():
        tail_ref[0] = u[tt - hist:, :]


def _pool_prompt(proj, w_pool, pool_scale, *, batch, seq, tt):
    p = pool_scale.shape[0]
    hist = POOL_BUF + 1
    nt = seq // tt
    kern = functools.partial(_pool_prompt_kernel, tt=tt)
    return pl.pallas_call(
        kern,
        grid=(batch, nt),
        in_specs=[pl.BlockSpec((tt, p), lambda b, t: (b * nt + t, 0)),
                  pl.BlockSpec(w_pool.shape, lambda b, t: (0, 0, 0)),
                  pl.BlockSpec((1, p), lambda b, t: (0, 0))],
        out_specs=[pl.BlockSpec((tt, p), lambda b, t: (b * nt + t, 0)),
                   pl.BlockSpec((1, hist, p), lambda b, t: (b, 0, 0))],
        out_shape=[jax.ShapeDtypeStruct((batch * seq, p), BF16),
                   jax.ShapeDtypeStruct((batch, hist, p), F32)],
        scratch_shapes=[pltpu.VMEM((hist, p), F32)],
        compiler_params=_cparams(("arbitrary", "arbitrary")),
        name="pool_prompt",
    )(proj, w_pool, pool_scale.reshape(1, p))


def _pool_decode_kernel(st_ref, u_ref, w_ref, s_ref, y_ref, new_ref):
    buf = st_ref.shape[0]
    gw = w_ref.shape[1]
    u = u_ref[...]
    outs = []
    run = u
    used = 0
    for gi, w in enumerate(POOL_WINDOWS):
        while used < w - 1:
            run = run + st_ref[buf - 1 - used]
            used += 1
        cols = slice(gi * gw, (gi + 1) * gw)
        pooled = run[:, cols] / float(min(w, PAST_LEN + 1)) - u[:, cols]
        outs.append(_dot(pooled.astype(BF16), w_ref[gi].astype(BF16)))
    y_ref[...] = (jnp.concatenate(outs, axis=-1) * s_ref[...]).astype(BF16)
    for r in range(buf - 1):
        new_ref[r] = st_ref[r + 1]
    new_ref[buf - 1] = u


def _pool_decode(state_t, proj_s, w_pool, pool_scale):
    buf, n, p = state_t.shape
    return pl.pallas_call(
        _pool_decode_kernel,
        grid=(1,),
        in_specs=[pl.BlockSpec((buf, n, p), lambda i: (0, 0, 0)),
                  pl.BlockSpec((n, p), lambda i: (0, 0)),
                  pl.BlockSpec(w_pool.shape, lambda i: (0, 0, 0)),
                  pl.BlockSpec((1, p), lambda i: (0, 0))],
        out_specs=[pl.BlockSpec((n, p), lambda i: (0, 0)),
                   pl.BlockSpec((buf, n, p), lambda i: (0, 0, 0))],
        out_shape=[jax.ShapeDtypeStruct((n, p), BF16),
                   jax.ShapeDtypeStruct((buf, n, p), F32)],
        compiler_params=_cparams(("arbitrary",)),
        name="pool_decode",
    )(state_t, proj_s, w_pool, pool_scale.reshape(1, p))


def _log_gamma(h):
    return float(np.log(1.0 - 2.0 ** (-5.0 - h)))


def _rope_tables(pos, half):
    inv = np.power(ROPE_BASE, -np.arange(half, dtype=np.float64) / half)
    ang = np.asarray(pos, np.float64)[:, None] * inv[None, :]
    return jnp.asarray(np.cos(ang), F32), jnp.asarray(np.sin(ang), F32)


def _rotary(x, cos, sin):
    half = x.shape[-1] // 2
    x1, x2 = x[:, :half], x[:, half:]
    return jnp.concatenate([x1 * cos - x2 * sin, x2 * cos + x1 * sin], axis=-1)


def _ret_tables(c):
    i = np.arange(c, dtype=np.float64)
    diff = i[:, None] - i[None, :]
    dec, qs, ks = [], [], []
    for h in range(RET_HEADS):
        lg = _log_gamma(h)
        dec.append(np.where(diff >= 0, np.exp(lg * np.maximum(diff, 0.0)), 0.0))
        qs.append(np.broadcast_to(np.exp(lg * (i + 1.0))[:, None], (c, LANES)))
        ks.append(np.broadcast_to(np.exp(lg * (c - 1.0 - i))[:, None], (c, LANES)))
    return (jnp.asarray(np.stack(dec), F32), jnp.asarray(np.stack(qs), F32), jnp.asarray(np.stack(ks), F32))


def _ret_prompt_kernel(q_ref, k_ref, v_ref, g_ref, cos_ref, sin_ref, dec_ref, qs_ref, ks_ref,
                       y_ref, s_out_ref, s_ref, *, chunk_decay):
    c = pl.program_id(1)
    nc = pl.num_programs(1)
    heads, dk, dv = s_ref.shape

    @pl.when(c == 0)
    def _():
        s_ref[...] = jnp.zeros_like(s_ref)

    cos, sin = cos_ref[...], sin_ref[...]
    reps = dk // LANES
    for h in range(heads):
        q = _rotary(q_ref[:, h * dk:(h + 1) * dk].astype(F32), cos, sin)
        k = _rotary(k_ref[:, h * dk:(h + 1) * dk].astype(F32), cos, sin) * (dk ** -0.5)
        v = v_ref[:, h * dv:(h + 1) * dv]
        qs = jnp.concatenate([qs_ref[h]] * reps, axis=-1)
        ks = jnp.concatenate([ks_ref[h]] * reps, axis=-1)
        scores = lax.dot_general(q.astype(BF16), k.astype(BF16), (((1,), (1,)), ((), ())),
                                 preferred_element_type=F32) * dec_ref[h]
        state = s_ref[h]
        o = _dot(scores.astype(BF16), v) + _dot((q * qs).astype(BF16), state.astype(BF16))
        kv = lax.dot_general((k * ks).astype(BF16), v, (((0,), (0,)), ((), ())), preferred_element_type=F32)
        new_state = chunk_decay[h] * state + kv
        s_ref[h] = new_state
        g = g_ref[:, h * dv:(h + 1) * dv].astype(F32)
        y_ref[:, h * dv:(h + 1) * dv] = (g * _sigmoid(g) * _rms(o)).astype(BF16)

        @pl.when(c == nc - 1)
        def _():
            s_out_ref[0, h] = new_state


def _ret_prompt(qkvg, *, batch, seq, d_model, chunk):
    heads = RET_HEADS
    dk = d_model // heads
    dv = 2 * d_model // heads
    nc = seq // chunk
    cos, sin = _rope_tables(np.arange(seq), dk // 2)
    dec, qs, ks = _ret_tables(chunk)
    chunk_decay = tuple(float(np.exp(_log_gamma(h) * chunk)) for h in range(heads))
    kern = functools.partial(_ret_prompt_kernel, chunk_decay=chunk_decay)
    const = lambda a: pl.BlockSpec(a.shape, lambda b, c: (0,) * a.ndim)
    return pl.pallas_call(
        kern,
        grid=(batch, nc),
        in_specs=[pl.BlockSpec((chunk, d_model), lambda b, c: (b * nc + c, 0)),
                  pl.BlockSpec((chunk, d_model), lambda b, c: (b * nc + c, 1)),
                  pl.BlockSpec((chunk, 2 * d_model), lambda b, c: (b * nc + c, 1)),
                  pl.BlockSpec((chunk, 2 * d_model), lambda b, c: (b * nc + c, 2)),
                  pl.BlockSpec((chunk, dk // 2), lambda b, c: (c, 0)),
                  pl.BlockSpec((chunk, dk // 2), lambda b, c: (c, 0)),
                  const(dec), const(qs), const(ks)],
        out_specs=[pl.BlockSpec((chunk, heads * dv), lambda b, c: (b * nc + c, 0)),
                   pl.BlockSpec((1, heads, dk, dv), lambda b, c: (b, 0, 0, 0))],
        out_shape=[jax.ShapeDtypeStruct((batch * seq, heads * dv), BF16),
                   jax.ShapeDtypeStruct((batch, heads, dk, dv), F32)],
        scratch_shapes=[pltpu.VMEM((heads, dk, dv), F32)],
        compiler_params=_cparams(("arbitrary", "arbitrary")),
        name="retention_prompt",
    )(qkvg, qkvg, qkvg, qkvg, cos, sin, dec, qs, ks)


def _ret_decode_kernel(q_ref, k_ref, v_ref, g_ref, cos_ref, sin_ref, s_ref, y_ref, s_out_ref,
                       kt_ref, qd_ref, o_ref, *, gammas, bs):
    h = pl.program_id(0)
    blk = pl.program_id(1)
    nblk = pl.num_programs(1)
    n, dk = q_ref.shape
    gamma = jnp.where(h == 0, gammas[0], jnp.where(h == 1, gammas[1], jnp.where(h == 2, gammas[2], gammas[3])))

    @pl.when(blk == 0)
    def _():
        cos, sin = cos_ref[...], sin_ref[...]
        q = _rotary(q_ref[...], cos, sin)
        k = _rotary(k_ref[...], cos, sin) * (dk ** -0.5)
        kt_ref[...] = k.T
        qd_ref[...] = (q * gamma).astype(BF16)
        o_ref[...] = jnp.sum(q * k, axis=-1, keepdims=True) * v_ref[...]

    vb = v_ref[...].astype(BF16)
    rows = lax.broadcasted_iota(jnp.int32, (n, 1), 0)
    lanes = lax.broadcasted_iota(jnp.int32, (1, n), 1)
    for i in range(bs):
        tok = blk * bs + i
        state = s_ref[i, 0]
        o_ref[...] += jnp.where(rows == tok, _dot(qd_ref[...], state.astype(BF16)), 0.0)
        k_col = jnp.where(lanes == tok, kt_ref[...], 0.0).astype(BF16)
        s_out_ref[i, 0] = gamma * state + _dot(k_col, vb)

    @pl.when(blk == nblk - 1)
    def _():
        g = g_ref[...]
        y_ref[...] = (g * _sigmoid(g) * _rms(o_ref[...])).astype(BF16)


def _ret_decode(proj_s, state, *, d_model, bs):
    n = proj_s.shape[0]
    heads = RET_HEADS
    dk = d_model // heads
    dv = 2 * d_model // heads
    cos, sin = _rope_tables(np.array([PAST_LEN]), dk // 2)
    gammas = tuple(float(np.exp(_log_gamma(h))) for h in range(heads))
    q0, k0, v0, g0 = d_model // dk, 2 * d_model // dk, 3 * d_model // dv, 5 * d_model // dv
    kern = functools.partial(_ret_decode_kernel, gammas=gammas, bs=bs)
    return pl.pallas_call(
        kern,
        grid=(heads, n // bs),
        in_specs=[pl.BlockSpec((n, dk), lambda h, b: (0, q0 + h)),
                  pl.BlockSpec((n, dk), lambda h, b: (0, k0 + h)),
                  pl.BlockSpec((n, dv), lambda h, b: (0, v0 + h)),
                  pl.BlockSpec((n, dv), lambda h, b: (0, g0 + h)),
                  pl.BlockSpec((1, dk // 2), lambda h, b: (0, 0)),
                  pl.BlockSpec((1, dk // 2), lambda h, b: (0, 0)),
                  pl.BlockSpec((bs, 1, dk, dv), lambda h, b: (b, h, 0, 0))],
        out_specs=[pl.BlockSpec((n, dv), lambda h, b: (0, h)),
                   pl.BlockSpec((bs, 1, dk, dv), lambda h, b: (b, h, 0, 0))],
        out_shape=[jax.ShapeDtypeStruct((n, heads * dv), BF16),
                   jax.ShapeDtypeStruct(state.shape, F32)],
        scratch_shapes=[pltpu.VMEM((dk, n), F32), pltpu.VMEM((n, dk), BF16), pltpu.VMEM((n, dv), F32)],
        compiler_params=_cparams(("arbitrary", "arbitrary")),
        name="retention_decode",
    )(proj_s, proj_s, proj_s, proj_s, cos, sin, state)


def _mix_kernel(x_ref, ya_ref, yb_ref, ga_ref, gb_ref, mod_ref, g2_ref, wa_ref, wb_ref, wo_ref,
                wr_ref, br_ref, x1_ref, h2_ref, route_ref, *, per_row, tiles_per_seq, n_experts):
    i = pl.program_id(0)
    tm, d = x_ref.shape
    b = i // tiles_per_seq

    merged = (_sigmoid(ga_ref[...].astype(F32)) * _dot(ya_ref[...], wa_ref[...])
              + _sigmoid(gb_ref[...].astype(F32)) * _dot(yb_ref[...], wb_ref[...]))
    mix = _dot(merged.astype(BF16), wo_ref[...])
    x1 = x_ref[...] + _mod_rows(mod_ref, 2, d, per_row, b) * mix
    x1_ref[...] = x1
    h2 = (_rms(x1) * g2_ref[...] * (1.0 + _mod_rows(mod_ref, 4, d, per_row, b))
          + _mod_rows(mod_ref, 3, d, per_row, b))
    h2_ref[...] = h2.astype(BF16)

    logits = _dot3(h2, wr_ref[...]) + br_ref[...]
    lane = lax.broadcasted_iota(jnp.int32, (tm, LANES), 1)
    neg = jnp.float32(-jnp.inf)
    work = jnp.where(lane < n_experts, logits, neg)
    vals, idxs = [], []
    for _ in range(TOP_K):
        m = jnp.max(work, axis=-1, keepdims=True)
        idx = jnp.min(jnp.where(work == m, lane, LANES), axis=-1, keepdims=True)
        vals.append(m)
        idxs.append(idx)
        work = jnp.where(lane == idx, neg, work)
    exps = [jnp.exp(v - vals[0]) for v in vals]
    denom = exps[0] + exps[1] + exps[2] + exps[3]
    route = jnp.zeros((tm, LANES), F32)
    for kk in range(TOP_K):
        route = jnp.where(lane == kk, idxs[kk].astype(F32), route)
        route = jnp.where(lane == TOP_K + kk, exps[kk] / denom, route)
    route_ref[...] = route


def _mix(x, ya, yb, proj, mod, g2, wa_bf, wb_bf, wo_bf, wr_pad, br_pad, *, tm, per_row, tiles_per_seq,
         n_experts, ga_blk):
    n, d = x.shape
    gb_blk = ga_blk + 1
    kern = functools.partial(_mix_kernel, per_row=per_row, tiles_per_seq=tiles_per_seq, n_experts=n_experts)
    mod_spec = (pl.BlockSpec((tm, mod.shape[1]), lambda i: (i, 0)) if per_row
                else pl.BlockSpec(mod.shape, lambda i: (0, 0)))
    full = lambda a: pl.BlockSpec(a.shape, lambda i: (0,) * a.ndim)
    return pl.pallas_call(
        kern,
        grid=(n // tm,),
        in_specs=[pl.BlockSpec((tm, d), lambda i: (i, 0)),
                  pl.BlockSpec((tm, ya.shape[1]), lambda i: (i, 0)),
                  pl.BlockSpec((tm, yb.shape[1]), lambda i: (i, 0)),
                  pl.BlockSpec((tm, d), lambda i: (i, ga_blk)),
                  pl.BlockSpec((tm, d), lambda i: (i, gb_blk)),
                  mod_spec,
                  pl.BlockSpec((1, d), lambda i: (0, 0)),
                  full(wa_bf), full(wb_bf), full(wo_bf), full(wr_pad), full(br_pad)],
        out_specs=[pl.BlockSpec((tm, d), lambda i: (i, 0)),
                   pl.BlockSpec((tm, d), lambda i: (i, 0)),
                   pl.BlockSpec((tm, LANES), lambda i: (i, 0))],
        out_shape=[jax.ShapeDtypeStruct((n, d), F32),
                   jax.ShapeDtypeStruct((n, d), BF16),
                   jax.ShapeDtypeStruct((n, LANES), F32)],
        compiler_params=_cparams(("arbitrary",)),
        name="mix_route_per_row" if per_row else "mix_route",
    )(x, ya, yb, proj, proj, mod, g2.reshape(1, d), wa_bf, wb_bf, wo_bf, wr_pad, br_pad)


def _sorted_rows(tt, n_experts):
    return _round_up(tt * TOP_K + n_experts * (ROW_ALIGN - 1) + ROW_ALIGN, SORT_BLOCK)


def _dispatch_kernel(hp_ref, rp_ref, hd_ref, rd_ref, xs_ref, r2_ref, cnt_ref, *, lmax):
    tt = hp_ref.shape[0]
    is_dec = pl.program_id(0) == pl.num_programs(0) - 1
    route = jnp.where(is_dec, rd_ref[...], rp_ref[...])
    lane = lax.broadcasted_iota(jnp.int32, (tt, LANES), 1)
    lane_f = lane.astype(F32)
    hits = [lane_f == route[:, kk:kk + 1] for kk in range(TOP_K)]
    sel = jnp.zeros((tt, LANES), F32)
    for hit in hits:
        sel = sel + jnp.where(hit, 1.0, 0.0)
    cnt = jnp.sum(sel, axis=0, keepdims=True)
    seg = jnp.ceil(cnt / ROW_ALIGN) * ROW_ALIGN
    a_i = lax.broadcasted_iota(jnp.int32, (LANES, LANES), 0)
    b_i = lax.broadcasted_iota(jnp.int32, (LANES, LANES), 1)
    upper = jnp.where(a_i < b_i, 1.0, 0.0)
    off = _dot3(jnp.broadcast_to(seg, (8, LANES)), upper)[0:1]
    r_i = lax.broadcasted_iota(jnp.int32, (tt, tt), 0)
    c_i = lax.broadcasted_iota(jnp.int32, (tt, tt), 1)
    lower = jnp.where(c_i < r_i, 1.0, 0.0).astype(BF16)
    place = _dot(lower, sel.astype(BF16)) + off
    r2 = jnp.zeros((tt, LANES), F32)
    for kk in range(TOP_K):
        pos = jnp.sum(jnp.where(hits[kk], place, 0.0), axis=-1, keepdims=True)
        r2 = jnp.where(lane == kk, pos, r2)
        r2 = jnp.where(lane == TOP_K + kk, route[:, TOP_K + kk:TOP_K + kk + 1], r2)
    r2_ref[...] = r2
    cnt_ref[...] = jnp.broadcast_to(cnt, cnt_ref.shape)
    r2t = r2.T
    h = jnp.where(is_dec, hd_ref[...], hp_ref[...])
    for rb in range(lmax // SORT_BLOCK):
        rio = (lax.broadcasted_iota(jnp.int32, (SORT_BLOCK, tt), 0) + rb * SORT_BLOCK).astype(F32)
        hit = rio == r2t[0:1, :]
        for kk in range(1, TOP_K):
            hit = hit | (rio == r2t[kk:kk + 1, :])
        perm = jnp.where(hit, 1.0, 0.0).astype(BF16)
        xs_ref[rb * SORT_BLOCK:(rb + 1) * SORT_BLOCK, :] = _dot(perm, h).astype(BF16)


def _dispatch(h2_p, route_p, h2_d, route_d, *, tt, lmax):
    n, d = h2_p.shape
    p_tiles = n // tt
    tiles = p_tiles + 1
    pad = tt - h2_d.shape[0]
    h2_d = jnp.pad(h2_d, ((0, pad), (0, 0)))
    route_d = jnp.pad(route_d, ((0, pad), (0, 0)), constant_values=-1.0)
    kern = functools.partial(_dispatch_kernel, lmax=lmax)
    clamp = lambda i: (jnp.minimum(i, p_tiles - 1), 0)
    return pl.pallas_call(
        kern,
        grid=(tiles,),
        in_specs=[pl.BlockSpec((tt, d), clamp),
                  pl.BlockSpec((tt, LANES), clamp),
                  pl.BlockSpec((tt, d), lambda i: (0, 0)),
                  pl.BlockSpec((tt, LANES), lambda i: (0, 0))],
        out_specs=[pl.BlockSpec((lmax, d), lambda i: (i, 0)),
                   pl.BlockSpec((tt, LANES), lambda i: (i, 0)),
                   pl.BlockSpec((8, LANES), lambda i: (i, 0))],
        out_shape=[jax.ShapeDtypeStruct((tiles * lmax, d), BF16),
                   jax.ShapeDtypeStruct((tiles * tt, LANES), F32),
                   jax.ShapeDtypeStruct((tiles * 8, LANES), F32)],
        compiler_params=_cparams(("arbitrary",)),
        name="dispatch_sort",
    )(h2_p, route_p, h2_d, route_d)


def _tables_kernel(cnt_ref, te_ref, na_ref, msrc_ref, csrc_ref, loc_ref, *, tiles, n_e, cpt, stride, lmax,
                   n_tiles, zero_row):
    def fill(ref, n, val):
        def body(i, carry):
            ref[i] = val
            return carry
        lax.fori_loop(0, n, body, 0, unroll=8)

    fill(msrc_ref, n_tiles * cpt, zero_row)
    fill(csrc_ref, tiles * stride, 0)
    fill(loc_ref, tiles, 0)

    def expert_body(e, g):
        def tile_body(i, g):
            nblk = (cnt_ref[i * n_e + e] + (ROW_ALIGN - 1)) // ROW_ALIGN
            loc = loc_ref[i]

            def block_body(q, carry):
                msrc_ref[g + q] = i * lmax + (loc + q) * ROW_ALIGN
                csrc_ref[i * stride + loc + q] = (g + q) * ROW_ALIGN
                return carry

            lax.fori_loop(0, nblk, block_body, 0)
            loc_ref[i] = loc + nblk
            return g + nblk

        g_end = lax.fori_loop(0, tiles, tile_body, g)
        t_first = g // cpt
        t_last = (g_end + (cpt - 1)) // cpt

        def mark(t, carry):
            te_ref[t] = e
            return carry

        lax.fori_loop(t_first, t_last, mark, 0)
        return t_last * cpt

    g = lax.fori_loop(0, n_e, expert_body, 0)
    n_act = g // cpt
    na_ref[0] = n_act
    last = te_ref[jnp.maximum(n_act - 1, 0)]

    def mark_rest(t, carry):
        te_ref[t] = last
        return carry

    lax.fori_loop(n_act, n_tiles, mark_rest, 0)


def _routing_tables(cnt_flat, *, tiles, n_e, tmo, n_tiles, stride, lmax):
    cpt = tmo // ROW_ALIGN
    kern = functools.partial(_tables_kernel, tiles=tiles, n_e=n_e, cpt=cpt, stride=stride, lmax=lmax,
                             n_tiles=n_tiles, zero_row=lmax - ROW_ALIGN)
    smem = pl.BlockSpec(memory_space=pltpu.SMEM)
    return pl.pallas_call(
        kern,
        in_specs=[smem],
        out_specs=[smem, smem, smem, smem],
        out_shape=[jax.ShapeDtypeStruct((n_tiles,), jnp.int32),
                   jax.ShapeDtypeStruct((1,), jnp.int32),
                   jax.ShapeDtypeStruct((n_tiles * cpt,), jnp.int32),
                   jax.ShapeDtypeStruct((tiles * stride,), jnp.int32)],
        scratch_shapes=[pltpu.SMEM((tiles,), jnp.int32)],
        name="routing_tables",
    )(cnt_flat)


def _moe_kernel(tile_expert_ref, n_active_ref, src_ref, xs_hbm, wi_ref, bi_ref, wo_ref, bo_ref,
                y_ref, xbuf, wi_bf, wo_bf, sem):
    j = pl.program_id(0)
    nt = pl.num_programs(0)
    tm = xbuf.shape[1]
    f = wo_ref.shape[1]
    cpt = tm // ROW_ALIGN
    slot = j % 2
    n_active = n_active_ref[0]

    def start_tile(tile, s):
        for c in range(cpt):
            row = pl.multiple_of(src_ref[tile * cpt + c], ROW_ALIGN)
            pltpu.make_async_copy(xs_hbm.at[pl.ds(row, ROW_ALIGN), :],
                                  xbuf.at[s, pl.ds(c * ROW_ALIGN, ROW_ALIGN), :], sem.at[s]).start()

    def wait_tile(s):
        pltpu.make_async_copy(xs_hbm.at[pl.ds(0, tm), :], xbuf.at[s], sem.at[s]).wait()

    @pl.when(j == 0)
    def _():
        start_tile(0, 0)

    start_tile(jnp.minimum(j + 1, nt - 1), 1 - slot)
    wait_tile(slot)

    @pl.when(j == nt - 1)
    def _():
        wait_tile(1 - slot)

    @pl.when(j < n_active)
    def _():
        prev = tile_expert_ref[jnp.maximum(j - 1, 0)]

        @pl.when((j == 0) | (tile_expert_ref[j] != prev))
        def _():
            wi_bf[...] = wi_ref[0].astype(BF16)
            wo_bf[...] = wo_ref[0].astype(BF16)

        hh = _dot(xbuf[slot], wi_bf[...]) + bi_ref[0]
        gate = jnp.minimum(hh[:, :f], SWIGLU_LIMIT)
        lin = jnp.clip(hh[:, f:], -SWIGLU_LIMIT, SWIGLU_LIMIT)
        act = (lin + 1.0) * gate * _sigmoid(SWIGLU_ALPHA * gate)
        y_ref[...] = (_dot(act.astype(BF16), wo_bf[...]) + bo_ref[0]).astype(BF16)

    @pl.when(j >= n_active)
    def _():
        y_ref[...] = jnp.zeros_like(y_ref)


def _moe(xs, moe_src, tile_expert, n_active, w_in, b_in, w_out, b_out, *, tm, n_tiles):
    e, d, f2 = w_in.shape
    f = w_out.shape[1]
    grid_spec = pltpu.PrefetchScalarGridSpec(
        num_scalar_prefetch=3,
        grid=(n_tiles,),
        in_specs=[pl.BlockSpec(memory_space=pl.ANY),
                  pl.BlockSpec((1, d, f2), lambda j, te, na, src: (te[j], 0, 0)),
                  pl.BlockSpec((1, 1, f2), lambda j, te, na, src: (te[j], 0, 0)),
                  pl.BlockSpec((1, f, d), lambda j, te, na, src: (te[j], 0, 0)),
                  pl.BlockSpec((1, 1, d), lambda j, te, na, src: (te[j], 0, 0))],
        out_specs=pl.BlockSpec((tm, d), lambda j, te, na, src: (j, 0)),
        scratch_shapes=[pltpu.VMEM((2, tm, d), BF16),
                        pltpu.VMEM((d, f2), BF16),
                        pltpu.VMEM((f, d), BF16),
                        pltpu.SemaphoreType.DMA((2,))],
    )
    return pl.pallas_call(
        _moe_kernel,
        grid_spec=grid_spec,
        out_shape=jax.ShapeDtypeStruct((n_tiles * tm, d), BF16),
        compiler_params=_cparams(("arbitrary",)),
        name="moe_experts",
    )(tile_expert, n_active, moe_src, xs, w_in, b_in.reshape(e, 1, f2), w_out, b_out.reshape(e, 1, d))


def _combine_kernel(src_ref, x1_ref, r2_ref, mod_ref, gf_ref, y_hbm, o_ref, ybuf, sem,
                    *, per_row, tiles_per_seq, tile0, stride):
    i = pl.program_id(0)
    nt = pl.num_programs(0)
    tt, d = x1_ref.shape
    lmax = ybuf.shape[1]
    nch = lmax // ROW_ALIGN
    slot = i % 2
    b = i // tiles_per_seq

    def start_tile(tile, s):
        for c in range(nch):
            row = pl.multiple_of(src_ref[(tile0 + tile) * stride + c], ROW_ALIGN)
            pltpu.make_async_copy(y_hbm.at[pl.ds(row, ROW_ALIGN), :],
                                  ybuf.at[s, pl.ds(c * ROW_ALIGN, ROW_ALIGN), :], sem.at[s]).start()

    def wait_tile(s):
        pltpu.make_async_copy(y_hbm.at[pl.ds(0, lmax), :], ybuf.at[s], sem.at[s]).wait()

    @pl.when(i == 0)
    def _():
        start_tile(0, 0)

    start_tile(jnp.minimum(i + 1, nt - 1), 1 - slot)
    wait_tile(slot)
    r2 = r2_ref[...]
    moe = jnp.zeros((tt, d), F32)
    for cb in range(lmax // SORT_BLOCK):
        col = (lax.broadcasted_iota(jnp.int32, (tt, SORT_BLOCK), 1) + cb * SORT_BLOCK).astype(F32)
        pw = jnp.zeros((tt, SORT_BLOCK), F32)
        for kk in range(TOP_K):
            pw = pw + jnp.where(col == r2[:, kk:kk + 1], r2[:, TOP_K + kk:TOP_K + kk + 1], 0.0)
        hi = pw.astype(BF16)
        lo = (pw - hi.astype(F32)).astype(BF16)
        yb = ybuf[slot, cb * SORT_BLOCK:(cb + 1) * SORT_BLOCK, :]
        moe = moe + _dot(hi, yb) + _dot(lo, yb)
    x2 = x1_ref[...] + _mod_rows(mod_ref, 5, d, per_row, b) * moe
    o_ref[...] = _rms(x2) * gf_ref[...]

    @pl.when(i == nt - 1)
    def _():
        wait_tile(1 - slot)


def _combine(comb_src, x1, r2, mod, gf, ys, *, tt, lmax, per_row, tiles_per_seq, tile0, stride, r2_block0):
    n, d = x1.shape
    kern = functools.partial(_combine_kernel, per_row=per_row, tiles_per_seq=tiles_per_seq, tile0=tile0,
                             stride=stride)
    mod_spec = (pl.BlockSpec((tt, mod.shape[1]), lambda i, s: (i, 0)) if per_row
                else pl.BlockSpec(mod.shape, lambda i, s: (0, 0)))
    grid_spec = pltpu.PrefetchScalarGridSpec(
        num_scalar_prefetch=1,
        grid=(n // tt,),
        in_specs=[pl.BlockSpec((tt, d), lambda i, s: (i, 0)),
                  pl.BlockSpec((tt, LANES), lambda i, s: (r2_block0 + i, 0)),
                  mod_spec,
                  pl.BlockSpec((1, d), lambda i, s: (0, 0)),
                  pl.BlockSpec(memory_space=pl.ANY)],
        out_specs=pl.BlockSpec((tt, d), lambda i, s: (i, 0)),
        scratch_shapes=[pltpu.VMEM((2, lmax, d), BF16), pltpu.SemaphoreType.DMA((2,))],
    )
    return pl.pallas_call(
        kern,
        grid_spec=grid_spec,
        out_shape=jax.ShapeDtypeStruct((n, d), F32),
        compiler_params=_cparams(("arbitrary",)),
        name="combine_final_per_row" if per_row else "combine_final",
    )(comb_src, x1, r2, mod, gf.reshape(1, d), ys)


def kernel(x_prompt, x_sample, c_prompt, c_sample, state_pool, state_ret, w_ada, b_ada, norm1_g, w_in,
           w_pool, pool_scale, w_up_pool, w_up_ret, w_out, norm2_g, w_router, b_router, w_expert_in,
           b_expert_in, w_expert_out, b_expert_out, final_norm_g):
    batch, seq, d = x_prompt.shape
    n_dec = x_sample.shape[0]
    n_prompt = batch * seq
    n_experts = w_router.shape[-1]
    xp = x_prompt.reshape(n_prompt, d)
    xs = x_sample.reshape(n_dec, d)

    w_in_bf = w_in[0].astype(BF16)
    wa_bf = w_up_pool[0].astype(BF16)
    wb_bf = w_up_ret[0].astype(BF16)
    wo_bf = w_out[0].astype(BF16)
    wr_pad = jnp.zeros((d, LANES), F32).at[:, :n_experts].set(w_router[0])
    br_pad = jnp.zeros((1, LANES), F32).at[0, :n_experts].set(b_router[0])

    c_all = jnp.concatenate([c_prompt, c_sample], axis=0)
    c_all = jnp.pad(c_all, ((0, -c_all.shape[0] % 16), (0, 0)))
    mod = _modulation(c_all, w_ada[0], b_ada[0])
    mod_p, mod_s = mod[:batch], mod[batch:batch + n_dec]

    tm = min(1024, seq)
    u_p, qkvg_p = _in_proj(xp, mod_p, norm1_g[0], w_in_bf, tm=tm, per_row=False, tiles_per_seq=seq // tm,
                           split=True)
    proj_s = _in_proj(xs, mod_s, norm1_g[0], w_in_bf, tm=n_dec, per_row=True, tiles_per_seq=1, split=False)

    ya_p, tail_p = _pool_prompt(u_p, w_pool[0], pool_scale[0], batch=batch, seq=seq, tt=min(256, seq))
    ya_s, pool_s_t = _pool_decode(jnp.transpose(state_pool[0], (1, 0, 2)), proj_s, w_pool[0], pool_scale[0])
    yb_p, ret_p = _ret_prompt(qkvg_p, batch=batch, seq=seq, d_model=d, chunk=min(RET_CHUNK, seq))
    yb_s, ret_s = _ret_decode(proj_s, state_ret[0], d_model=d, bs=8)

    tmix = min(512, seq)
    x1_p, h2_p, route_p = _mix(xp, ya_p, yb_p, qkvg_p, mod_p, norm2_g[0], wa_bf, wb_bf, wo_bf, wr_pad, br_pad,
                               tm=tmix, per_row=False, tiles_per_seq=seq // tmix, n_experts=n_experts,
                               ga_blk=6)
    x1_s, h2_s, route_s = _mix(xs, ya_s, yb_s, proj_s, mod_s, norm2_g[0], wa_bf, wb_bf, wo_bf, wr_pad, br_pad,
                               tm=n_dec, per_row=True, tiles_per_seq=1, n_experts=n_experts, ga_blk=7)

    tt = min(TOKEN_TILE, seq)
    assert n_dec <= tt
    p_tiles = n_prompt // tt
    lmax = _sorted_rows(tt, n_experts)
    xs_buf, r2, cnt_t = _dispatch(h2_p, route_p, h2_s, route_s, tt=tt, lmax=lmax)

    tmo = MOE_ROW_TILE
    cnt = cnt_t.reshape(p_tiles + 1, 8, LANES)[:, 0, :n_experts].astype(jnp.int32).reshape(-1)
    max_rows = (n_prompt + n_dec) * TOP_K + (p_tiles + 1) * n_experts * (ROW_ALIGN - 1)
    n_tiles = max_rows // tmo + n_experts
    stride = lmax // ROW_ALIGN
    tile_expert, n_active, moe_src, comb_src = _routing_tables(
        cnt, tiles=p_tiles + 1, n_e=n_experts, tmo=tmo, n_tiles=n_tiles, stride=stride, lmax=lmax)

    ys = _moe(xs_buf, moe_src, tile_expert, n_active, w_expert_in[0], b_expert_in[0], w_expert_out[0],
              b_expert_out[0], tm=tmo, n_tiles=n_tiles)

    assert n_prompt % n_dec == 0
    y_p = _combine(comb_src, x1_p, r2, mod_p, final_norm_g, ys, tt=tt, lmax=lmax, per_row=False,
                   tiles_per_seq=seq // tt, tile0=0, stride=stride, r2_block0=0)
    y_s = _combine(comb_src, x1_s, r2, mod_s, final_norm_g, ys, tt=n_dec, lmax=lmax, per_row=True,
                   tiles_per_seq=1, tile0=p_tiles, stride=stride, r2_block0=n_prompt // n_dec)

    return (y_p.reshape(batch, seq, d),
            y_s.reshape(n_dec, 1, d),
            tail_p[None, :, 1:, :],
            ret_p[None],
            jnp.transpose(pool_s_t, (1, 0, 2))[None],
            ret_s[None])
```

```python
import functools

import numpy as np
import jax
import jax.numpy as jnp
from jax import lax
from jax.experimental import pallas as pl
from jax.experimental.pallas import tpu as pltpu

F32 = jnp.float32
BF16 = jnp.bfloat16

POOL_WINDOWS = (2, 4, 8, 16)
POOL_BUF = max(POOL_WINDOWS) - 1
RET_HEADS = 4
RET_CHUNK = 256
ROPE_BASE = 10000.0
PAST_LEN = 16384
TOP_K = 4
SWIGLU_LIMIT = 7.0
SWIGLU_ALPHA = 1.702
N_MOD = 6
EPS = 1e-6
LANES = 128
ROW_ALIGN = 16
SORT_BLOCK = 256
POS_SPLIT = 64.0
TOKEN_TILE = 512
MOE_ROW_TILE = 512
VMEM_LIMIT = 56 * 1024 * 1024


def _cparams(sem):
    return pltpu.CompilerParams(dimension_semantics=sem, vmem_limit_bytes=VMEM_LIMIT)


def _dot(a, b):
    return jnp.dot(a, b, preferred_element_type=F32)


def _dot3(a, b):
    a_hi = a.astype(BF16)
    a_lo = (a - a_hi.astype(F32)).astype(BF16)
    b_hi = b.astype(BF16)
    b_lo = (b - b_hi.astype(F32)).astype(BF16)
    return _dot(a_hi, b_hi) + _dot(a_hi, b_lo) + _dot(a_lo, b_hi)


def _sigmoid(x):
    return 1.0 / (1.0 + jnp.exp(-x))


def _rms(x):
    return x * lax.rsqrt(jnp.mean(x * x, axis=-1, keepdims=True) + EPS)


def _round_up(x, m):
    return (x + m - 1) // m * m


def _mod_kernel(c_ref, w_ref, b_ref, o_ref):
    c = c_ref[...]
    o_ref[...] = _dot3(c * _sigmoid(c), w_ref[...]) + b_ref[...]


def _modulation(c, w_ada, b_ada):
    rows, d = c.shape
    n = w_ada.shape[1]
    tn = d
    return pl.pallas_call(
        _mod_kernel,
        grid=(n // tn,),
        in_specs=[pl.BlockSpec((rows, d), lambda j: (0, 0)),
                  pl.BlockSpec((d, tn), lambda j: (0, j)),
                  pl.BlockSpec((1, tn), lambda j: (0, j))],
        out_specs=pl.BlockSpec((rows, tn), lambda j: (0, j)),
        out_shape=jax.ShapeDtypeStruct((rows, n), F32),
        compiler_params=_cparams(("arbitrary",)),
        name="adaln_modulation",
    )(c, w_ada, b_ada.reshape(1, n))


def _mod_rows(mod_ref, which, d, per_row, b):
    if per_row:
        return mod_ref[:, which * d:(which + 1) * d]
    return mod_ref[pl.ds(b, 1), which * d:(which + 1) * d]


def _in_proj_kernel(x_ref, mod_ref, g_ref, w_ref, *outs, per_row, tiles_per_seq, split):
    i = pl.program_id(0)
    d = x_ref.shape[1]
    b = i // tiles_per_seq
    sh = _mod_rows(mod_ref, 0, d, per_row, b)
    sc = _mod_rows(mod_ref, 1, d, per_row, b)
    h = (_rms(x_ref[...]) * g_ref[...] * (1.0 + sc) + sh).astype(BF16)
    for c in range(w_ref.shape[1] // d):
        acc = _dot(h, w_ref[:, c * d:(c + 1) * d])
        if not split:
            outs[0][:, c * d:(c + 1) * d] = acc
        elif c == 0:
            outs[0][...] = acc
        else:
            outs[1][:, (c - 1) * d:c * d] = acc.astype(BF16)


def _in_proj(x, mod, g, w_bf, *, tm, per_row, tiles_per_seq, split):
    n, d = x.shape
    width = w_bf.shape[1]
    kern = functools.partial(_in_proj_kernel, per_row=per_row, tiles_per_seq=tiles_per_seq, split=split)
    mod_spec = (pl.BlockSpec((tm, mod.shape[1]), lambda i: (i, 0)) if per_row
                else pl.BlockSpec(mod.shape, lambda i: (0, 0)))
    if split:
        out_specs = [pl.BlockSpec((tm, d), lambda i: (i, 0)),
                     pl.BlockSpec((tm, width - d), lambda i: (i, 0))]
        out_shape = [jax.ShapeDtypeStruct((n, d), F32), jax.ShapeDtypeStruct((n, width - d), BF16)]
    else:
        out_specs = pl.BlockSpec((tm, width), lambda i: (i, 0))
        out_shape = jax.ShapeDtypeStruct((n, width), F32)
    return pl.pallas_call(
        kern,
        grid=(n // tm,),
        in_specs=[pl.BlockSpec((tm, d), lambda i: (i, 0)),
                  mod_spec,
                  pl.BlockSpec((1, d), lambda i: (0, 0)),
                  pl.BlockSpec((d, width), lambda i: (0, 0), pipeline_mode=pl.Buffered(1))],
        out_specs=out_specs,
        out_shape=out_shape,
        compiler_params=_cparams(("arbitrary",)),
        name="in_proj_per_row" if per_row else "in_proj",
    )(x, mod, g.reshape(1, d), w_bf)


def _pool_prompt_kernel(u_ref, w_ref, s_ref, y_ref, tail_ref, prev_ref, *, tt):
    t = pl.program_id(1)
    nt = pl.num_programs(1)
    hist = prev_ref.shape[0]
    gw = w_ref.shape[1]

    @pl.when(t == 0)
    def _():
        prev_ref[...] = jnp.zeros_like(prev_ref)

    u = u_ref[...]
    ext = jnp.concatenate([prev_ref[...], u], axis=0)
    pos = t * tt + lax.broadcasted_iota(jnp.int32, (tt, 1), 0)
    outs = []
    for gi, w in enumerate(POOL_WINDOWS):
        cols = slice(gi * gw, (gi + 1) * gw)
        run = ext[:, cols]
        span = 1
        while span < w:
            run = run + pltpu.roll(run, span, 0)
            span *= 2
        inv_cnt = 1.0 / jnp.minimum(w, pos + 1).astype(F32)
        pooled = run[hist:, :] * inv_cnt - u[:, cols]
        outs.append(_dot(pooled.astype(BF16), w_ref[gi].astype(BF16)))
    y = jnp.concatenate(outs, axis=-1) * s_ref[...]
    y_ref[...] = y.astype(BF16)
    prev_ref[...] = u[tt - hist:, :]

    @pl.when(t == nt - 1)
    def _():
        tail_ref[0] = u[tt - hist:, :]


def _pool_prompt(proj, w_pool, pool_scale, *, batch, seq, tt):
    p = pool_scale.shape[0]
    hist = POOL_BUF + 1
    nt = seq // tt
    kern = functools.partial(_pool_prompt_kernel, tt=tt)
    return pl.pallas_call(
        kern,
        grid=(batch, nt),
        in_specs=[pl.BlockSpec((tt, p), lambda b, t: (b * nt + t, 0)),
                  pl.BlockSpec(w_pool.shape, lambda b, t: (0, 0, 0)),
                  pl.BlockSpec((1, p), lambda b, t: (0, 0))],
        out_specs=[pl.BlockSpec((tt, p), lambda b, t: (b * nt + t, 0)),
                   pl.BlockSpec((1, hist, p), lambda b, t: (b, 0, 0))],
        out_shape=[jax.ShapeDtypeStruct((batch * seq, p), BF16),
                   jax.ShapeDtypeStruct((batch, hist, p), F32)],
        scratch_shapes=[pltpu.VMEM((hist, p), F32)],
        compiler_params=_cparams(("arbitrary", "arbitrary")),
        name="pool_prompt",
    )(proj, w_pool, pool_scale.reshape(1, p))


def _pool_decode_kernel(st_ref, u_ref, w_ref, s_ref, y_ref, new_ref):
    buf = st_ref.shape[0]
    gw = w_ref.shape[1]
    u = u_ref[...]
    outs = []
    run = u
    used = 0
    for gi, w in enumerate(POOL_WINDOWS):
        while used < w - 1:
            run = run + st_ref[buf - 1 - used]
            used += 1
        cols = slice(gi * gw, (gi + 1) * gw)
        pooled = run[:, cols] / float(min(w, PAST_LEN + 1)) - u[:, cols]
        outs.append(_dot(pooled.astype(BF16), w_ref[gi].astype(BF16)))
    y_ref[...] = (jnp.concatenate(outs, axis=-1) * s_ref[...]).astype(BF16)
    for r in range(buf - 1):
        new_ref[r] = st_ref[r + 1]
    new_ref[buf - 1] = u


def _pool_decode(state_t, proj_s, w_pool, pool_scale):
    buf, n, p = state_t.shape
    return pl.pallas_call(
        _pool_decode_kernel,
        grid=(1,),
        in_specs=[pl.BlockSpec((buf, n, p), lambda i: (0, 0, 0)),
                  pl.BlockSpec((n, p), lambda i: (0, 0)),
                  pl.BlockSpec(w_pool.shape, lambda i: (0, 0, 0)),
                  pl.BlockSpec((1, p), lambda i: (0, 0))],
        out_specs=[pl.BlockSpec((n, p), lambda i: (0, 0)),
                   pl.BlockSpec((buf, n, p), lambda i: (0, 0, 0))],
        out_shape=[jax.ShapeDtypeStruct((n, p), BF16),
                   jax.ShapeDtypeStruct((buf, n, p), F32)],
        compiler_params=_cparams(("arbitrary",)),
        name="pool_decode",
    )(state_t, proj_s, w_pool, pool_scale.reshape(1, p))


def _log_gamma(h):
    return float(np.log(1.0 - 2.0 ** (-5.0 - h)))


def _rope_tables(pos, half):
    inv = np.power(ROPE_BASE, -np.arange(half, dtype=np.float64) / half)
    ang = np.asarray(pos, np.float64)[:, None] * inv[None, :]
    return jnp.asarray(np.cos(ang), F32), jnp.asarray(np.sin(ang), F32)


def _rotary(x, cos, sin):
    half = x.shape[-1] // 2
    x1, x2 = x[:, :half], x[:, half:]
    return jnp.concatenate([x1 * cos - x2 * sin, x2 * cos + x1 * sin], axis=-1)


def _ret_tables(c):
    i = np.arange(c, dtype=np.float64)
    diff = i[:, None] - i[None, :]
    dec, qs, ks = [], [], []
    for h in range(RET_HEADS):
        lg = _log_gamma(h)
        dec.append(np.where(diff >= 0, np.exp(lg * np.maximum(diff, 0.0)), 0.0))
        qs.append(np.broadcast_to(np.exp(lg * (i + 1.0))[:, None], (c, LANES)))
        ks.append(np.broadcast_to(np.exp(lg * (c - 1.0 - i))[:, None], (c, LANES)))
    return (jnp.asarray(np.stack(dec), F32), jnp.asarray(np.stack(qs), F32), jnp.asarray(np.stack(ks), F32))


def _ret_prompt_kernel(q_ref, k_ref, v_ref, g_ref, cos_ref, sin_ref, dec_ref, qs_ref, ks_ref,
                       y_ref, s_out_ref, s_ref, *, chunk_decay):
    c = pl.program_id(1)
    nc = pl.num_programs(1)
    heads, dk, dv = s_ref.shape

    @pl.when(c == 0)
    def _():
        s_ref[...] = jnp.zeros_like(s_ref)

    cos, sin = cos_ref[...], sin_ref[...]
    reps = dk // LANES
    for h in range(heads):
        q = _rotary(q_ref[:, h * dk:(h + 1) * dk].astype(F32), cos, sin)
        k = _rotary(k_ref[:, h * dk:(h + 1) * dk].astype(F32), cos, sin) * (dk ** -0.5)
        v = v_ref[:, h * dv:(h + 1) * dv]
        qs = jnp.concatenate([qs_ref[h]] * reps, axis=-1)
        ks = jnp.concatenate([ks_ref[h]] * reps, axis=-1)
        scores = lax.dot_general(q.astype(BF16), k.astype(BF16), (((1,), (1,)), ((), ())),
                                 preferred_element_type=F32) * dec_ref[h]
        state = s_ref[h]
        o = _dot(scores.astype(BF16), v) + _dot((q * qs).astype(BF16), state.astype(BF16))
        kv = lax.dot_general((k * ks).astype(BF16), v, (((0,), (0,)), ((), ())), preferred_element_type=F32)
        new_state = chunk_decay[h] * state + kv
        s_ref[h] = new_state
        g = g_ref[:, h * dv:(h + 1) * dv].astype(F32)
        y_ref[:, h * dv:(h + 1) * dv] = (g * _sigmoid(g) * _rms(o)).astype(BF16)

        @pl.when(c == nc - 1)
        def _():
            s_out_ref[0, h] = new_state


def _ret_prompt(qkvg, *, batch, seq, d_model, chunk):
    heads = RET_HEADS
    dk = d_model // heads
    dv = 2 * d_model // heads
    nc = seq // chunk
    cos, sin = _rope_tables(np.arange(seq), dk // 2)
    dec, qs, ks = _ret_tables(chunk)
    chunk_decay = tuple(float(np.exp(_log_gamma(h) * chunk)) for h in range(heads))
    kern = functools.partial(_ret_prompt_kernel, chunk_decay=chunk_decay)
    const = lambda a: pl.BlockSpec(a.shape, lambda b, c: (0,) * a.ndim)
    return pl.pallas_call(
        kern,
        grid=(batch, nc),
        in_specs=[pl.BlockSpec((chunk, d_model), lambda b, c: (b * nc + c, 0)),
                  pl.BlockSpec((chunk, d_model), lambda b, c: (b * nc + c, 1)),
                  pl.BlockSpec((chunk, 2 * d_model), lambda b, c: (b * nc + c, 1)),
                  pl.BlockSpec((chunk, 2 * d_model), lambda b, c: (b * nc + c, 2)),
                  pl.BlockSpec((chunk, dk // 2), lambda b, c: (c, 0)),
                  pl.BlockSpec((chunk, dk // 2), lambda b, c: (c, 0)),
                  const(dec), const(qs), const(ks)],
        out_specs=[pl.BlockSpec((chunk, heads * dv), lambda b, c: (b * nc + c, 0)),
                   pl.BlockSpec((1, heads, dk, dv), lambda b, c: (b, 0, 0, 0))],
        out_shape=[jax.ShapeDtypeStruct((batch * seq, heads * dv), BF16),
                   jax.ShapeDtypeStruct((batch, heads, dk, dv), F32)],
        scratch_shapes=[pltpu.VMEM((heads, dk, dv), F32)],
        compiler_params=_cparams(("arbitrary", "arbitrary")),
        name="retention_prompt",
    )(qkvg, qkvg, qkvg, qkvg, cos, sin, dec, qs, ks)


def _ret_decode_kernel(q_ref, k_ref, v_ref, g_ref, cos_ref, sin_ref, s_ref, y_ref, s_out_ref,
                       kt_ref, qd_ref, o_ref, *, gammas, bs):
    h = pl.program_id(0)
    blk = pl.program_id(1)
    nblk = pl.num_programs(1)
    n, dk = q_ref.shape
    gamma = jnp.where(h == 0, gammas[0], jnp.where(h == 1, gammas[1], jnp.where(h == 2, gammas[2], gammas[3])))

    @pl.when(blk == 0)
    def _():
        cos, sin = cos_ref[...], sin_ref[...]
        q = _rotary(q_ref[...], cos, sin)
        k = _rotary(k_ref[...], cos, sin) * (dk ** -0.5)
        kt_ref[...] = k.T
        qd_ref[...] = (q * gamma).astype(BF16)
        o_ref[...] = jnp.sum(q * k, axis=-1, keepdims=True) * v_ref[...]

    vb = v_ref[...].astype(BF16)
    rows = lax.broadcasted_iota(jnp.int32, (n, 1), 0)
    lanes = lax.broadcasted_iota(jnp.int32, (1, n), 1)
    for i in range(bs):
        tok = blk * bs + i
        state = s_ref[i, 0]
        o_ref[...] += jnp.where(rows == tok, _dot(qd_ref[...], state.astype(BF16)), 0.0)
        k_col = jnp.where(lanes == tok, kt_ref[...], 0.0).astype(BF16)
        s_out_ref[i, 0] = gamma * state + _dot(k_col, vb)

    @pl.when(blk == nblk - 1)
    def _():
        g = g_ref[...]
        y_ref[...] = (g * _sigmoid(g) * _rms(o_ref[...])).astype(BF16)


def _ret_decode(proj_s, state, *, d_model, bs):
    n = proj_s.shape[0]
    heads = RET_HEADS
    dk = d_model // heads
    dv = 2 * d_model // heads
    cos, sin = _rope_tables(np.array([PAST_LEN]), dk // 2)
    gammas = tuple(float(np.exp(_log_gamma(h))) for h in range(heads))
    q0, k0, v0, g0 = d_model // dk, 2 * d_model // dk, 3 * d_model // dv, 5 * d_model // dv
    kern = functools.partial(_ret_decode_kernel, gammas=gammas, bs=bs)
    return pl.pallas_call(
        kern,
        grid=(heads, n // bs),
        in_specs=[pl.BlockSpec((n, dk), lambda h, b: (0, q0 + h)),
                  pl.BlockSpec((n, dk), lambda h, b: (0, k0 + h)),
                  pl.BlockSpec((n, dv), lambda h, b: (0, v0 + h)),
                  pl.BlockSpec((n, dv), lambda h, b: (0, g0 + h)),
                  pl.BlockSpec((1, dk // 2), lambda h, b: (0, 0)),
                  pl.BlockSpec((1, dk // 2), lambda h, b: (0, 0)),
                  pl.BlockSpec((bs, 1, dk, dv), lambda h, b: (b, h, 0, 0))],
        out_specs=[pl.BlockSpec((n, dv), lambda h, b: (0, h)),
                   pl.BlockSpec((bs, 1, dk, dv), lambda h, b: (b, h, 0, 0))],
        out_shape=[jax.ShapeDtypeStruct((n, heads * dv), BF16),
                   jax.ShapeDtypeStruct(state.shape, F32)],
        scratch_shapes=[pltpu.VMEM((dk, n), F32), pltpu.VMEM((n, dk), BF16), pltpu.VMEM((n, dv), F32)],
        compiler_params=_cparams(("arbitrary", "arbitrary")),
        name="retention_decode",
    )(proj_s, proj_s, proj_s, proj_s, cos, sin, state)


def _mix_kernel(x_ref, ya_ref, yb_ref, ga_ref, gb_ref, mod_ref, g2_ref, wa_ref, wb_ref, wo_ref,
                wr_ref, br_ref, x1_ref, h2_ref, route_ref, *, per_row, tiles_per_seq, n_experts):
    i = pl.program_id(0)
    tm, d = x_ref.shape
    b = i // tiles_per_seq

    merged = (_sigmoid(ga_ref[...].astype(F32)) * _dot(ya_ref[...], wa_ref[...])
              + _sigmoid(gb_ref[...].astype(F32)) * _dot(yb_ref[...], wb_ref[...]))
    mix = _dot(merged.astype(BF16), wo_ref[...])
    x1 = x_ref[...] + _mod_rows(mod_ref, 2, d, per_row, b) * mix
    x1_ref[...] = x1
    h2 = (_rms(x1) * g2_ref[...] * (1.0 + _mod_rows(mod_ref, 4, d, per_row, b))
          + _mod_rows(mod_ref, 3, d, per_row, b))
    h2_ref[...] = h2.astype(BF16)

    logits = _dot3(h2, wr_ref[...]) + br_ref[...]
    lane = lax.broadcasted_iota(jnp.int32, (tm, LANES), 1)
    neg = jnp.float32(-jnp.inf)
    work = jnp.where(lane < n_experts, logits, neg)
    vals, idxs = [], []
    for _ in range(TOP_K):
        m = jnp.max(work, axis=-1, keepdims=True)
        idx = jnp.min(jnp.where(work == m, lane, LANES), axis=-1, keepdims=True)
        vals.append(m)
        idxs.append(idx)
        work = jnp.where(lane == idx, neg, work)
    exps = [jnp.exp(v - vals[0]) for v in vals]
    denom = exps[0] + exps[1] + exps[2] + exps[3]
    route = jnp.zeros((tm, LANES), F32)
    for kk in range(TOP_K):
        route = jnp.where(lane == kk, idxs[kk].astype(F32), route)
        route = jnp.where(lane == TOP_K + kk, exps[kk] / denom, route)
    route_ref[...] = route


def _mix(x, ya, yb, proj, mod, g2, wa_bf, wb_bf, wo_bf, wr_pad, br_pad, *, tm, per_row, tiles_per_seq,
         n_experts, ga_blk):
    n, d = x.shape
    gb_blk = ga_blk + 1
    kern = functools.partial(_mix_kernel, per_row=per_row, tiles_per_seq=tiles_per_seq, n_experts=n_experts)
    mod_spec = (pl.BlockSpec((tm, mod.shape[1]), lambda i: (i, 0)) if per_row
                else pl.BlockSpec(mod.shape, lambda i: (0, 0)))
    full = lambda a: pl.BlockSpec(a.shape, lambda i: (0,) * a.ndim)
    return pl.pallas_call(
        kern,
        grid=(n // tm,),
        in_specs=[pl.BlockSpec((tm, d), lambda i: (i, 0)),
                  pl.BlockSpec((tm, ya.shape[1]), lambda i: (i, 0)),
                  pl.BlockSpec((tm, yb.shape[1]), lambda i: (i, 0)),
                  pl.BlockSpec((tm, d), lambda i: (i, ga_blk)),
                  pl.BlockSpec((tm, d), lambda i: (i, gb_blk)),
                  mod_spec,
                  pl.BlockSpec((1, d), lambda i: (0, 0)),
                  full(wa_bf), full(wb_bf), full(wo_bf), full(wr_pad), full(br_pad)],
        out_specs=[pl.BlockSpec((tm, d), lambda i: (i, 0)),
                   pl.BlockSpec((tm, d), lambda i: (i, 0)),
                   pl.BlockSpec((tm, LANES), lambda i: (i, 0))],
        out_shape=[jax.ShapeDtypeStruct((n, d), F32),
                   jax.ShapeDtypeStruct((n, d), BF16),
                   jax.ShapeDtypeStruct((n, LANES), F32)],
        compiler_params=_cparams(("arbitrary",)),
        name="mix_route_per_row" if per_row else "mix_route",
    )(x, ya, yb, proj, proj, mod, g2.reshape(1, d), wa_bf, wb_bf, wo_bf, wr_pad, br_pad)


def _sorted_rows(tt, n_experts):
    return _round_up(tt * TOP_K + n_experts * (ROW_ALIGN - 1) + ROW_ALIGN, SORT_BLOCK)


def _dispatch_kernel(hp_ref, rp_ref, hd_ref, rd_ref, xs_ref, r2_ref, cnt_ref, *, lmax):
    tt = hp_ref.shape[0]
    is_dec = pl.program_id(0) == pl.num_programs(0) - 1
    route = jnp.where(is_dec, rd_ref[...], rp_ref[...])
    lane = lax.broadcasted_iota(jnp.int32, (tt, LANES), 1)
    lane_f = lane.astype(F32)
    hits = [lane_f == route[:, kk:kk + 1] for kk in range(TOP_K)]
    sel = jnp.zeros((tt, LANES), F32)
    for hit in hits:
        sel = sel + jnp.where(hit, 1.0, 0.0)
    cnt = jnp.sum(sel, axis=0, keepdims=True)
    seg = jnp.ceil(cnt / ROW_ALIGN) * ROW_ALIGN
    a_i = lax.broadcasted_iota(jnp.int32, (LANES, LANES), 0)
    b_i = lax.broadcasted_iota(jnp.int32, (LANES, LANES), 1)
    upper = jnp.where(a_i < b_i, 1.0, 0.0)
    off = _dot3(jnp.broadcast_to(seg, (8, LANES)), upper)[0:1]
    r_i = lax.broadcasted_iota(jnp.int32, (tt, tt), 0)
    c_i = lax.broadcasted_iota(jnp.int32, (tt, tt), 1)
    lower = jnp.where(c_i < r_i, 1.0, 0.0).astype(BF16)
    place = _dot(lower, sel.astype(BF16)) + off
    r2 = jnp.zeros((tt, LANES), F32)
    side = jnp.zeros((tt, LANES), F32)
    for kk in range(TOP_K):
        found = jnp.sum(jnp.where(hits[kk], 1.0, 0.0), axis=-1, keepdims=True)
        pos = jnp.where(found > 0.0, jnp.sum(jnp.where(hits[kk], place, 0.0), axis=-1, keepdims=True), -1.0)
        r2 = jnp.where(lane == kk, pos, r2)
        w = route[:, TOP_K + kk:TOP_K + kk + 1]
        w_hi = w.astype(BF16).astype(F32)
        pos_a = jnp.floor(pos / POS_SPLIT)
        side = jnp.where(lane == kk, w_hi, side)
        side = jnp.where(lane == TOP_K + kk, w - w_hi, side)
        side = jnp.where(lane == 2 * TOP_K + kk, pos_a, side)
        side = jnp.where(lane == 3 * TOP_K + kk, pos - POS_SPLIT * pos_a, side)
    r2_ref[...] = r2
    cnt_ref[...] = jnp.broadcast_to(cnt, cnt_ref.shape)
    r2t = r2.T.astype(jnp.int32).astype(jnp.int16)
    h = jnp.concatenate([jnp.where(is_dec, hd_ref[...], hp_ref[...]), side.astype(BF16)], axis=-1)
    d = hp_ref.shape[1]
    out_lane = lax.broadcasted_iota(jnp.int32, (SORT_BLOCK, LANES), 1)
    one = jnp.ones((SORT_BLOCK, tt), BF16)
    zero = jnp.zeros((SORT_BLOCK, tt), BF16)
    for rb in range(lmax // SORT_BLOCK):
        rio = (lax.broadcasted_iota(jnp.int32, (SORT_BLOCK, tt), 0) + rb * SORT_BLOCK).astype(jnp.int16)
        hit = rio == r2t[0:1, :]
        for kk in range(1, TOP_K):
            hit = hit | (rio == r2t[kk:kk + 1, :])
        perm = jnp.where(hit, one, zero)
        res = _dot(perm, h)
        info = res[:, d:]
        row = (lax.broadcasted_iota(jnp.int32, (SORT_BLOCK, LANES), 0) + rb * SORT_BLOCK).astype(F32)
        pos = (POS_SPLIT * pltpu.roll(info, LANES - 2 * TOP_K, 1) + pltpu.roll(info, LANES - 3 * TOP_K, 1))
        mine = jnp.where((pos == row) & (out_lane < TOP_K), 1.0, 0.0)
        mine = mine + pltpu.roll(mine, TOP_K, 1)
        rows = slice(rb * SORT_BLOCK, (rb + 1) * SORT_BLOCK)
        xs_ref[rows, :d] = res[:, :d].astype(BF16)
        xs_ref[rows, d:] = (info * mine).astype(BF16)


def _dispatch(h2_p, route_p, h2_d, route_d, *, tt, lmax):
    n, d = h2_p.shape
    p_tiles = n // tt
    tiles = p_tiles + 1
    pad = tt - h2_d.shape[0]
    h2_d = jnp.pad(h2_d, ((0, pad), (0, 0)))
    route_d = jnp.pad(route_d, ((0, pad), (0, 0)), constant_values=-1.0)
    kern = functools.partial(_dispatch_kernel, lmax=lmax)
    clamp = lambda i: (jnp.minimum(i, p_tiles - 1), 0)
    return pl.pallas_call(
        kern,
        grid=(tiles,),
        in_specs=[pl.BlockSpec((tt, d), clamp),
                  pl.BlockSpec((tt, LANES), clamp),
                  pl.BlockSpec((tt, d), lambda i: (0, 0)),
                  pl.BlockSpec((tt, LANES), lambda i: (0, 0))],
        out_specs=[pl.BlockSpec((lmax, d + LANES), lambda i: (i, 0)),
                   pl.BlockSpec((tt, LANES), lambda i: (i, 0)),
                   pl.BlockSpec((8, LANES), lambda i: (i, 0))],
        out_shape=[jax.ShapeDtypeStruct((tiles * lmax, d + LANES), BF16),
                   jax.ShapeDtypeStruct((tiles * tt, LANES), F32),
                   jax.ShapeDtypeStruct((tiles * 8, LANES), F32)],
        compiler_params=_cparams(("arbitrary",)),
        name="dispatch_sort",
    )(h2_p, route_p, h2_d, route_d)


def _tables_kernel(cnt_ref, te_ref, na_ref, msrc_ref, csrc_ref, nxt_ref, par_ref, loc_ref, *, tiles, n_e, cpt,
                   stride, lmax, n_tiles, zero_row):
    def fill(ref, n, val):
        def body(i, carry):
            ref[i] = val
            return carry
        lax.fori_loop(0, n, body, 0, unroll=8)

    fill(msrc_ref, n_tiles * cpt, zero_row)
    fill(csrc_ref, tiles * stride, 0)
    fill(loc_ref, tiles, 0)

    def expert_body(e, carry):
        g, run = carry

        def tile_body(i, g):
            nblk = (cnt_ref[i * n_e + e] + (ROW_ALIGN - 1)) // ROW_ALIGN
            loc = loc_ref[i]

            def block_body(q, carry):
                msrc_ref[g + q] = i * lmax + (loc + q) * ROW_ALIGN
                csrc_ref[i * stride + loc + q] = (g + q) * ROW_ALIGN
                return carry

            lax.fori_loop(0, nblk, block_body, 0)
            loc_ref[i] = loc + nblk
            return g + nblk

        g_end = lax.fori_loop(0, tiles, tile_body, g)
        t_first = g // cpt
        t_last = (g_end + (cpt - 1)) // cpt

        def mark(t, carry):
            te_ref[t] = e
            par_ref[t] = run % 2
            return carry

        lax.fori_loop(t_first, t_last, mark, 0)
        return t_last * cpt, jnp.where(t_last > t_first, run + 1, run)

    g, _ = lax.fori_loop(0, n_e, expert_body, (jnp.int32(0), jnp.int32(0)))
    n_act = g // cpt
    na_ref[0] = n_act
    last = te_ref[jnp.maximum(n_act - 1, 0)]

    def mark_rest(t, carry):
        te_ref[t] = last
        par_ref[t] = 0
        nxt_ref[t] = -1
        return carry

    lax.fori_loop(n_act, n_tiles, mark_rest, 0)

    def mark_next(s, nxt):
        t = n_act - 1 - s
        after = te_ref[jnp.minimum(t + 1, n_tiles - 1)]
        nxt = jnp.where((t + 1 < n_act) & (after != te_ref[t]), after, nxt)
        nxt_ref[t] = nxt
        return nxt

    lax.fori_loop(0, n_act, mark_next, jnp.int32(-1))


def _routing_tables(cnt_flat, *, tiles, n_e, tmo, n_tiles, stride, lmax):
    cpt = tmo // ROW_ALIGN
    kern = functools.partial(_tables_kernel, tiles=tiles, n_e=n_e, cpt=cpt, stride=stride, lmax=lmax,
                             n_tiles=n_tiles, zero_row=lmax - ROW_ALIGN)
    smem = pl.BlockSpec(memory_space=pltpu.SMEM)
    return pl.pallas_call(
        kern,
        in_specs=[smem],
        out_specs=[smem] * 6,
        out_shape=[jax.ShapeDtypeStruct((n_tiles,), jnp.int32),
                   jax.ShapeDtypeStruct((1,), jnp.int32),
                   jax.ShapeDtypeStruct((n_tiles * cpt,), jnp.int32),
                   jax.ShapeDtypeStruct((tiles * stride,), jnp.int32),
                   jax.ShapeDtypeStruct((n_tiles,), jnp.int32),
                   jax.ShapeDtypeStruct((n_tiles,), jnp.int32)],
        scratch_shapes=[pltpu.SMEM((tiles,), jnp.int32)],
        name="routing_tables",
    )(cnt_flat)


def _moe_kernel(tile_expert_ref, n_active_ref, src_ref, nxt_ref, par_ref, xs_hbm, wi_hbm, bi_ref, wo_hbm, bo_ref,
                y_ref, xbuf, wi_f32, wo_f32, wi_bf, wo_bf, sem, wsem):
    j = pl.program_id(0)
    tm = xbuf.shape[1]
    d = y_ref.shape[1]
    f = wo_bf.shape[0]
    cpt = tm // ROW_ALIGN
    slot = j % 2
    n_active = n_active_ref[0]
    expert = tile_expert_ref[j]

    def start_tile(tile, s):
        for c in range(cpt):
            row = pl.multiple_of(src_ref[tile * cpt + c], ROW_ALIGN)
            pltpu.make_async_copy(xs_hbm.at[pl.ds(row, ROW_ALIGN), :],
                                  xbuf.at[s, pl.ds(c * ROW_ALIGN, ROW_ALIGN), :], sem.at[s]).start()

    def wait_tile(s):
        pltpu.make_async_copy(xs_hbm.at[pl.ds(0, tm), :], xbuf.at[s], sem.at[s]).wait()

    def weight_copies(e, s):
        return (pltpu.make_async_copy(wi_hbm.at[e], wi_f32.at[s], wsem.at[0, s]),
                pltpu.make_async_copy(wo_hbm.at[e], wo_f32.at[s], wsem.at[1, s]))

    @pl.when(j == 0)
    def _():
        start_tile(0, 0)
        for cp in weight_copies(expert, par_ref[0]):
            cp.start()

    @pl.when(j + 1 < n_active)
    def _():
        start_tile(j + 1, 1 - slot)

    @pl.when(j < n_active)
    def _():
        @pl.when((j == 0) | (expert != tile_expert_ref[jnp.maximum(j - 1, 0)]))
        def _():
            ws = par_ref[j]
            for cp in weight_copies(expert, ws):
                cp.wait()

            @pl.when(nxt_ref[j] >= 0)
            def _():
                for cp in weight_copies(nxt_ref[j], 1 - ws):
                    cp.start()

            wi_bf[...] = wi_f32[ws].astype(BF16)
            wo_bf[...] = wo_f32[ws].astype(BF16)

        wait_tile(slot)
        rows = xbuf[slot]
        hh = _dot(rows[:, :d], wi_bf[...]) + bi_ref[0]
        gate = jnp.minimum(hh[:, :f], SWIGLU_LIMIT)
        lin = jnp.clip(hh[:, f:], -SWIGLU_LIMIT, SWIGLU_LIMIT)
        act = (lin + 1.0) * gate * _sigmoid(SWIGLU_ALPHA * gate)
        weight = jnp.sum(rows[:, d:].astype(F32), axis=-1, keepdims=True)
        y_ref[...] = ((_dot(act.astype(BF16), wo_bf[...]) + bo_ref[0]) * weight).astype(BF16)

    @pl.when(j >= n_active)
    def _():
        y_ref[...] = jnp.zeros_like(y_ref)


def _moe(xs, tables, w_in, b_in, w_out, b_out, *, tm, n_tiles):
    tile_expert, n_active, moe_src, _, nxt, par = tables
    e, d, f2 = w_in.shape
    f = w_out.shape[1]
    bias = lambda width: pl.BlockSpec((1, 1, width), lambda j, te, *_: (te[j], 0, 0))
    grid_spec = pltpu.PrefetchScalarGridSpec(
        num_scalar_prefetch=5,
        grid=(n_tiles,),
        in_specs=[pl.BlockSpec(memory_space=pl.ANY),
                  pl.BlockSpec(memory_space=pl.ANY),
                  bias(f2),
                  pl.BlockSpec(memory_space=pl.ANY),
                  bias(d)],
        out_specs=pl.BlockSpec((tm, d), lambda j, *_: (j, 0)),
        scratch_shapes=[pltpu.VMEM((2, tm, xs.shape[1]), BF16),
                        pltpu.VMEM((2, d, f2), F32),
                        pltpu.VMEM((2, f, d), F32),
                        pltpu.VMEM((d, f2), BF16),
                        pltpu.VMEM((f, d), BF16),
                        pltpu.SemaphoreType.DMA((2,)),
                        pltpu.SemaphoreType.DMA((2, 2))],
    )
    return pl.pallas_call(
        _moe_kernel,
        grid_spec=grid_spec,
        out_shape=jax.ShapeDtypeStruct((n_tiles * tm, d), BF16),
        compiler_params=_cparams(("arbitrary",)),
        name="moe_experts",
    )(tile_expert, n_active, moe_src, nxt, par, xs, w_in, b_in.reshape(e, 1, f2), w_out, b_out.reshape(e, 1, d))


def _combine_kernel(src_ref, x1_ref, r2_ref, mod_ref, gf_ref, y_hbm, o_ref, ybuf, sem,
                    *, per_row, tiles_per_seq, tile0, stride):
    i = pl.program_id(0)
    nt = pl.num_programs(0)
    tt, d = x1_ref.shape
    lmax = ybuf.shape[1]
    nch = lmax // ROW_ALIGN
    slot = i % 2
    b = i // tiles_per_seq

    def start_tile(tile, s):
        for c in range(nch):
            row = pl.multiple_of(src_ref[(tile0 + tile) * stride + c], ROW_ALIGN)
            pltpu.make_async_copy(y_hbm.at[pl.ds(row, ROW_ALIGN), :],
                                  ybuf.at[s, pl.ds(c * ROW_ALIGN, ROW_ALIGN), :], sem.at[s]).start()

    def wait_tile(s):
        pltpu.make_async_copy(y_hbm.at[pl.ds(0, lmax), :], ybuf.at[s], sem.at[s]).wait()

    @pl.when(i == 0)
    def _():
        start_tile(0, 0)

    start_tile(jnp.minimum(i + 1, nt - 1), 1 - slot)
    wait_tile(slot)
    r2 = r2_ref[...].astype(jnp.int32).astype(jnp.int16)
    one = jnp.ones((tt, SORT_BLOCK), BF16)
    zero = jnp.zeros((tt, SORT_BLOCK), BF16)
    moe = jnp.zeros((tt, d), F32)
    for cb in range(lmax // SORT_BLOCK):
        col = (lax.broadcasted_iota(jnp.int32, (tt, SORT_BLOCK), 1) + cb * SORT_BLOCK).astype(jnp.int16)
        hit = col == r2[:, 0:1]
        for kk in range(1, TOP_K):
            hit = hit | (col == r2[:, kk:kk + 1])
        moe = moe + _dot(jnp.where(hit, one, zero), ybuf[slot, cb * SORT_BLOCK:(cb + 1) * SORT_BLOCK, :])
    x2 = x1_ref[...] + _mod_rows(mod_ref, 5, d, per_row, b) * moe
    o_ref[...] = _rms(x2) * gf_ref[...]

    @pl.when(i == nt - 1)
    def _():
        wait_tile(1 - slot)


def _combine(comb_src, x1, r2, mod, gf, ys, *, tt, lmax, per_row, tiles_per_seq, tile0, stride, r2_block0):
    n, d = x1.shape
    kern = functools.partial(_combine_kernel, per_row=per_row, tiles_per_seq=tiles_per_seq, tile0=tile0,
                             stride=stride)
    mod_spec = (pl.BlockSpec((tt, mod.shape[1]), lambda i, s: (i, 0)) if per_row
                else pl.BlockSpec(mod.shape, lambda i, s: (0, 0)))
    grid_spec = pltpu.PrefetchScalarGridSpec(
        num_scalar_prefetch=1,
        grid=(n // tt,),
        in_specs=[pl.BlockSpec((tt, d), lambda i, s: (i, 0)),
                  pl.BlockSpec((tt, LANES), lambda i, s: (r2_block0 + i, 0)),
                  mod_spec,
                  pl.BlockSpec((1, d), lambda i, s: (0, 0)),
                  pl.BlockSpec(memory_space=pl.ANY)],
        out_specs=pl.BlockSpec((tt, d), lambda i, s: (i, 0)),
        scratch_shapes=[pltpu.VMEM((2, lmax, d), BF16), pltpu.SemaphoreType.DMA((2,))],
    )
    return pl.pallas_call(
        kern,
        grid_spec=grid_spec,
        out_shape=jax.ShapeDtypeStruct((n, d), F32),
        compiler_params=_cparams(("arbitrary",)),
        name="combine_final_per_row" if per_row else "combine_final",
    )(comb_src, x1, r2, mod, gf.reshape(1, d), ys)


def kernel(x_prompt, x_sample, c_prompt, c_sample, state_pool, state_ret, w_ada, b_ada, norm1_g, w_in,
           w_pool, pool_scale, w_up_pool, w_up_ret, w_out, norm2_g, w_router, b_router, w_expert_in,
           b_expert_in, w_expert_out, b_expert_out, final_norm_g):
    batch, seq, d = x_prompt.shape
    n_dec = x_sample.shape[0]
    n_prompt = batch * seq
    n_experts = w_router.shape[-1]
    xp = x_prompt.reshape(n_prompt, d)
    xs = x_sample.reshape(n_dec, d)

    w_in_bf = w_in[0].astype(BF16)
    wa_bf = w_up_pool[0].astype(BF16)
    wb_bf = w_up_ret[0].astype(BF16)
    wo_bf = w_out[0].astype(BF16)
    wr_pad = jnp.zeros((d, LANES), F32).at[:, :n_experts].set(w_router[0])
    br_pad = jnp.zeros((1, LANES), F32).at[0, :n_experts].set(b_router[0])

    c_all = jnp.concatenate([c_prompt, c_sample], axis=0)
    c_all = jnp.pad(c_all, ((0, -c_all.shape[0] % 16), (0, 0)))
    mod = _modulation(c_all, w_ada[0], b_ada[0])
    mod_p, mod_s = mod[:batch], mod[batch:batch + n_dec]

    tm = min(512, seq)
    u_p, qkvg_p = _in_proj(xp, mod_p, norm1_g[0], w_in_bf, tm=tm, per_row=False, tiles_per_seq=seq // tm,
                           split=True)
    proj_s = _in_proj(xs, mod_s, norm1_g[0], w_in_bf, tm=n_dec, per_row=True, tiles_per_seq=1, split=False)

    ya_p, tail_p = _pool_prompt(u_p, w_pool[0], pool_scale[0], batch=batch, seq=seq, tt=min(256, seq))
    ya_s, pool_s_t = _pool_decode(jnp.transpose(state_pool[0], (1, 0, 2)), proj_s, w_pool[0], pool_scale[0])
    yb_p, ret_p = _ret_prompt(qkvg_p, batch=batch, seq=seq, d_model=d, chunk=min(RET_CHUNK, seq))
    yb_s, ret_s = _ret_decode(proj_s, state_ret[0], d_model=d, bs=8)

    tmix = min(512, seq)
    x1_p, h2_p, route_p = _mix(xp, ya_p, yb_p, qkvg_p, mod_p, norm2_g[0], wa_bf, wb_bf, wo_bf, wr_pad, br_pad,
                               tm=tmix, per_row=False, tiles_per_seq=seq // tmix, n_experts=n_experts,
                               ga_blk=6)
    x1_s, h2_s, route_s = _mix(xs, ya_s, yb_s, proj_s, mod_s, norm2_g[0], wa_bf, wb_bf, wo_bf, wr_pad, br_pad,
                               tm=n_dec, per_row=True, tiles_per_seq=1, n_experts=n_experts, ga_blk=7)

    tt = min(TOKEN_TILE, seq)
    assert n_dec <= tt
    p_tiles = n_prompt // tt
    lmax = _sorted_rows(tt, n_experts)
    xs_buf, r2, cnt_t = _dispatch(h2_p, route_p, h2_s, route_s, tt=tt, lmax=lmax)

    tmo = MOE_ROW_TILE
    cnt = cnt_t.reshape(p_tiles + 1, 8, LANES)[:, 0, :n_experts].astype(jnp.int32).reshape(-1)
    max_rows = (n_prompt + n_dec) * TOP_K + (p_tiles + 1) * n_experts * (ROW_ALIGN - 1)
    n_tiles = max_rows // tmo + n_experts
    stride = lmax // ROW_ALIGN
    tables = _routing_tables(cnt, tiles=p_tiles + 1, n_e=n_experts, tmo=tmo, n_tiles=n_tiles, stride=stride,
                             lmax=lmax)
    comb_src = tables[3]

    ys = _moe(xs_buf, tables, w_expert_in[0], b_expert_in[0], w_expert_out[0], b_expert_out[0], tm=tmo,
              n_tiles=n_tiles)

    assert n_prompt % n_dec == 0
    y_p = _combine(comb_src, x1_p, r2, mod_p, final_norm_g, ys, tt=tt, lmax=lmax, per_row=False,
                   tiles_per_seq=seq // tt, tile0=0, stride=stride, r2_block0=0)
    y_s = _combine(comb_src, x1_s, r2, mod_s, final_norm_g, ys, tt=n_dec, lmax=lmax, per_row=True,
                   tiles_per_seq=1, tile0=p_tiles, stride=stride, r2_block0=n_prompt // n_dec)

    return (y_p.reshape(batch, seq, d),
            y_s.reshape(n_dec, 1, d),
            tail_p[None, :, 1:, :],
            ret_p[None],
            jnp.transpose(pool_s_t, (1, 0, 2))[None],
            ret_s[None])
```

```python
import functools

import numpy as np
import jax
import jax.numpy as jnp
from jax import lax
from jax.experimental import pallas as pl
from jax.experimental.pallas import tpu as pltpu

F32 = jnp.float32
BF16 = jnp.bfloat16

POOL_WINDOWS = (2, 4, 8, 16)
POOL_BUF = max(POOL_WINDOWS) - 1
RET_HEADS = 4
RET_CHUNK = 256
ROPE_BASE = 10000.0
PAST_LEN = 16384
TOP_K = 4
SWIGLU_LIMIT = 7.0
SWIGLU_ALPHA = 1.702
N_MOD = 6
EPS = 1e-6
LANES = 128
ROW_ALIGN = 16
SORT_BLOCK = 512
POS_SPLIT = 64.0
TOKEN_TILE = 512
MOE_ROW_TILE = 512
VMEM_LIMIT = 56 * 1024 * 1024


def _cparams(sem):
    return pltpu.CompilerParams(dimension_semantics=sem, vmem_limit_bytes=VMEM_LIMIT)


def _dot(a, b):
    return jnp.dot(a, b, preferred_element_type=F32)


def _dot3(a, b):
    a_hi = a.astype(BF16)
    a_lo = (a - a_hi.astype(F32)).astype(BF16)
    b_hi = b.astype(BF16)
    b_lo = (b - b_hi.astype(F32)).astype(BF16)
    return _dot(a_hi, b_hi) + _dot(a_hi, b_lo) + _dot(a_lo, b_hi)


def _dot3_nt(a, b):
    nt = lambda x, y: lax.dot_general(x, y, (((1,), (1,)), ((), ())), preferred_element_type=F32)
    a_hi = a.astype(BF16)
    a_lo = (a - a_hi.astype(F32)).astype(BF16)
    b_hi = b.astype(BF16)
    b_lo = (b - b_hi.astype(F32)).astype(BF16)
    return nt(a_hi, b_hi) + nt(a_hi, b_lo) + nt(a_lo, b_hi)


def _sigmoid(x):
    return 0.5 + 0.5 * jnp.tanh(0.5 * x)


def _silu(x):
    half = 0.5 * x
    return half + half * jnp.tanh(half)


def _rms(x):
    return x * lax.rsqrt(jnp.mean(x * x, axis=-1, keepdims=True) + EPS)


def _round_up(x, m):
    return (x + m - 1) // m * m


def _mod_kernel(c_ref, w_ref, b_ref, o_ref):
    c = c_ref[...]
    o_ref[...] = _dot3(_silu(c_ref[...]), w_ref[...]) + b_ref[...]


def _modulation(c, w_ada, b_ada):
    rows, d = c.shape
    n = w_ada.shape[1]
    tn = d
    return pl.pallas_call(
        _mod_kernel,
        grid=(n // tn,),
        in_specs=[pl.BlockSpec((rows, d), lambda j: (0, 0)),
                  pl.BlockSpec((d, tn), lambda j: (0, j)),
                  pl.BlockSpec((1, tn), lambda j: (0, j))],
        out_specs=pl.BlockSpec((rows, tn), lambda j: (0, j)),
        out_shape=jax.ShapeDtypeStruct((rows, n), F32),
        compiler_params=_cparams(("arbitrary",)),
        name="adaln_modulation",
    )(c, w_ada, b_ada.reshape(1, n))


def _mod_rows(mod_ref, which, d, per_row, b):
    if per_row:
        return mod_ref[:, which * d:(which + 1) * d]
    return mod_ref[pl.ds(b, 1), which * d:(which + 1) * d]


def _in_proj_kernel(x_ref, mod_ref, g_ref, w_ref, *outs, per_row, tiles_per_seq, split):
    i = pl.program_id(0)
    d = x_ref.shape[1]
    b = i // tiles_per_seq
    sh = _mod_rows(mod_ref, 0, d, per_row, b)
    sc = _mod_rows(mod_ref, 1, d, per_row, b)
    h = (_rms(x_ref[...]) * g_ref[...] * (1.0 + sc) + sh).astype(BF16)
    for c in range(w_ref.shape[1] // d):
        acc = _dot(h, w_ref[:, c * d:(c + 1) * d])
        if not split:
            outs[0][:, c * d:(c + 1) * d] = acc
        elif c == 0:
            outs[0][...] = acc
        else:
            outs[1][:, (c - 1) * d:c * d] = acc.astype(BF16)


def _in_proj(x, mod, g, w_bf, *, tm, per_row, tiles_per_seq, split):
    n, d = x.shape
    width = w_bf.shape[1]
    kern = functools.partial(_in_proj_kernel, per_row=per_row, tiles_per_seq=tiles_per_seq, split=split)
    mod_spec = (pl.BlockSpec((tm, mod.shape[1]), lambda i: (i, 0)) if per_row
                else pl.BlockSpec(mod.shape, lambda i: (0, 0)))
    if split:
        out_specs = [pl.BlockSpec((tm, d), lambda i: (i, 0)),
                     pl.BlockSpec((tm, width - d), lambda i: (i, 0))]
        out_shape = [jax.ShapeDtypeStruct((n, d), F32), jax.ShapeDtypeStruct((n, width - d), BF16)]
    else:
        out_specs = pl.BlockSpec((tm, width), lambda i: (i, 0))
        out_shape = jax.ShapeDtypeStruct((n, width), F32)
    return pl.pallas_call(
        kern,
        grid=(n // tm,),
        in_specs=[pl.BlockSpec((tm, d), lambda i: (i, 0)),
                  mod_spec,
                  pl.BlockSpec((1, d), lambda i: (0, 0)),
                  pl.BlockSpec((d, width), lambda i: (0, 0), pipeline_mode=pl.Buffered(1))],
        out_specs=out_specs,
        out_shape=out_shape,
        compiler_params=_cparams(("arbitrary",)),
        name="in_proj_per_row" if per_row else "in_proj",
    )(x, mod, g.reshape(1, d), w_bf)


def _pool_prompt_kernel(u_ref, w_ref, s_ref, y_ref, tail_ref, prev_ref, *, tt):
    t = pl.program_id(1)
    nt = pl.num_programs(1)
    hist = prev_ref.shape[0]
    gw = w_ref.shape[1]

    @pl.when(t == 0)
    def _():
        prev_ref[...] = jnp.zeros_like(prev_ref)

    u = u_ref[...]
    ext = jnp.concatenate([prev_ref[...], u], axis=0)
    pos = t * tt + lax.broadcasted_iota(jnp.int32, (tt, 1), 0)
    outs = []
    for gi, w in enumerate(POOL_WINDOWS):
        cols = slice(gi * gw, (gi + 1) * gw)
        run = ext[:, cols]
        span = 1
        while span < w:
            run = run + pltpu.roll(run, span, 0)
            span *= 2
        inv_cnt = 1.0 / jnp.minimum(w, pos + 1).astype(F32)
        pooled = run[hist:, :] * inv_cnt - u[:, cols]
        outs.append(_dot(pooled.astype(BF16), w_ref[gi].astype(BF16)))
    y = jnp.concatenate(outs, axis=-1) * s_ref[...]
    y_ref[...] = y.astype(BF16)
    prev_ref[...] = u[tt - hist:, :]

    @pl.when(t == nt - 1)
    def _():
        tail_ref[0] = u[tt - hist:, :]


def _pool_prompt(proj, w_pool, pool_scale, *, batch, seq, tt):
    p = pool_scale.shape[0]
    hist = POOL_BUF + 1
    nt = seq // tt
    kern = functools.partial(_pool_prompt_kernel, tt=tt)
    return pl.pallas_call(
        kern,
        grid=(batch, nt),
        in_specs=[pl.BlockSpec((tt, p), lambda b, t: (b * nt + t, 0)),
                  pl.BlockSpec(w_pool.shape, lambda b, t: (0, 0, 0)),
                  pl.BlockSpec((1, p), lambda b, t: (0, 0))],
        out_specs=[pl.BlockSpec((tt, p), lambda b, t: (b * nt + t, 0)),
                   pl.BlockSpec((1, hist, p), lambda b, t: (b, 0, 0))],
        out_shape=[jax.ShapeDtypeStruct((batch * seq, p), BF16),
                   jax.ShapeDtypeStruct((batch, hist, p), F32)],
        scratch_shapes=[pltpu.VMEM((hist, p), F32)],
        compiler_params=_cparams(("arbitrary", "arbitrary")),
        name="pool_prompt",
    )(proj, w_pool, pool_scale.reshape(1, p))


def _pool_decode_kernel(st_ref, u_ref, w_ref, s_ref, y_ref, new_ref):
    buf = st_ref.shape[0]
    gw = w_ref.shape[1]
    u = u_ref[...]
    outs = []
    run = u
    used = 0
    for gi, w in enumerate(POOL_WINDOWS):
        while used < w - 1:
            run = run + st_ref[buf - 1 - used]
            used += 1
        cols = slice(gi * gw, (gi + 1) * gw)
        pooled = run[:, cols] / float(min(w, PAST_LEN + 1)) - u[:, cols]
        outs.append(_dot(pooled.astype(BF16), w_ref[gi].astype(BF16)))
    y_ref[...] = (jnp.concatenate(outs, axis=-1) * s_ref[...]).astype(BF16)
    for r in range(buf - 1):
        new_ref[r] = st_ref[r + 1]
    new_ref[buf - 1] = u


def _pool_decode(state_t, proj_s, w_pool, pool_scale):
    buf, n, p = state_t.shape
    return pl.pallas_call(
        _pool_decode_kernel,
        grid=(1,),
        in_specs=[pl.BlockSpec((buf, n, p), lambda i: (0, 0, 0)),
                  pl.BlockSpec((n, p), lambda i: (0, 0)),
                  pl.BlockSpec(w_pool.shape, lambda i: (0, 0, 0)),
                  pl.BlockSpec((1, p), lambda i: (0, 0))],
        out_specs=[pl.BlockSpec((n, p), lambda i: (0, 0)),
                   pl.BlockSpec((buf, n, p), lambda i: (0, 0, 0))],
        out_shape=[jax.ShapeDtypeStruct((n, p), BF16),
                   jax.ShapeDtypeStruct((buf, n, p), F32)],
        compiler_params=_cparams(("arbitrary",)),
        name="pool_decode",
    )(state_t, proj_s, w_pool, pool_scale.reshape(1, p))


def _log_gamma(h):
    return float(np.log(1.0 - 2.0 ** (-5.0 - h)))


def _rope_tables(pos, half):
    inv = np.power(ROPE_BASE, -np.arange(half, dtype=np.float64) / half)
    ang = np.asarray(pos, np.float64)[:, None] * inv[None, :]
    return jnp.asarray(np.cos(ang), F32), jnp.asarray(np.sin(ang), F32)


def _rotary(x, cos, sin):
    half = x.shape[-1] // 2
    x1, x2 = x[:, :half], x[:, half:]
    return jnp.concatenate([x1 * cos - x2 * sin, x2 * cos + x1 * sin], axis=-1)


def _ret_tables(c, key_scale):
    i = np.arange(c, dtype=np.float64)
    diff = i[:, None] - i[None, :]
    dec, qs, ks = [], [], []
    for h in range(RET_HEADS):
        lg = _log_gamma(h)
        dec.append(key_scale * np.where(diff >= 0, np.exp(lg * np.maximum(diff, 0.0)), 0.0))
        qs.append(np.broadcast_to(np.exp(lg * (i + 1.0))[:, None], (c, LANES)))
        ks.append(key_scale * np.broadcast_to(np.exp(lg * (c - 1.0 - i))[:, None], (c, LANES)))
    return (jnp.asarray(np.stack(dec), F32), jnp.asarray(np.stack(qs), F32), jnp.asarray(np.stack(ks), F32))


def _ret_prompt_kernel(q_ref, k_ref, v_ref, g_ref, cos_ref, sin_ref, dec_ref, qs_ref, ks_ref,
                       y_ref, s_out_ref, s_ref, *, chunk_decay):
    c = pl.program_id(1)
    nc = pl.num_programs(1)
    heads, dk, dv = s_ref.shape

    @pl.when(c == 0)
    def _():
        s_ref[...] = jnp.zeros_like(s_ref)

    cos, sin = cos_ref[...], sin_ref[...]
    reps = dk // LANES
    for h in range(heads):
        q = _rotary(q_ref[:, h * dk:(h + 1) * dk].astype(F32), cos, sin)
        k = _rotary(k_ref[:, h * dk:(h + 1) * dk].astype(F32), cos, sin)
        v = v_ref[:, h * dv:(h + 1) * dv]
        qs = jnp.concatenate([qs_ref[h]] * reps, axis=-1)
        ks = jnp.concatenate([ks_ref[h]] * reps, axis=-1)
        scores = lax.dot_general(q.astype(BF16), k.astype(BF16), (((1,), (1,)), ((), ())),
                                 preferred_element_type=F32) * dec_ref[h]
        state = s_ref[h]
        o = _dot(scores.astype(BF16), v) + _dot((q * qs).astype(BF16), state.astype(BF16))
        kv = lax.dot_general((k * ks).astype(BF16), v, (((0,), (0,)), ((), ())), preferred_element_type=F32)
        new_state = chunk_decay[h] * state + kv
        s_ref[h] = new_state
        g = g_ref[:, h * dv:(h + 1) * dv].astype(F32)
        y_ref[:, h * dv:(h + 1) * dv] = (_silu(g) * _rms(o)).astype(BF16)

        @pl.when(c == nc - 1)
        def _():
            s_out_ref[0, h] = new_state


def _ret_prompt(qkvg, *, batch, seq, d_model, chunk):
    heads = RET_HEADS
    dk = d_model // heads
    dv = 2 * d_model // heads
    nc = seq // chunk
    cos, sin = _rope_tables(np.arange(seq), dk // 2)
    key_scale = dk ** -0.5
    assert np.log2(key_scale) == np.round(np.log2(key_scale))
    dec, qs, ks = _ret_tables(chunk, key_scale)
    chunk_decay = tuple(float(np.exp(_log_gamma(h) * chunk)) for h in range(heads))
    kern = functools.partial(_ret_prompt_kernel, chunk_decay=chunk_decay)
    const = lambda a: pl.BlockSpec(a.shape, lambda b, c: (0,) * a.ndim)
    return pl.pallas_call(
        kern,
        grid=(batch, nc),
        in_specs=[pl.BlockSpec((chunk, d_model), lambda b, c: (b * nc + c, 0)),
                  pl.BlockSpec((chunk, d_model), lambda b, c: (b * nc + c, 1)),
                  pl.BlockSpec((chunk, 2 * d_model), lambda b, c: (b * nc + c, 1)),
                  pl.BlockSpec((chunk, 2 * d_model), lambda b, c: (b * nc + c, 2)),
                  pl.BlockSpec((chunk, dk // 2), lambda b, c: (c, 0)),
                  pl.BlockSpec((chunk, dk // 2), lambda b, c: (c, 0)),
                  const(dec), const(qs), const(ks)],
        out_specs=[pl.BlockSpec((chunk, heads * dv), lambda b, c: (b * nc + c, 0)),
                   pl.BlockSpec((1, heads, dk, dv), lambda b, c: (b, 0, 0, 0))],
        out_shape=[jax.ShapeDtypeStruct((batch * seq, heads * dv), BF16),
                   jax.ShapeDtypeStruct((batch, heads, dk, dv), F32)],
        scratch_shapes=[pltpu.VMEM((heads, dk, dv), F32)],
        compiler_params=_cparams(("arbitrary", "arbitrary")),
        name="retention_prompt",
    )(qkvg, qkvg, qkvg, qkvg, cos, sin, dec, qs, ks)


def _ret_decode_kernel(q_ref, k_ref, v_ref, g_ref, cos_ref, sin_ref, s_ref, y_ref, s_out_ref,
                       kt_ref, qd_ref, o_ref, *, gammas, bs):
    h = pl.program_id(0)
    blk = pl.program_id(1)
    nblk = pl.num_programs(1)
    n, dk = q_ref.shape
    gamma = jnp.where(h == 0, gammas[0], jnp.where(h == 1, gammas[1], jnp.where(h == 2, gammas[2], gammas[3])))

    @pl.when(blk == 0)
    def _():
        cos, sin = cos_ref[...], sin_ref[...]
        q = _rotary(q_ref[...], cos, sin)
        k = _rotary(k_ref[...], cos, sin) * (dk ** -0.5)
        kt_ref[...] = k.T
        qd_ref[...] = (q * gamma).astype(BF16)
        o_ref[...] = jnp.sum(q * k, axis=-1, keepdims=True) * v_ref[...]

    vb = v_ref[...].astype(BF16)
    rows = lax.broadcasted_iota(jnp.int32, (n, 1), 0)
    lanes = lax.broadcasted_iota(jnp.int32, (1, n), 1)
    for i in range(bs):
        tok = blk * bs + i
        state = s_ref[i, 0]
        o_ref[...] += jnp.where(rows == tok, _dot(qd_ref[...], state.astype(BF16)), 0.0)
        k_col = jnp.where(lanes == tok, kt_ref[...], 0.0).astype(BF16)
        s_out_ref[i, 0] = gamma * state + _dot(k_col, vb)

    @pl.when(blk == nblk - 1)
    def _():
        g = g_ref[...]
        y_ref[...] = (_silu(g) * _rms(o_ref[...])).astype(BF16)


def _ret_decode(proj_s, state, *, d_model, bs):
    n = proj_s.shape[0]
    heads = RET_HEADS
    dk = d_model // heads
    dv = 2 * d_model // heads
    cos, sin = _rope_tables(np.array([PAST_LEN]), dk // 2)
    gammas = tuple(float(np.exp(_log_gamma(h))) for h in range(heads))
    q0, k0, v0, g0 = d_model // dk, 2 * d_model // dk, 3 * d_model // dv, 5 * d_model // dv
    kern = functools.partial(_ret_decode_kernel, gammas=gammas, bs=bs)
    return pl.pallas_call(
        kern,
        grid=(heads, n // bs),
        in_specs=[pl.BlockSpec((n, dk), lambda h, b: (0, q0 + h)),
                  pl.BlockSpec((n, dk), lambda h, b: (0, k0 + h)),
                  pl.BlockSpec((n, dv), lambda h, b: (0, v0 + h)),
                  pl.BlockSpec((n, dv), lambda h, b: (0, g0 + h)),
                  pl.BlockSpec((1, dk // 2), lambda h, b: (0, 0)),
                  pl.BlockSpec((1, dk // 2), lambda h, b: (0, 0)),
                  pl.BlockSpec((bs, 1, dk, dv), lambda h, b: (b, h, 0, 0))],
        out_specs=[pl.BlockSpec((n, dv), lambda h, b: (0, h)),
                   pl.BlockSpec((bs, 1, dk, dv), lambda h, b: (b, h, 0, 0))],
        out_shape=[jax.ShapeDtypeStruct((n, heads * dv), BF16),
                   jax.ShapeDtypeStruct(state.shape, F32)],
        scratch_shapes=[pltpu.VMEM((dk, n), F32), pltpu.VMEM((n, dk), BF16), pltpu.VMEM((n, dv), F32)],
        compiler_params=_cparams(("arbitrary", "arbitrary")),
        name="retention_decode",
    )(proj_s, proj_s, proj_s, proj_s, cos, sin, state)


def _mix_kernel(x_ref, ya_ref, yb_ref, ga_ref, gb_ref, mod_ref, g2_ref, wa_ref, wb_ref, wo_ref,
                wr_ref, br_ref, x1_ref, h2_ref, route_ref, *, per_row, tiles_per_seq, n_experts):
    i = pl.program_id(0)
    tm, d = x_ref.shape
    b = i // tiles_per_seq

    merged = (_sigmoid(ga_ref[...].astype(F32)) * _dot(ya_ref[...], wa_ref[...])
              + _sigmoid(gb_ref[...].astype(F32)) * _dot(yb_ref[...], wb_ref[...]))
    mix = _dot(merged.astype(BF16), wo_ref[...])
    x1 = x_ref[...] + _mod_rows(mod_ref, 2, d, per_row, b) * mix
    x1_ref[...] = x1
    h2 = (_rms(x1) * g2_ref[...] * (1.0 + _mod_rows(mod_ref, 4, d, per_row, b))
          + _mod_rows(mod_ref, 3, d, per_row, b))
    h2_ref[...] = h2.astype(BF16)

    logits_t = _dot3_nt(wr_ref[...], h2) + br_ref[:, 0:1]
    row = lax.broadcasted_iota(jnp.int32, (n_experts, tm), 0)
    neg = jnp.float32(-jnp.inf)
    work = logits_t
    vals, idxs = [], []
    for _ in range(TOP_K):
        m = jnp.max(work, axis=0, keepdims=True)
        idx = jnp.min(jnp.where(work == m, row, n_experts), axis=0, keepdims=True)
        vals.append(m)
        idxs.append(idx)
        work = jnp.where(row == idx, neg, work)
    exps = [jnp.exp(v - vals[0]) for v in vals]
    denom = exps[0] + exps[1] + exps[2] + exps[3]
    route_t = jnp.concatenate([idx.astype(F32) for idx in idxs] + [e / denom for e in exps]
                              + [jnp.zeros((LANES - 2 * TOP_K, tm), F32)], axis=0)
    route_ref[...] = route_t.T


def _mix(x, ya, yb, proj, mod, g2, wa_bf, wb_bf, wo_bf, wr_pad, br_pad, *, tm, per_row, tiles_per_seq,
         n_experts, ga_blk):
    n, d = x.shape
    gb_blk = ga_blk + 1
    kern = functools.partial(_mix_kernel, per_row=per_row, tiles_per_seq=tiles_per_seq, n_experts=n_experts)
    mod_spec = (pl.BlockSpec((tm, mod.shape[1]), lambda i: (i, 0)) if per_row
                else pl.BlockSpec(mod.shape, lambda i: (0, 0)))
    full = lambda a: pl.BlockSpec(a.shape, lambda i: (0,) * a.ndim)
    return pl.pallas_call(
        kern,
        grid=(n // tm,),
        in_specs=[pl.BlockSpec((tm, d), lambda i: (i, 0)),
                  pl.BlockSpec((tm, ya.shape[1]), lambda i: (i, 0)),
                  pl.BlockSpec((tm, yb.shape[1]), lambda i: (i, 0)),
                  pl.BlockSpec((tm, d), lambda i: (i, ga_blk)),
                  pl.BlockSpec((tm, d), lambda i: (i, gb_blk)),
                  mod_spec,
                  pl.BlockSpec((1, d), lambda i: (0, 0)),
                  full(wa_bf), full(wb_bf), full(wo_bf), full(wr_pad), full(br_pad)],
        out_specs=[pl.BlockSpec((tm, d), lambda i: (i, 0)),
                   pl.BlockSpec((tm, d), lambda i: (i, 0)),
                   pl.BlockSpec((tm, LANES), lambda i: (i, 0))],
        out_shape=[jax.ShapeDtypeStruct((n, d), F32),
                   jax.ShapeDtypeStruct((n, d), BF16),
                   jax.ShapeDtypeStruct((n, LANES), F32)],
        compiler_params=_cparams(("arbitrary",)),
        name="mix_route_per_row" if per_row else "mix_route",
    )(x, ya, yb, proj, proj, mod, g2.reshape(1, d), wa_bf, wb_bf, wo_bf, wr_pad, br_pad)


def _sorted_rows(tt, n_experts):
    return _round_up(tt * TOP_K + n_experts * (ROW_ALIGN - 1) + ROW_ALIGN, SORT_BLOCK)


def _dispatch_kernel(hp_ref, rp_ref, hd_ref, rd_ref, lower_ref, upper_ref, xs_ref, r2_ref, cnt_ref, *, lmax):
    tt = hp_ref.shape[0]
    is_dec = pl.program_id(0) == pl.num_programs(0) - 1
    route = jnp.where(is_dec, rd_ref[...], rp_ref[...])
    lane = lax.broadcasted_iota(jnp.int32, (tt, LANES), 1)
    lane_f = lane.astype(F32)
    hits = [lane_f == route[:, kk:kk + 1] for kk in range(TOP_K)]
    sel = jnp.zeros((tt, LANES), F32)
    for hit in hits:
        sel = sel + jnp.where(hit, 1.0, 0.0)
    cnt = jnp.sum(sel, axis=0, keepdims=True)
    seg = jnp.ceil(cnt / ROW_ALIGN) * ROW_ALIGN
    off = _dot3(jnp.broadcast_to(seg, (8, LANES)), upper_ref[...])[0:1]
    place = _dot(lower_ref[...], sel.astype(BF16)) + off
    r2 = jnp.zeros((tt, LANES), F32)
    side = jnp.zeros((tt, LANES), F32)
    for kk in range(TOP_K):
        found = jnp.sum(jnp.where(hits[kk], 1.0, 0.0), axis=-1, keepdims=True)
        pos = jnp.where(found > 0.0, jnp.sum(jnp.where(hits[kk], place, 0.0), axis=-1, keepdims=True), -1.0)
        r2 = jnp.where(lane == kk, pos, r2)
        w = route[:, TOP_K + kk:TOP_K + kk + 1]
        w_hi = w.astype(BF16).astype(F32)
        pos_a = jnp.floor(pos / POS_SPLIT)
        side = jnp.where(lane == kk, w_hi, side)
        side = jnp.where(lane == TOP_K + kk, w - w_hi, side)
        side = jnp.where(lane == 2 * TOP_K + kk, pos_a, side)
        side = jnp.where(lane == 3 * TOP_K + kk, pos - POS_SPLIT * pos_a, side)
    r2_ref[...] = r2
    cnt_ref[...] = jnp.broadcast_to(cnt, cnt_ref.shape)
    r2t = r2.T.astype(jnp.int32).astype(jnp.int16)
    h = jnp.concatenate([jnp.where(is_dec, hd_ref[...], hp_ref[...]), side.astype(BF16)], axis=-1)
    d = hp_ref.shape[1]
    out_lane = lax.broadcasted_iota(jnp.int32, (SORT_BLOCK, LANES), 1)
    one = jnp.ones((SORT_BLOCK, tt), BF16)
    zero = jnp.zeros((SORT_BLOCK, tt), BF16)
    for rb in range(lmax // SORT_BLOCK):
        rio = (lax.broadcasted_iota(jnp.int32, (SORT_BLOCK, tt), 0) + rb * SORT_BLOCK).astype(jnp.int16)
        hit = rio == r2t[0:1, :]
        for kk in range(1, TOP_K):
            hit = hit | (rio == r2t[kk:kk + 1, :])
        perm = jnp.where(hit, one, zero)
        res = _dot(perm, h)
        info = res[:, d:]
        row = (lax.broadcasted_iota(jnp.int32, (SORT_BLOCK, LANES), 0) + rb * SORT_BLOCK).astype(F32)
        pos = (POS_SPLIT * pltpu.roll(info, LANES - 2 * TOP_K, 1) + pltpu.roll(info, LANES - 3 * TOP_K, 1))
        mine = jnp.where((pos == row) & (out_lane < TOP_K), 1.0, 0.0)
        mine = mine + pltpu.roll(mine, TOP_K, 1)
        rows = slice(rb * SORT_BLOCK, (rb + 1) * SORT_BLOCK)
        xs_ref[rows, :d] = res[:, :d].astype(BF16)
        xs_ref[rows, d:] = (info * mine).astype(BF16)


def _dispatch(h2_p, route_p, h2_d, route_d, *, tt, lmax):
    n, d = h2_p.shape
    p_tiles = n // tt
    tiles = p_tiles + 1
    pad = tt - h2_d.shape[0]
    h2_d = jnp.pad(h2_d, ((0, pad), (0, 0)))
    route_d = jnp.pad(route_d, ((0, pad), (0, 0)), constant_values=-1.0)
    kern = functools.partial(_dispatch_kernel, lmax=lmax)
    clamp = lambda i: (jnp.minimum(i, p_tiles - 1), 0)
    lower = jnp.asarray(np.tril(np.ones((tt, tt), np.float32), -1), BF16)
    upper = jnp.asarray(np.triu(np.ones((LANES, LANES), np.float32), 1), F32)
    return pl.pallas_call(
        kern,
        grid=(tiles,),
        in_specs=[pl.BlockSpec((tt, d), clamp),
                  pl.BlockSpec((tt, LANES), clamp),
                  pl.BlockSpec((tt, d), lambda i: (0, 0)),
                  pl.BlockSpec((tt, LANES), lambda i: (0, 0)),
                  pl.BlockSpec((tt, tt), lambda i: (0, 0)),
                  pl.BlockSpec((LANES, LANES), lambda i: (0, 0))],
        out_specs=[pl.BlockSpec((lmax, d + LANES), lambda i: (i, 0)),
                   pl.BlockSpec((tt, LANES), lambda i: (i, 0)),
                   pl.BlockSpec((8, LANES), lambda i: (i, 0))],
        out_shape=[jax.ShapeDtypeStruct((tiles * lmax, d + LANES), BF16),
                   jax.ShapeDtypeStruct((tiles * tt, LANES), F32),
                   jax.ShapeDtypeStruct((tiles * 8, LANES), F32)],
        compiler_params=_cparams(("arbitrary",)),
        name="dispatch_sort",
    )(h2_p, route_p, h2_d, route_d, lower, upper)


def _tables_kernel(cnt_ref, te_ref, na_ref, msrc_ref, csrc_ref, nxt_ref, par_ref, loc_ref, *, tiles, n_e, cpt,
                   stride, lmax, n_tiles, zero_row):
    def fill(ref, n, val):
        def body(i, carry):
            ref[i] = val
            return carry
        lax.fori_loop(0, n, body, 0, unroll=8)

    fill(msrc_ref, n_tiles * cpt, zero_row)
    fill(csrc_ref, tiles * stride, 0)
    fill(loc_ref, tiles, 0)

    def expert_body(e, carry):
        g, run = carry

        def tile_body(i, g):
            nblk = (cnt_ref[i * n_e + e] + (ROW_ALIGN - 1)) // ROW_ALIGN
            loc = loc_ref[i]

            def block_body(q, carry):
                msrc_ref[g + q] = i * lmax + (loc + q) * ROW_ALIGN
                csrc_ref[i * stride + loc + q] = (g + q) * ROW_ALIGN
                return carry

            lax.fori_loop(0, nblk, block_body, 0)
            loc_ref[i] = loc + nblk
            return g + nblk

        g_end = lax.fori_loop(0, tiles, tile_body, g)
        t_first = g // cpt
        t_last = (g_end + (cpt - 1)) // cpt

        def mark(t, carry):
            te_ref[t] = e
            par_ref[t] = run % 2
            return carry

        lax.fori_loop(t_first, t_last, mark, 0)
        return t_last * cpt, jnp.where(t_last > t_first, run + 1, run)

    g, _ = lax.fori_loop(0, n_e, expert_body, (jnp.int32(0), jnp.int32(0)))
    n_act = g // cpt
    na_ref[0] = n_act
    last = te_ref[jnp.maximum(n_act - 1, 0)]

    def mark_rest(t, carry):
        te_ref[t] = last
        par_ref[t] = 0
        nxt_ref[t] = -1
        return carry

    lax.fori_loop(n_act, n_tiles, mark_rest, 0)

    def mark_next(s, nxt):
        t = n_act - 1 - s
        after = te_ref[jnp.minimum(t + 1, n_tiles - 1)]
        nxt = jnp.where((t + 1 < n_act) & (after != te_ref[t]), after, nxt)
        nxt_ref[t] = nxt
        return nxt

    lax.fori_loop(0, n_act, mark_next, jnp.int32(-1))


def _routing_tables(cnt_flat, *, tiles, n_e, tmo, n_tiles, stride, lmax):
    cpt = tmo // ROW_ALIGN
    kern = functools.partial(_tables_kernel, tiles=tiles, n_e=n_e, cpt=cpt, stride=stride, lmax=lmax,
                             n_tiles=n_tiles, zero_row=lmax - ROW_ALIGN)
    smem = pl.BlockSpec(memory_space=pltpu.SMEM)
    return pl.pallas_call(
        kern,
        in_specs=[smem],
        out_specs=[smem] * 6,
        out_shape=[jax.ShapeDtypeStruct((n_tiles,), jnp.int32),
                   jax.ShapeDtypeStruct((1,), jnp.int32),
                   jax.ShapeDtypeStruct((n_tiles * cpt,), jnp.int32),
                   jax.ShapeDtypeStruct((tiles * stride,), jnp.int32),
                   jax.ShapeDtypeStruct((n_tiles,), jnp.int32),
                   jax.ShapeDtypeStruct((n_tiles,), jnp.int32)],
        scratch_shapes=[pltpu.SMEM((tiles,), jnp.int32)],
        name="routing_tables",
    )(cnt_flat)


def _moe_kernel(tile_expert_ref, n_active_ref, src_ref, nxt_ref, par_ref, xs_hbm, wi_hbm, bi_ref, wo_hbm, bo_ref,
                y_ref, xbuf, wi_f32, wo_f32, wi_bf, wo_bf, sem, wsem):
    j = pl.program_id(0)
    tm = xbuf.shape[1]
    d = y_ref.shape[1]
    f = wo_bf.shape[0]
    cpt = tm // ROW_ALIGN
    slot = j % 2
    n_active = n_active_ref[0]
    expert = tile_expert_ref[j]

    def start_tile(tile, s):
        for c in range(cpt):
            row = pl.multiple_of(src_ref[tile * cpt + c], ROW_ALIGN)
            pltpu.make_async_copy(xs_hbm.at[pl.ds(row, ROW_ALIGN), :],
                                  xbuf.at[s, pl.ds(c * ROW_ALIGN, ROW_ALIGN), :], sem.at[s]).start()

    def wait_tile(s):
        pltpu.make_async_copy(xs_hbm.at[pl.ds(0, tm), :], xbuf.at[s], sem.at[s]).wait()

    def weight_copies(e, s):
        return (pltpu.make_async_copy(wi_hbm.at[e], wi_f32.at[s], wsem.at[0, s]),
                pltpu.make_async_copy(wo_hbm.at[e], wo_f32.at[s], wsem.at[1, s]))

    @pl.when(j == 0)
    def _():
        start_tile(0, 0)
        for cp in weight_copies(expert, par_ref[0]):
            cp.start()

    @pl.when(j + 1 < n_active)
    def _():
        start_tile(j + 1, 1 - slot)

    @pl.when(j < n_active)
    def _():
        @pl.when((j == 0) | (expert != tile_expert_ref[jnp.maximum(j - 1, 0)]))
        def _():
            ws = par_ref[j]
            for cp in weight_copies(expert, ws):
                cp.wait()

            @pl.when(nxt_ref[j] >= 0)
            def _():
                for cp in weight_copies(nxt_ref[j], 1 - ws):
                    cp.start()

            wi_bf[...] = wi_f32[ws].astype(BF16)
            wo_bf[...] = wo_f32[ws].astype(BF16)

        wait_tile(slot)
        rows = xbuf[slot]
        hh = _dot(rows[:, :d], wi_bf[...]) + bi_ref[0]
        gate = jnp.minimum(hh[:, :f], SWIGLU_LIMIT)
        lin = jnp.clip(hh[:, f:], -SWIGLU_LIMIT, SWIGLU_LIMIT)
        half_gate = 0.5 * gate
        act = (lin + 1.0) * (half_gate + half_gate * jnp.tanh((0.5 * SWIGLU_ALPHA) * gate))
        weight = jnp.sum(rows[:, d:].astype(F32), axis=-1, keepdims=True)
        y_ref[...] = ((_dot(act.astype(BF16), wo_bf[...]) + bo_ref[0]) * weight).astype(BF16)

    @pl.when(j >= n_active)
    def _():
        y_ref[...] = jnp.zeros_like(y_ref)


def _moe(xs, tables, w_in, b_in, w_out, b_out, *, tm, n_tiles):
    tile_expert, n_active, moe_src, _, nxt, par = tables
    e, d, f2 = w_in.shape
    f = w_out.shape[1]
    bias = lambda width: pl.BlockSpec((1, 1, width), lambda j, te, *_: (te[j], 0, 0))
    grid_spec = pltpu.PrefetchScalarGridSpec(
        num_scalar_prefetch=5,
        grid=(n_tiles,),
        in_specs=[pl.BlockSpec(memory_space=pl.ANY),
                  pl.BlockSpec(memory_space=pl.ANY),
                  bias(f2),
                  pl.BlockSpec(memory_space=pl.ANY),
                  bias(d)],
        out_specs=pl.BlockSpec((tm, d), lambda j, *_: (j, 0)),
        scratch_shapes=[pltpu.VMEM((2, tm, xs.shape[1]), BF16),
                        pltpu.VMEM((2, d, f2), F32),
                        pltpu.VMEM((2, f, d), F32),
                        pltpu.VMEM((d, f2), BF16),
                        pltpu.VMEM((f, d), BF16),
                        pltpu.SemaphoreType.DMA((2,)),
                        pltpu.SemaphoreType.DMA((2, 2))],
    )
    return pl.pallas_call(
        _moe_kernel,
        grid_spec=grid_spec,
        out_shape=jax.ShapeDtypeStruct((n_tiles * tm, d), BF16),
        compiler_params=_cparams(("arbitrary",)),
        name="moe_experts",
    )(tile_expert, n_active, moe_src, nxt, par, xs, w_in, b_in.reshape(e, 1, f2), w_out, b_out.reshape(e, 1, d))


def _combine_kernel(src_ref, x1_ref, r2_ref, mod_ref, gf_ref, y_hbm, o_ref, ybuf, sem,
                    *, per_row, tiles_per_seq, tile0, stride, single):
    i = pl.program_id(0)
    nt = pl.num_programs(0)
    tt, d = x1_ref.shape
    lmax = ybuf.shape[1]
    nch = lmax // ROW_ALIGN
    slot = i % 2
    b = i // tiles_per_seq

    def start_tile(tile, s):
        for c in range(nch):
            row = pl.multiple_of(src_ref[(tile0 + tile) * stride + c], ROW_ALIGN)
            pltpu.make_async_copy(y_hbm.at[pl.ds(row, ROW_ALIGN), :],
                                  ybuf.at[s, pl.ds(c * ROW_ALIGN, ROW_ALIGN), :], sem.at[s]).start()

    def wait_tile(s):
        pltpu.make_async_copy(y_hbm.at[pl.ds(0, lmax), :], ybuf.at[s], sem.at[s]).wait()

    @pl.when(i == 0)
    def _():
        start_tile(0, 0)

    if not single:
        start_tile(jnp.minimum(i + 1, nt - 1), 1 - slot)
    wait_tile(slot)
    r2 = r2_ref[...].astype(jnp.int32).astype(jnp.int16)
    one = jnp.ones((tt, SORT_BLOCK), BF16)
    zero = jnp.zeros((tt, SORT_BLOCK), BF16)
    moe = jnp.zeros((tt, d), F32)
    for cb in range(lmax // SORT_BLOCK):
        col = (lax.broadcasted_iota(jnp.int32, (tt, SORT_BLOCK), 1) + cb * SORT_BLOCK).astype(jnp.int16)
        hit = col == r2[:, 0:1]
        for kk in range(1, TOP_K):
            hit = hit | (col == r2[:, kk:kk + 1])
        moe = moe + _dot(jnp.where(hit, one, zero), ybuf[slot, cb * SORT_BLOCK:(cb + 1) * SORT_BLOCK, :])
    x2 = x1_ref[...] + _mod_rows(mod_ref, 5, d, per_row, b) * moe
    o_ref[...] = _rms(x2) * gf_ref[...]

    if not single:
        @pl.when(i == nt - 1)
        def _():
            wait_tile(1 - slot)


def _combine(comb_src, x1, r2, mod, gf, ys, *, tt, lmax, per_row, tiles_per_seq, tile0, stride, r2_block0):
    n, d = x1.shape
    kern = functools.partial(_combine_kernel, per_row=per_row, tiles_per_seq=tiles_per_seq, tile0=tile0,
                             stride=stride, single=(n // tt == 1))
    mod_spec = (pl.BlockSpec((tt, mod.shape[1]), lambda i, s: (i, 0)) if per_row
                else pl.BlockSpec(mod.shape, lambda i, s: (0, 0)))
    grid_spec = pltpu.PrefetchScalarGridSpec(
        num_scalar_prefetch=1,
        grid=(n // tt,),
        in_specs=[pl.BlockSpec((tt, d), lambda i, s: (i, 0)),
                  pl.BlockSpec((tt, LANES), lambda i, s: (r2_block0 + i, 0)),
                  mod_spec,
                  pl.BlockSpec((1, d), lambda i, s: (0, 0)),
                  pl.BlockSpec(memory_space=pl.ANY)],
        out_specs=pl.BlockSpec((tt, d), lambda i, s: (i, 0)),
        scratch_shapes=[pltpu.VMEM((2, lmax, d), BF16), pltpu.SemaphoreType.DMA((2,))],
    )
    return pl.pallas_call(
        kern,
        grid_spec=grid_spec,
        out_shape=jax.ShapeDtypeStruct((n, d), F32),
        compiler_params=_cparams(("arbitrary",)),
        name="combine_final_per_row" if per_row else "combine_final",
    )(comb_src, x1, r2, mod, gf.reshape(1, d), ys)


def kernel(x_prompt, x_sample, c_prompt, c_sample, state_pool, state_ret, w_ada, b_ada, norm1_g, w_in,
           w_pool, pool_scale, w_up_pool, w_up_ret, w_out, norm2_g, w_router, b_router, w_expert_in,
           b_expert_in, w_expert_out, b_expert_out, final_norm_g):
    batch, seq, d = x_prompt.shape
    n_dec = x_sample.shape[0]
    n_prompt = batch * seq
    n_experts = w_router.shape[-1]
    xp = x_prompt.reshape(n_prompt, d)
    xs = x_sample.reshape(n_dec, d)

    w_in_bf = w_in[0].astype(BF16)
    wa_bf = w_up_pool[0].astype(BF16)
    wb_bf = w_up_ret[0].astype(BF16)
    wo_bf = w_out[0].astype(BF16)
    assert n_experts % 8 == 0
    wr_pad = w_router[0].T
    br_pad = jnp.broadcast_to(b_router[0][:, None], (n_experts, LANES))

    c_all = jnp.concatenate([c_prompt, c_sample], axis=0)
    c_all = jnp.pad(c_all, ((0, -c_all.shape[0] % 16), (0, 0)))
    mod = _modulation(c_all, w_ada[0], b_ada[0])
    mod_p, mod_s = mod[:batch], mod[batch:batch + n_dec]

    tm = min(512, seq)
    u_p, qkvg_p = _in_proj(xp, mod_p, norm1_g[0], w_in_bf, tm=tm, per_row=False, tiles_per_seq=seq // tm,
                           split=True)
    proj_s = _in_proj(xs, mod_s, norm1_g[0], w_in_bf, tm=n_dec, per_row=True, tiles_per_seq=1, split=False)

    ya_p, tail_p = _pool_prompt(u_p, w_pool[0], pool_scale[0], batch=batch, seq=seq, tt=min(1024, seq))
    ya_s, pool_s_t = _pool_decode(jnp.transpose(state_pool[0], (1, 0, 2)), proj_s, w_pool[0], pool_scale[0])
    yb_p, ret_p = _ret_prompt(qkvg_p, batch=batch, seq=seq, d_model=d, chunk=min(RET_CHUNK, seq))
    yb_s, ret_s = _ret_decode(proj_s, state_ret[0], d_model=d, bs=8)

    tmix = min(512, seq)
    x1_p, h2_p, route_p = _mix(xp, ya_p, yb_p, qkvg_p, mod_p, norm2_g[0], wa_bf, wb_bf, wo_bf, wr_pad, br_pad,
                               tm=tmix, per_row=False, tiles_per_seq=seq // tmix, n_experts=n_experts,
                               ga_blk=6)
    x1_s, h2_s, route_s = _mix(xs, ya_s, yb_s, proj_s, mod_s, norm2_g[0], wa_bf, wb_bf, wo_bf, wr_pad, br_pad,
                               tm=n_dec, per_row=True, tiles_per_seq=1, n_experts=n_experts, ga_blk=7)

    tt = min(TOKEN_TILE, seq)
    assert n_dec <= tt
    p_tiles = n_prompt // tt
    lmax = _sorted_rows(tt, n_experts)
    xs_buf, r2, cnt_t = _dispatch(h2_p, route_p, h2_s, route_s, tt=tt, lmax=lmax)

    tmo = MOE_ROW_TILE
    cnt = cnt_t.reshape(p_tiles + 1, 8, LANES)[:, 0, :n_experts].astype(jnp.int32).reshape(-1)
    max_rows = (n_prompt + n_dec) * TOP_K + (p_tiles + 1) * n_experts * (ROW_ALIGN - 1)
    n_tiles = max_rows // tmo + n_experts
    stride = lmax // ROW_ALIGN
    tables = _routing_tables(cnt, tiles=p_tiles + 1, n_e=n_experts, tmo=tmo, n_tiles=n_tiles, stride=stride,
                             lmax=lmax)
    comb_src = tables[3]

    ys = _moe(xs_buf, tables, w_expert_in[0], b_expert_in[0], w_expert_out[0], b_expert_out[0], tm=tmo,
              n_tiles=n_tiles)

    assert n_prompt % n_dec == 0
    y_p = _combine(comb_src, x1_p, r2, mod_p, final_norm_g, ys, tt=tt, lmax=lmax, per_row=False,
                   tiles_per_seq=seq // tt, tile0=0, stride=stride, r2_block0=0)
    y_s = _combine(comb_src, x1_s, r2, mod_s, final_norm_g, ys, tt=n_dec, lmax=min(lmax, _sorted_rows(n_dec, n_experts)),
                   per_row=True, tiles_per_seq=1, tile0=p_tiles, stride=stride, r2_block0=n_prompt // n_dec)

    return (y_p.reshape(batch, seq, d),
            y_s.reshape(n_dec, 1, d),
            tail_p[None, :, 1:, :],
            ret_p[None],
            jnp.transpose(pool_s_t, (1, 0, 2))[None],
            ret_s[None])
```

```python
import functools

import numpy as np
import jax
import jax.numpy as jnp
from jax import lax
from jax.experimental import pallas as pl
from jax.experimental.pallas import tpu as pltpu

F32 = jnp.float32
BF16 = jnp.bfloat16

POOL_WINDOWS = (2, 4, 8, 16)
POOL_BUF = max(POOL_WINDOWS) - 1
RET_HEADS = 4
RET_CHUNK = 256
ROPE_BASE = 10000.0
PAST_LEN = 16384
TOP_K = 4
SWIGLU_LIMIT = 7.0
SWIGLU_ALPHA = 1.702
N_MOD = 6
EPS = 1e-6
LANES = 128
ROW_ALIGN = 16
SORT_BLOCK = 512
POS_SPLIT = 64.0
POS_LANE = 16
TOKEN_TILE = 512
MOE_ROW_TILE = 512
VMEM_LIMIT = 56 * 1024 * 1024


def _cparams(sem):
    return pltpu.CompilerParams(dimension_semantics=sem, vmem_limit_bytes=VMEM_LIMIT)


def _dot(a, b):
    return jnp.dot(a, b, preferred_element_type=F32)


def _dot3(a, b):
    a_hi = a.astype(BF16)
    a_lo = (a - a_hi.astype(F32)).astype(BF16)
    b_hi = b.astype(BF16)
    b_lo = (b - b_hi.astype(F32)).astype(BF16)
    return _dot(a_hi, b_hi) + _dot(a_hi, b_lo) + _dot(a_lo, b_hi)


def _dot3_nt(a, b):
    nt = lambda x, y: lax.dot_general(x, y, (((1,), (1,)), ((), ())), preferred_element_type=F32)
    a_hi = a.astype(BF16)
    a_lo = (a - a_hi.astype(F32)).astype(BF16)
    b_hi = b.astype(BF16)
    b_lo = (b - b_hi.astype(F32)).astype(BF16)
    return nt(a_hi, b_hi) + nt(a_hi, b_lo) + nt(a_lo, b_hi)


def _sigmoid(x):
    return 0.5 + 0.5 * jnp.tanh(0.5 * x)


def _silu(x):
    half = 0.5 * x
    return half + half * jnp.tanh(half)


def _rms(x):
    return x * lax.rsqrt(jnp.mean(x * x, axis=-1, keepdims=True) + EPS)


def _round_up(x, m):
    return (x + m - 1) // m * m


def _mod_kernel(c_ref, w_ref, b_ref, o_ref):
    c = c_ref[...]
    o_ref[...] = _dot3(_silu(c_ref[...]), w_ref[...]) + b_ref[...]


def _modulation(c, w_ada, b_ada):
    rows, d = c.shape
    n = w_ada.shape[1]
    tn = d
    return pl.pallas_call(
        _mod_kernel,
        grid=(n // tn,),
        in_specs=[pl.BlockSpec((rows, d), lambda j: (0, 0)),
                  pl.BlockSpec((d, tn), lambda j: (0, j)),
                  pl.BlockSpec((1, tn), lambda j: (0, j))],
        out_specs=pl.BlockSpec((rows, tn), lambda j: (0, j)),
        out_shape=jax.ShapeDtypeStruct((rows, n), F32),
        compiler_params=_cparams(("arbitrary",)),
        name="adaln_modulation",
    )(c, w_ada, b_ada.reshape(1, n))


def _mod_rows(mod_ref, which, d, per_row, b):
    if per_row:
        return mod_ref[:, which * d:(which + 1) * d]
    return mod_ref[pl.ds(b, 1), which * d:(which + 1) * d]


def _in_proj_kernel(x_ref, mod_ref, g_ref, w_ref, *outs, per_row, tiles_per_seq, split):
    i = pl.program_id(0)
    d = x_ref.shape[1]
    b = i // tiles_per_seq
    sh = _mod_rows(mod_ref, 0, d, per_row, b)
    sc = _mod_rows(mod_ref, 1, d, per_row, b)
    h = (_rms(x_ref[...]) * g_ref[...] * (1.0 + sc) + sh).astype(BF16)
    for c in range(w_ref.shape[1] // d):
        acc = _dot(h, w_ref[:, c * d:(c + 1) * d])
        if not split:
            outs[0][:, c * d:(c + 1) * d] = acc
        elif c == 0:
            outs[0][...] = acc
        else:
            outs[1][:, (c - 1) * d:c * d] = acc.astype(BF16)


def _in_proj(x, mod, g, w_bf, *, tm, per_row, tiles_per_seq, split):
    n, d = x.shape
    width = w_bf.shape[1]
    kern = functools.partial(_in_proj_kernel, per_row=per_row, tiles_per_seq=tiles_per_seq, split=split)
    mod_spec = (pl.BlockSpec((tm, mod.shape[1]), lambda i: (i, 0)) if per_row
                else pl.BlockSpec(mod.shape, lambda i: (0, 0)))
    if split:
        out_specs = [pl.BlockSpec((tm, d), lambda i: (i, 0)),
                     pl.BlockSpec((tm, width - d), lambda i: (i, 0))]
        out_shape = [jax.ShapeDtypeStruct((n, d), F32), jax.ShapeDtypeStruct((n, width - d), BF16)]
    else:
        out_specs = pl.BlockSpec((tm, width), lambda i: (i, 0))
        out_shape = jax.ShapeDtypeStruct((n, width), F32)
    return pl.pallas_call(
        kern,
        grid=(n // tm,),
        in_specs=[pl.BlockSpec((tm, d), lambda i: (i, 0)),
                  mod_spec,
                  pl.BlockSpec((1, d), lambda i: (0, 0)),
                  pl.BlockSpec((d, width), lambda i: (0, 0), pipeline_mode=pl.Buffered(1))],
        out_specs=out_specs,
        out_shape=out_shape,
        compiler_params=_cparams(("arbitrary",)),
        name="in_proj_per_row" if per_row else "in_proj",
    )(x, mod, g.reshape(1, d), w_bf)


def _pool_prompt_kernel(u_ref, w_ref, s_ref, y_ref, tail_ref, prev_ref, *, tt):
    t = pl.program_id(1)
    nt = pl.num_programs(1)
    hist = prev_ref.shape[0]
    gw = w_ref.shape[1]

    @pl.when(t == 0)
    def _():
        prev_ref[...] = jnp.zeros_like(prev_ref)

    u = u_ref[...]
    ext = jnp.concatenate([prev_ref[...], u], axis=0)
    pos = t * tt + lax.broadcasted_iota(jnp.int32, (tt, 1), 0)
    outs = []
    for gi, w in enumerate(POOL_WINDOWS):
        cols = slice(gi * gw, (gi + 1) * gw)
        run = ext[:, cols]
        span = 1
        while span < w:
            run = run + pltpu.roll(run, span, 0)
            span *= 2
        inv_cnt = 1.0 / jnp.minimum(w, pos + 1).astype(F32)
        pooled = run[hist:, :] * inv_cnt - u[:, cols]
        outs.append(_dot(pooled.astype(BF16), w_ref[gi].astype(BF16)))
    y = jnp.concatenate(outs, axis=-1) * s_ref[...]
    y_ref[...] = y.astype(BF16)
    prev_ref[...] = u[tt - hist:, :]

    @pl.when(t == nt - 1)
    def _():
        tail_ref[0] = u[tt - hist:, :]


def _pool_prompt(proj, w_pool, pool_scale, *, batch, seq, tt):
    p = pool_scale.shape[0]
    hist = POOL_BUF + 1
    nt = seq // tt
    kern = functools.partial(_pool_prompt_kernel, tt=tt)
    return pl.pallas_call(
        kern,
        grid=(batch, nt),
        in_specs=[pl.BlockSpec((tt, p), lambda b, t: (b * nt + t, 0)),
                  pl.BlockSpec(w_pool.shape, lambda b, t: (0, 0, 0)),
                  pl.BlockSpec((1, p), lambda b, t: (0, 0))],
        out_specs=[pl.BlockSpec((tt, p), lambda b, t: (b * nt + t, 0)),
                   pl.BlockSpec((1, hist, p), lambda b, t: (b, 0, 0))],
        out_shape=[jax.ShapeDtypeStruct((batch * seq, p), BF16),
                   jax.ShapeDtypeStruct((batch, hist, p), F32)],
        scratch_shapes=[pltpu.VMEM((hist, p), F32)],
        compiler_params=_cparams(("arbitrary", "arbitrary")),
        name="pool_prompt",
    )(proj, w_pool, pool_scale.reshape(1, p))


def _pool_decode_kernel(st_ref, u_ref, w_ref, s_ref, y_ref, new_ref):
    buf = st_ref.shape[0]
    gw = w_ref.shape[1]
    u = u_ref[...]
    outs = []
    run = u
    used = 0
    for gi, w in enumerate(POOL_WINDOWS):
        while used < w - 1:
            run = run + st_ref[buf - 1 - used]
            used += 1
        cols = slice(gi * gw, (gi + 1) * gw)
        pooled = run[:, cols] / float(min(w, PAST_LEN + 1)) - u[:, cols]
        outs.append(_dot(pooled.astype(BF16), w_ref[gi].astype(BF16)))
    y_ref[...] = (jnp.concatenate(outs, axis=-1) * s_ref[...]).astype(BF16)
    for r in range(buf - 1):
        new_ref[r] = st_ref[r + 1]
    new_ref[buf - 1] = u


def _pool_decode(state_t, proj_s, w_pool, pool_scale):
    buf, n, p = state_t.shape
    return pl.pallas_call(
        _pool_decode_kernel,
        grid=(1,),
        in_specs=[pl.BlockSpec((buf, n, p), lambda i: (0, 0, 0)),
                  pl.BlockSpec((n, p), lambda i: (0, 0)),
                  pl.BlockSpec(w_pool.shape, lambda i: (0, 0, 0)),
                  pl.BlockSpec((1, p), lambda i: (0, 0))],
        out_specs=[pl.BlockSpec((n, p), lambda i: (0, 0)),
                   pl.BlockSpec((buf, n, p), lambda i: (0, 0, 0))],
        out_shape=[jax.ShapeDtypeStruct((n, p), BF16),
                   jax.ShapeDtypeStruct((buf, n, p), F32)],
        compiler_params=_cparams(("arbitrary",)),
        name="pool_decode",
    )(state_t, proj_s, w_pool, pool_scale.reshape(1, p))


def _log_gamma(h):
    return float(np.log(1.0 - 2.0 ** (-5.0 - h)))


def _rope_tables(pos, half):
    inv = np.power(ROPE_BASE, -np.arange(half, dtype=np.float64) / half)
    ang = np.asarray(pos, np.float64)[:, None] * inv[None, :]
    return jnp.asarray(np.cos(ang), F32), jnp.asarray(np.sin(ang), F32)


def _rotary(x, cos, sin):
    half = x.shape[-1] // 2
    x1, x2 = x[:, :half], x[:, half:]
    return jnp.concatenate([x1 * cos - x2 * sin, x2 * cos + x1 * sin], axis=-1)


def _ret_tables(c, key_scale):
    i = np.arange(c, dtype=np.float64)
    diff = i[:, None] - i[None, :]
    dec, qs, ks = [], [], []
    for h in range(RET_HEADS):
        lg = _log_gamma(h)
        dec.append(key_scale * np.where(diff >= 0, np.exp(lg * np.maximum(diff, 0.0)), 0.0))
        qs.append(np.broadcast_to(np.exp(lg * (i + 1.0))[:, None], (c, LANES)))
        ks.append(key_scale * np.broadcast_to(np.exp(lg * (c - 1.0 - i))[:, None], (c, LANES)))
    return (jnp.asarray(np.stack(dec), F32), jnp.asarray(np.stack(qs), F32), jnp.asarray(np.stack(ks), F32))


def _ret_prompt_kernel(q_ref, k_ref, v_ref, g_ref, cos_ref, sin_ref, dec_ref, qs_ref, ks_ref,
                       y_ref, s_out_ref, s_ref, *, chunk_decay):
    c = pl.program_id(1)
    nc = pl.num_programs(1)
    heads, dk, dv = s_ref.shape

    @pl.when(c == 0)
    def _():
        s_ref[...] = jnp.zeros_like(s_ref)

    cos, sin = cos_ref[...], sin_ref[...]
    reps = dk // LANES
    for h in range(heads):
        q = _rotary(q_ref[:, h * dk:(h + 1) * dk].astype(F32), cos, sin)
        k = _rotary(k_ref[:, h * dk:(h + 1) * dk].astype(F32), cos, sin)
        v = v_ref[:, h * dv:(h + 1) * dv]
        qs = jnp.concatenate([qs_ref[h]] * reps, axis=-1)
        ks = jnp.concatenate([ks_ref[h]] * reps, axis=-1)
        scores = lax.dot_general(q.astype(BF16), k.astype(BF16), (((1,), (1,)), ((), ())),
                                 preferred_element_type=F32) * dec_ref[h]
        state = s_ref[h]
        o = _dot(scores.astype(BF16), v) + _dot((q * qs).astype(BF16), state.astype(BF16))
        kv = lax.dot_general((k * ks).astype(BF16), v, (((0,), (0,)), ((), ())), preferred_element_type=F32)
        new_state = chunk_decay[h] * state + kv
        s_ref[h] = new_state
        g = g_ref[:, h * dv:(h + 1) * dv].astype(F32)
        y_ref[:, h * dv:(h + 1) * dv] = (_silu(g) * _rms(o)).astype(BF16)

        @pl.when(c == nc - 1)
        def _():
            s_out_ref[0, h] = new_state


def _ret_prompt(qkvg, *, batch, seq, d_model, chunk):
    heads = RET_HEADS
    dk = d_model // heads
    dv = 2 * d_model // heads
    nc = seq // chunk
    cos, sin = _rope_tables(np.arange(seq), dk // 2)
    key_scale = dk ** -0.5
    assert np.log2(key_scale) == np.round(np.log2(key_scale))
    dec, qs, ks = _ret_tables(chunk, key_scale)
    chunk_decay = tuple(float(np.exp(_log_gamma(h) * chunk)) for h in range(heads))
    kern = functools.partial(_ret_prompt_kernel, chunk_decay=chunk_decay)
    const = lambda a: pl.BlockSpec(a.shape, lambda b, c: (0,) * a.ndim)
    return pl.pallas_call(
        kern,
        grid=(batch, nc),
        in_specs=[pl.BlockSpec((chunk, d_model), lambda b, c: (b * nc + c, 0)),
                  pl.BlockSpec((chunk, d_model), lambda b, c: (b * nc + c, 1)),
                  pl.BlockSpec((chunk, 2 * d_model), lambda b, c: (b * nc + c, 1)),
                  pl.BlockSpec((chunk, 2 * d_model), lambda b, c: (b * nc + c, 2)),
                  pl.BlockSpec((chunk, dk // 2), lambda b, c: (c, 0)),
                  pl.BlockSpec((chunk, dk // 2), lambda b, c: (c, 0)),
                  const(dec), const(qs), const(ks)],
        out_specs=[pl.BlockSpec((chunk, heads * dv), lambda b, c: (b * nc + c, 0)),
                   pl.BlockSpec((1, heads, dk, dv), lambda b, c: (b, 0, 0, 0))],
        out_shape=[jax.ShapeDtypeStruct((batch * seq, heads * dv), BF16),
                   jax.ShapeDtypeStruct((batch, heads, dk, dv), F32)],
        scratch_shapes=[pltpu.VMEM((heads, dk, dv), F32)],
        compiler_params=_cparams(("arbitrary", "arbitrary")),
        name="retention_prompt",
    )(qkvg, qkvg, qkvg, qkvg, cos, sin, dec, qs, ks)


def _ret_decode_kernel(q_ref, k_ref, v_ref, g_ref, cos_ref, sin_ref, s_ref, y_ref, s_out_ref,
                       kt_ref, qd_ref, o_ref, *, gammas, bs):
    h = pl.program_id(0)
    blk = pl.program_id(1)
    nblk = pl.num_programs(1)
    n, dk = q_ref.shape
    gamma = jnp.where(h == 0, gammas[0], jnp.where(h == 1, gammas[1], jnp.where(h == 2, gammas[2], gammas[3])))

    @pl.when(blk == 0)
    def _():
        cos, sin = cos_ref[...], sin_ref[...]
        q = _rotary(q_ref[...], cos, sin)
        k = _rotary(k_ref[...], cos, sin) * (dk ** -0.5)
        kt_ref[...] = k.T
        qd_ref[...] = (q * gamma).astype(BF16)
        o_ref[...] = jnp.sum(q * k, axis=-1, keepdims=True) * v_ref[...]

    vb = v_ref[...].astype(BF16)
    rows = lax.broadcasted_iota(jnp.int32, (n, 1), 0)
    lanes = lax.broadcasted_iota(jnp.int32, (1, n), 1)
    for i in range(bs):
        tok = blk * bs + i
        state = s_ref[i, 0]
        o_ref[...] += jnp.where(rows == tok, _dot(qd_ref[...], state.astype(BF16)), 0.0)
        k_col = jnp.where(lanes == tok, kt_ref[...], 0.0).astype(BF16)
        s_out_ref[i, 0] = gamma * state + _dot(k_col, vb)

    @pl.when(blk == nblk - 1)
    def _():
        g = g_ref[...]
        y_ref[...] = (_silu(g) * _rms(o_ref[...])).astype(BF16)


def _ret_decode(proj_s, state, *, d_model, bs):
    n = proj_s.shape[0]
    heads = RET_HEADS
    dk = d_model // heads
    dv = 2 * d_model // heads
    cos, sin = _rope_tables(np.array([PAST_LEN]), dk // 2)
    gammas = tuple(float(np.exp(_log_gamma(h))) for h in range(heads))
    q0, k0, v0, g0 = d_model // dk, 2 * d_model // dk, 3 * d_model // dv, 5 * d_model // dv
    kern = functools.partial(_ret_decode_kernel, gammas=gammas, bs=bs)
    return pl.pallas_call(
        kern,
        grid=(heads, n // bs),
        in_specs=[pl.BlockSpec((n, dk), lambda h, b: (0, q0 + h)),
                  pl.BlockSpec((n, dk), lambda h, b: (0, k0 + h)),
                  pl.BlockSpec((n, dv), lambda h, b: (0, v0 + h)),
                  pl.BlockSpec((n, dv), lambda h, b: (0, g0 + h)),
                  pl.BlockSpec((1, dk // 2), lambda h, b: (0, 0)),
                  pl.BlockSpec((1, dk // 2), lambda h, b: (0, 0)),
                  pl.BlockSpec((bs, 1, dk, dv), lambda h, b: (b, h, 0, 0))],
        out_specs=[pl.BlockSpec((n, dv), lambda h, b: (0, h)),
                   pl.BlockSpec((bs, 1, dk, dv), lambda h, b: (b, h, 0, 0))],
        out_shape=[jax.ShapeDtypeStruct((n, heads * dv), BF16),
                   jax.ShapeDtypeStruct(state.shape, F32)],
        scratch_shapes=[pltpu.VMEM((dk, n), F32), pltpu.VMEM((n, dk), BF16), pltpu.VMEM((n, dv), F32)],
        compiler_params=_cparams(("arbitrary", "arbitrary")),
        name="retention_decode",
    )(proj_s, proj_s, proj_s, proj_s, cos, sin, state)


def _mix_kernel(x_ref, ya_ref, yb_ref, ga_ref, gb_ref, mod_ref, g2_ref, wa_ref, wb_ref, wo_ref,
                wr_ref, br_ref, x1_ref, h2_ref, route_ref, *, per_row, tiles_per_seq, n_experts):
    i = pl.program_id(0)
    tm, d = x_ref.shape
    b = i // tiles_per_seq

    merged = (_sigmoid(ga_ref[...].astype(F32)) * _dot(ya_ref[...], wa_ref[...])
              + _sigmoid(gb_ref[...].astype(F32)) * _dot(yb_ref[...], wb_ref[...]))
    mix = _dot(merged.astype(BF16), wo_ref[...])
    x1 = x_ref[...] + _mod_rows(mod_ref, 2, d, per_row, b) * mix
    x1_ref[...] = x1
    h2 = (_rms(x1) * g2_ref[...] * (1.0 + _mod_rows(mod_ref, 4, d, per_row, b))
          + _mod_rows(mod_ref, 3, d, per_row, b))
    h2_ref[...] = h2.astype(BF16)

    logits_t = _dot3_nt(wr_ref[...], h2) + br_ref[:, 0:1]
    row = lax.broadcasted_iota(jnp.int32, (n_experts, tm), 0)
    neg = jnp.float32(-jnp.inf)
    work = logits_t
    vals, idxs = [], []
    for _ in range(TOP_K):
        m = jnp.max(work, axis=0, keepdims=True)
        idx = jnp.min(jnp.where(work == m, row, n_experts), axis=0, keepdims=True)
        vals.append(m)
        idxs.append(idx)
        work = jnp.where(row == idx, neg, work)
    exps = [jnp.exp(v - vals[0]) for v in vals]
    denom = exps[0] + exps[1] + exps[2] + exps[3]
    route_ref[...] = jnp.concatenate([idx.astype(F32) for idx in idxs] + [e / denom for e in exps]
                                     + [jnp.zeros((LANES - 2 * TOP_K, tm), F32)], axis=0)


def _mix(x, ya, yb, proj, mod, g2, wa_bf, wb_bf, wo_bf, wr_pad, br_pad, *, tm, per_row, tiles_per_seq,
         n_experts, ga_blk):
    n, d = x.shape
    gb_blk = ga_blk + 1
    kern = functools.partial(_mix_kernel, per_row=per_row, tiles_per_seq=tiles_per_seq, n_experts=n_experts)
    mod_spec = (pl.BlockSpec((tm, mod.shape[1]), lambda i: (i, 0)) if per_row
                else pl.BlockSpec(mod.shape, lambda i: (0, 0)))
    full = lambda a: pl.BlockSpec(a.shape, lambda i: (0,) * a.ndim)
    return pl.pallas_call(
        kern,
        grid=(n // tm,),
        in_specs=[pl.BlockSpec((tm, d), lambda i: (i, 0)),
                  pl.BlockSpec((tm, ya.shape[1]), lambda i: (i, 0)),
                  pl.BlockSpec((tm, yb.shape[1]), lambda i: (i, 0)),
                  pl.BlockSpec((tm, d), lambda i: (i, ga_blk)),
                  pl.BlockSpec((tm, d), lambda i: (i, gb_blk)),
                  mod_spec,
                  pl.BlockSpec((1, d), lambda i: (0, 0)),
                  full(wa_bf), full(wb_bf), full(wo_bf), full(wr_pad), full(br_pad)],
        out_specs=[pl.BlockSpec((tm, d), lambda i: (i, 0)),
                   pl.BlockSpec((tm, d), lambda i: (i, 0)),
                   pl.BlockSpec((LANES, tm), lambda i: (0, i))],
        out_shape=[jax.ShapeDtypeStruct((n, d), F32),
                   jax.ShapeDtypeStruct((n, d), BF16),
                   jax.ShapeDtypeStruct((LANES, n), F32)],
        compiler_params=_cparams(("arbitrary",)),
        name="mix_route_per_row" if per_row else "mix_route",
    )(x, ya, yb, proj, proj, mod, g2.reshape(1, d), wa_bf, wb_bf, wo_bf, wr_pad, br_pad)


def _sorted_rows(tt, n_experts):
    return _round_up(tt * TOP_K + n_experts * (ROW_ALIGN - 1) + ROW_ALIGN, SORT_BLOCK)


def _dispatch_kernel(hp_ref, rp_ref, hd_ref, rd_ref, before_ref, below_ref, xs_ref, r2_ref, cnt_ref, *, lmax):
    tt = hp_ref.shape[0]
    n_e = cnt_ref.shape[0]
    is_dec = pl.program_id(0) == pl.num_programs(0) - 1
    route_t = jnp.where(is_dec, rd_ref[...], rp_ref[...])
    expert = lax.broadcasted_iota(jnp.int32, (n_e, tt), 0).astype(F32)
    hits = [expert == route_t[kk:kk + 1, :] for kk in range(TOP_K)]
    sel = jnp.zeros((n_e, tt), F32)
    for hit in hits:
        sel = sel + jnp.where(hit, 1.0, 0.0)
    cnt = jnp.sum(sel, axis=1, keepdims=True)
    seg = jnp.broadcast_to(jnp.ceil(cnt / ROW_ALIGN) * ROW_ALIGN, (n_e, LANES))
    off = _dot3(below_ref[...], seg)[:, 0:1]
    place = _dot(sel.astype(BF16), before_ref[...]) + off
    w_hi, w_lo, pos_a, pos_b, pos_all = [], [], [], [], []
    for kk in range(TOP_K):
        found = jnp.sum(jnp.where(hits[kk], 1.0, 0.0), axis=0, keepdims=True)
        pos = jnp.where(found > 0.0, jnp.sum(jnp.where(hits[kk], place, 0.0), axis=0, keepdims=True), -1.0)
        w = route_t[TOP_K + kk:TOP_K + kk + 1, :]
        hi = w.astype(BF16).astype(F32)
        a = jnp.floor(pos / POS_SPLIT)
        w_hi.append(hi)
        w_lo.append(w - hi)
        pos_a.append(a)
        pos_b.append(pos - POS_SPLIT * a)
        pos_all.append(pos)
    info_t = jnp.concatenate(w_hi + w_lo + pos_a + pos_b + pos_all
                             + [jnp.zeros((LANES - 5 * TOP_K, tt), F32)], axis=0)
    side = info_t.T
    r2_ref[...] = side
    cnt_ref[...] = jnp.broadcast_to(cnt, cnt_ref.shape)
    r2t = info_t.astype(jnp.int32).astype(jnp.int16)
    h = jnp.concatenate([jnp.where(is_dec, hd_ref[...], hp_ref[...]), side.astype(BF16)], axis=-1)
    d = hp_ref.shape[1]
    out_lane = lax.broadcasted_iota(jnp.int32, (SORT_BLOCK, LANES), 1)
    one = jnp.ones((SORT_BLOCK, tt), BF16)
    zero = jnp.zeros((SORT_BLOCK, tt), BF16)
    for rb in range(lmax // SORT_BLOCK):
        rio = (lax.broadcasted_iota(jnp.int32, (SORT_BLOCK, tt), 0) + rb * SORT_BLOCK).astype(jnp.int16)
        hit = rio == r2t[POS_LANE:POS_LANE + 1, :]
        for kk in range(1, TOP_K):
            hit = hit | (rio == r2t[POS_LANE + kk:POS_LANE + kk + 1, :])
        perm = jnp.where(hit, one, zero)
        res = _dot(perm, h)
        info = res[:, d:]
        row = (lax.broadcasted_iota(jnp.int32, (SORT_BLOCK, LANES), 0) + rb * SORT_BLOCK).astype(F32)
        pos = (POS_SPLIT * pltpu.roll(info, LANES - 2 * TOP_K, 1) + pltpu.roll(info, LANES - 3 * TOP_K, 1))
        mine = jnp.where((pos == row) & (out_lane < TOP_K), 1.0, 0.0)
        mine = mine + pltpu.roll(mine, TOP_K, 1)
        rows = slice(rb * SORT_BLOCK, (rb + 1) * SORT_BLOCK)
        xs_ref[rows, :d] = res[:, :d].astype(BF16)
        xs_ref[rows, d:] = (info * mine).astype(BF16)


def _dispatch(h2_p, route_p, h2_d, route_d, *, tt, lmax, n_experts):
    n, d = h2_p.shape
    p_tiles = n // tt
    tiles = p_tiles + 1
    pad = tt - h2_d.shape[0]
    h2_d = jnp.pad(h2_d, ((0, pad), (0, 0)))
    route_d = jnp.pad(route_d, ((0, 0), (0, pad)), constant_values=-1.0)
    kern = functools.partial(_dispatch_kernel, lmax=lmax)
    before = jnp.asarray(np.triu(np.ones((tt, tt), np.float32), 1), BF16)
    below = jnp.asarray(np.tril(np.ones((n_experts, n_experts), np.float32), -1), F32)
    return pl.pallas_call(
        kern,
        grid=(tiles,),
        in_specs=[pl.BlockSpec((tt, d), lambda i: (jnp.minimum(i, p_tiles - 1), 0)),
                  pl.BlockSpec((LANES, tt), lambda i: (0, jnp.minimum(i, p_tiles - 1))),
                  pl.BlockSpec((tt, d), lambda i: (0, 0)),
                  pl.BlockSpec((LANES, tt), lambda i: (0, 0)),
                  pl.BlockSpec((tt, tt), lambda i: (0, 0)),
                  pl.BlockSpec((n_experts, n_experts), lambda i: (0, 0))],
        out_specs=[pl.BlockSpec((lmax, d + LANES), lambda i: (i, 0)),
                   pl.BlockSpec((tt, LANES), lambda i: (i, 0)),
                   pl.BlockSpec((n_experts, LANES), lambda i: (i, 0))],
        out_shape=[jax.ShapeDtypeStruct((tiles * lmax, d + LANES), BF16),
                   jax.ShapeDtypeStruct((tiles * tt, LANES), F32),
                   jax.ShapeDtypeStruct((tiles * n_experts, LANES), F32)],
        compiler_params=_cparams(("arbitrary",)),
        name="dispatch_sort",
    )(h2_p, route_p, h2_d, route_d, before, below)


def _tables_kernel(cnt_ref, te_ref, na_ref, msrc_ref, csrc_ref, nxt_ref, par_ref, loc_ref, *, tiles, n_e, cpt,
                   stride, lmax, n_tiles, zero_row):
    def fill(ref, n, val):
        def body(i, carry):
            ref[i] = val
            return carry
        lax.fori_loop(0, n, body, 0, unroll=8)

    fill(msrc_ref, n_tiles * cpt, zero_row)
    fill(csrc_ref, tiles * stride, 0)
    fill(loc_ref, tiles, 0)

    def expert_body(e, carry):
        g, run = carry

        def tile_body(i, g):
            nblk = (cnt_ref[i * n_e + e] + (ROW_ALIGN - 1)) // ROW_ALIGN
            loc = loc_ref[i]

            def block_body(q, carry):
                msrc_ref[g + q] = i * lmax + (loc + q) * ROW_ALIGN
                csrc_ref[i * stride + loc + q] = (g + q) * ROW_ALIGN
                return carry

            lax.fori_loop(0, nblk, block_body, 0)
            loc_ref[i] = loc + nblk
            return g + nblk

        g_end = lax.fori_loop(0, tiles, tile_body, g)
        t_first = g // cpt
        t_last = (g_end + (cpt - 1)) // cpt

        def mark(t, carry):
            te_ref[t] = e
            par_ref[t] = run % 2
            return carry

        lax.fori_loop(t_first, t_last, mark, 0)
        return t_last * cpt, jnp.where(t_last > t_first, run + 1, run)

    g, _ = lax.fori_loop(0, n_e, expert_body, (jnp.int32(0), jnp.int32(0)))
    n_act = g // cpt
    na_ref[0] = n_act
    last = te_ref[jnp.maximum(n_act - 1, 0)]

    def mark_rest(t, carry):
        te_ref[t] = last
        par_ref[t] = 0
        nxt_ref[t] = -1
        return carry

    lax.fori_loop(n_act, n_tiles, mark_rest, 0)

    def mark_next(s, nxt):
        t = n_act - 1 - s
        after = te_ref[jnp.minimum(t + 1, n_tiles - 1)]
        nxt = jnp.where((t + 1 < n_act) & (after != te_ref[t]), after, nxt)
        nxt_ref[t] = nxt
        return nxt

    lax.fori_loop(0, n_act, mark_next, jnp.int32(-1))


def _routing_tables(cnt_flat, *, tiles, n_e, tmo, n_tiles, stride, lmax):
    cpt = tmo // ROW_ALIGN
    kern = functools.partial(_tables_kernel, tiles=tiles, n_e=n_e, cpt=cpt, stride=stride, lmax=lmax,
                             n_tiles=n_tiles, zero_row=lmax - ROW_ALIGN)
    smem = pl.BlockSpec(memory_space=pltpu.SMEM)
    return pl.pallas_call(
        kern,
        in_specs=[smem],
        out_specs=[smem] * 6,
        out_shape=[jax.ShapeDtypeStruct((n_tiles,), jnp.int32),
                   jax.ShapeDtypeStruct((1,), jnp.int32),
                   jax.ShapeDtypeStruct((n_tiles * cpt,), jnp.int32),
                   jax.ShapeDtypeStruct((tiles * stride,), jnp.int32),
                   jax.ShapeDtypeStruct((n_tiles,), jnp.int32),
                   jax.ShapeDtypeStruct((n_tiles,), jnp.int32)],
        scratch_shapes=[pltpu.SMEM((tiles,), jnp.int32)],
        name="routing_tables",
    )(cnt_flat)


def _moe_kernel(tile_expert_ref, n_active_ref, src_ref, nxt_ref, par_ref, xs_hbm, wi_hbm, bi_ref, wo_hbm, bo_ref,
                y_ref, xbuf, wi_f32, wo_f32, wi_bf, wo_bf, sem, wsem):
    j = pl.program_id(0)
    tm = xbuf.shape[1]
    d = y_ref.shape[1]
    f = wo_bf.shape[0]
    cpt = tm // ROW_ALIGN
    slot = j % 2
    n_active = n_active_ref[0]
    expert = tile_expert_ref[j]

    def start_tile(tile, s):
        for c in range(cpt):
            row = pl.multiple_of(src_ref[tile * cpt + c], ROW_ALIGN)
            pltpu.make_async_copy(xs_hbm.at[pl.ds(row, ROW_ALIGN), :],
                                  xbuf.at[s, pl.ds(c * ROW_ALIGN, ROW_ALIGN), :], sem.at[s]).start()

    def wait_tile(s):
        pltpu.make_async_copy(xs_hbm.at[pl.ds(0, tm), :], xbuf.at[s], sem.at[s]).wait()

    def weight_copies(e, s):
        return (pltpu.make_async_copy(wi_hbm.at[e], wi_f32.at[s], wsem.at[0, s]),
                pltpu.make_async_copy(wo_hbm.at[e], wo_f32.at[s], wsem.at[1, s]))

    @pl.when(j == 0)
    def _():
        start_tile(0, 0)
        for cp in weight_copies(expert, par_ref[0]):
            cp.start()

    @pl.when(j + 1 < n_active)
    def _():
        start_tile(j + 1, 1 - slot)

    @pl.when(j < n_active)
    def _():
        @pl.when((j == 0) | (expert != tile_expert_ref[jnp.maximum(j - 1, 0)]))
        def _():
            ws = par_ref[j]
            for cp in weight_copies(expert, ws):
                cp.wait()

            @pl.when(nxt_ref[j] >= 0)
            def _():
                for cp in weight_copies(nxt_ref[j], 1 - ws):
                    cp.start()

            wi_bf[...] = wi_f32[ws].astype(BF16)
            wo_bf[...] = wo_f32[ws].astype(BF16)

        wait_tile(slot)
        rows = xbuf[slot]
        hh = _dot(rows[:, :d], wi_bf[...]) + bi_ref[0]
        gate = jnp.minimum(hh[:, :f], SWIGLU_LIMIT)
        lin = jnp.clip(hh[:, f:], -SWIGLU_LIMIT, SWIGLU_LIMIT)
        half_gate = 0.5 * gate
        act = (lin + 1.0) * (half_gate + half_gate * jnp.tanh((0.5 * SWIGLU_ALPHA) * gate))
        weight = jnp.sum(rows[:, d:].astype(F32), axis=-1, keepdims=True)
        y_ref[...] = ((_dot(act.astype(BF16), wo_bf[...]) + bo_ref[0]) * weight).astype(BF16)

    @pl.when(j >= n_active)
    def _():
        y_ref[...] = jnp.zeros_like(y_ref)


def _moe(xs, tables, w_in, b_in, w_out, b_out, *, tm, n_tiles):
    tile_expert, n_active, moe_src, _, nxt, par = tables
    e, d, f2 = w_in.shape
    f = w_out.shape[1]
    bias = lambda width: pl.BlockSpec((1, 1, width), lambda j, te, *_: (te[j], 0, 0))
    grid_spec = pltpu.PrefetchScalarGridSpec(
        num_scalar_prefetch=5,
        grid=(n_tiles,),
        in_specs=[pl.BlockSpec(memory_space=pl.ANY),
                  pl.BlockSpec(memory_space=pl.ANY),
                  bias(f2),
                  pl.BlockSpec(memory_space=pl.ANY),
                  bias(d)],
        out_specs=pl.BlockSpec((tm, d), lambda j, *_: (j, 0)),
        scratch_shapes=[pltpu.VMEM((2, tm, xs.shape[1]), BF16),
                        pltpu.VMEM((2, d, f2), F32),
                        pltpu.VMEM((2, f, d), F32),
                        pltpu.VMEM((d, f2), BF16),
                        pltpu.VMEM((f, d), BF16),
                        pltpu.SemaphoreType.DMA((2,)),
                        pltpu.SemaphoreType.DMA((2, 2))],
    )
    return pl.pallas_call(
        _moe_kernel,
        grid_spec=grid_spec,
        out_shape=jax.ShapeDtypeStruct((n_tiles * tm, d), BF16),
        compiler_params=_cparams(("arbitrary",)),
        name="moe_experts",
    )(tile_expert, n_active, moe_src, nxt, par, xs, w_in, b_in.reshape(e, 1, f2), w_out, b_out.reshape(e, 1, d))


def _combine_kernel(src_ref, x1_ref, r2_ref, mod_ref, gf_ref, y_hbm, o_ref, ybuf, sem,
                    *, per_row, tiles_per_seq, tile0, stride, single):
    i = pl.program_id(0)
    nt = pl.num_programs(0)
    tt, d = x1_ref.shape
    lmax = ybuf.shape[1]
    nch = lmax // ROW_ALIGN
    slot = i % 2
    b = i // tiles_per_seq

    def start_tile(tile, s):
        for c in range(nch):
            row = pl.multiple_of(src_ref[(tile0 + tile) * stride + c], ROW_ALIGN)
            pltpu.make_async_copy(y_hbm.at[pl.ds(row, ROW_ALIGN), :],
                                  ybuf.at[s, pl.ds(c * ROW_ALIGN, ROW_ALIGN), :], sem.at[s]).start()

    def wait_tile(s):
        pltpu.make_async_copy(y_hbm.at[pl.ds(0, lmax), :], ybuf.at[s], sem.at[s]).wait()

    @pl.when(i == 0)
    def _():
        start_tile(0, 0)

    if not single:
        start_tile(jnp.minimum(i + 1, nt - 1), 1 - slot)
    wait_tile(slot)
    r2 = r2_ref[...].astype(jnp.int32).astype(jnp.int16)
    one = jnp.ones((tt, SORT_BLOCK), BF16)
    zero = jnp.zeros((tt, SORT_BLOCK), BF16)
    moe = jnp.zeros((tt, d), F32)
    for cb in range(lmax // SORT_BLOCK):
        col = (lax.broadcasted_iota(jnp.int32, (tt, SORT_BLOCK), 1) + cb * SORT_BLOCK).astype(jnp.int16)
        hit = col == r2[:, POS_LANE:POS_LANE + 1]
        for kk in range(1, TOP_K):
            hit = hit | (col == r2[:, POS_LANE + kk:POS_LANE + kk + 1])
        moe = moe + _dot(jnp.where(hit, one, zero), ybuf[slot, cb * SORT_BLOCK:(cb + 1) * SORT_BLOCK, :])
    x2 = x1_ref[...] + _mod_rows(mod_ref, 5, d, per_row, b) * moe
    o_ref[...] = _rms(x2) * gf_ref[...]

    if not single:
        @pl.when(i == nt - 1)
        def _():
            wait_tile(1 - slot)


def _combine(comb_src, x1, r2, mod, gf, ys, *, tt, lmax, per_row, tiles_per_seq, tile0, stride, r2_block0):
    n, d = x1.shape
    kern = functools.partial(_combine_kernel, per_row=per_row, tiles_per_seq=tiles_per_seq, tile0=tile0,
                             stride=stride, single=(n // tt == 1))
    mod_spec = (pl.BlockSpec((tt, mod.shape[1]), lambda i, s: (i, 0)) if per_row
                else pl.BlockSpec(mod.shape, lambda i, s: (0, 0)))
    grid_spec = pltpu.PrefetchScalarGridSpec(
        num_scalar_prefetch=1,
        grid=(n // tt,),
        in_specs=[pl.BlockSpec((tt, d), lambda i, s: (i, 0)),
                  pl.BlockSpec((tt, LANES), lambda i, s: (r2_block0 + i, 0)),
                  mod_spec,
                  pl.BlockSpec((1, d), lambda i, s: (0, 0)),
                  pl.BlockSpec(memory_space=pl.ANY)],
        out_specs=pl.BlockSpec((tt, d), lambda i, s: (i, 0)),
        scratch_shapes=[pltpu.VMEM((2, lmax, d), BF16), pltpu.SemaphoreType.DMA((2,))],
    )
    return pl.pallas_call(
        kern,
        grid_spec=grid_spec,
        out_shape=jax.ShapeDtypeStruct((n, d), F32),
        compiler_params=_cparams(("arbitrary",)),
        name="combine_final_per_row" if per_row else "combine_final",
    )(comb_src, x1, r2, mod, gf.reshape(1, d), ys)


def kernel(x_prompt, x_sample, c_prompt, c_sample, state_pool, state_ret, w_ada, b_ada, norm1_g, w_in,
           w_pool, pool_scale, w_up_pool, w_up_ret, w_out, norm2_g, w_router, b_router, w_expert_in,
           b_expert_in, w_expert_out, b_expert_out, final_norm_g):
    batch, seq, d = x_prompt.shape
    n_dec = x_sample.shape[0]
    n_prompt = batch * seq
    n_experts = w_router.shape[-1]
    xp = x_prompt.reshape(n_prompt, d)
    xs = x_sample.reshape(n_dec, d)

    w_in_bf = w_in[0].astype(BF16)
    wa_bf = w_up_pool[0].astype(BF16)
    wb_bf = w_up_ret[0].astype(BF16)
    wo_bf = w_out[0].astype(BF16)
    assert n_experts % 8 == 0
    wr_pad = w_router[0].T
    br_pad = jnp.broadcast_to(b_router[0][:, None], (n_experts, LANES))

    c_all = jnp.concatenate([c_prompt, c_sample], axis=0)
    c_all = jnp.pad(c_all, ((0, -c_all.shape[0] % 16), (0, 0)))
    mod = _modulation(c_all, w_ada[0], b_ada[0])
    mod_p, mod_s = mod[:batch], mod[batch:batch + n_dec]

    tm = min(512, seq)
    u_p, qkvg_p = _in_proj(xp, mod_p, norm1_g[0], w_in_bf, tm=tm, per_row=False, tiles_per_seq=seq // tm,
                           split=True)
    proj_s = _in_proj(xs, mod_s, norm1_g[0], w_in_bf, tm=n_dec, per_row=True, tiles_per_seq=1, split=False)

    ya_p, tail_p = _pool_prompt(u_p, w_pool[0], pool_scale[0], batch=batch, seq=seq, tt=min(1024, seq))
    ya_s, pool_s_t = _pool_decode(jnp.transpose(state_pool[0], (1, 0, 2)), proj_s, w_pool[0], pool_scale[0])
    yb_p, ret_p = _ret_prompt(qkvg_p, batch=batch, seq=seq, d_model=d, chunk=min(RET_CHUNK, seq))
    yb_s, ret_s = _ret_decode(proj_s, state_ret[0], d_model=d, bs=8)

    tmix = min(512, seq)
    x1_p, h2_p, route_p = _mix(xp, ya_p, yb_p, qkvg_p, mod_p, norm2_g[0], wa_bf, wb_bf, wo_bf, wr_pad, br_pad,
                               tm=tmix, per_row=False, tiles_per_seq=seq // tmix, n_experts=n_experts,
                               ga_blk=6)
    x1_s, h2_s, route_s = _mix(xs, ya_s, yb_s, proj_s, mod_s, norm2_g[0], wa_bf, wb_bf, wo_bf, wr_pad, br_pad,
                               tm=n_dec, per_row=True, tiles_per_seq=1, n_experts=n_experts, ga_blk=7)

    tt = min(TOKEN_TILE, seq)
    assert n_dec <= tt
    p_tiles = n_prompt // tt
    lmax = _sorted_rows(tt, n_experts)
    xs_buf, r2, cnt_t = _dispatch(h2_p, route_p, h2_s, route_s, tt=tt, lmax=lmax, n_experts=n_experts)

    tmo = MOE_ROW_TILE
    cnt = cnt_t[:, 0].astype(jnp.int32)
    max_rows = (n_prompt + n_dec) * TOP_K + (p_tiles + 1) * n_experts * (ROW_ALIGN - 1)
    n_tiles = max_rows // tmo + n_experts
    stride = lmax // ROW_ALIGN
    tables = _routing_tables(cnt, tiles=p_tiles + 1, n_e=n_experts, tmo=tmo, n_tiles=n_tiles, stride=stride,
                             lmax=lmax)
    comb_src = tables[3]

    ys = _moe(xs_buf, tables, w_expert_in[0], b_expert_in[0], w_expert_out[0], b_expert_out[0], tm=tmo,
              n_tiles=n_tiles)

    assert n_prompt % n_dec == 0
    y_p = _combine(comb_src, x1_p, r2, mod_p, final_norm_g, ys, tt=tt, lmax=lmax, per_row=False,
                   tiles_per_seq=seq // tt, tile0=0, stride=stride, r2_block0=0)
    y_s = _combine(comb_src, x1_s, r2, mod_s, final_norm_g, ys, tt=n_dec, lmax=min(lmax, _sorted_rows(n_dec, n_experts)),
                   per_row=True, tiles_per_seq=1, tile0=p_tiles, stride=stride, r2_block0=n_prompt // n_dec)

    return (y_p.reshape(batch, seq, d),
            y_s.reshape(n_dec, 1, d),
            tail_p[None, :, 1:, :],
            ret_p[None],
            jnp.transpose(pool_s_t, (1, 0, 2))[None],
            ret_s[None])
```

```python
import functools

import numpy as np
import jax
import jax.numpy as jnp
from jax import lax
from jax.experimental import pallas as pl
from jax.experimental.pallas import tpu as pltpu

F32 = jnp.float32
BF16 = jnp.bfloat16

POOL_WINDOWS = (2, 4, 8, 16)
POOL_BUF = max(POOL_WINDOWS) - 1
RET_HEADS = 4
RET_CHUNK = 256
ROPE_BASE = 10000.0
PAST_LEN = 16384
TOP_K = 4
SWIGLU_LIMIT = 7.0
SWIGLU_ALPHA = 1.702
N_MOD = 6
EPS = 1e-6
LANES = 128
ROW_ALIGN = 16
SORT_BLOCK = 512
POS_SPLIT = 64.0
POS_LANE = 16
TOKEN_TILE = 512
MOE_ROW_TILE = 512
VMEM_LIMIT = 56 * 1024 * 1024


def _cparams(sem):
    return pltpu.CompilerParams(dimension_semantics=sem, vmem_limit_bytes=VMEM_LIMIT)


def _dot(a, b):
    return jnp.dot(a, b, preferred_element_type=F32)


def _dot3(a, b):
    a_hi = a.astype(BF16)
    a_lo = (a - a_hi.astype(F32)).astype(BF16)
    b_hi = b.astype(BF16)
    b_lo = (b - b_hi.astype(F32)).astype(BF16)
    return _dot(a_hi, b_hi) + _dot(a_hi, b_lo) + _dot(a_lo, b_hi)


def _dot3_nt(a, b):
    nt = lambda x, y: lax.dot_general(x, y, (((1,), (1,)), ((), ())), preferred_element_type=F32)
    a_hi = a.astype(BF16)
    a_lo = (a - a_hi.astype(F32)).astype(BF16)
    b_hi = b.astype(BF16)
    b_lo = (b - b_hi.astype(F32)).astype(BF16)
    return nt(a_hi, b_hi) + nt(a_hi, b_lo) + nt(a_lo, b_hi)


def _sigmoid(x):
    return 0.5 + 0.5 * jnp.tanh(0.5 * x)


def _silu(x):
    half = 0.5 * x
    return half + half * jnp.tanh(half)


def _rms(x):
    return x * lax.rsqrt(jnp.mean(x * x, axis=-1, keepdims=True) + EPS)


def _round_up(x, m):
    return (x + m - 1) // m * m


def _mod_kernel(c_ref, w_ref, b_ref, o_ref):
    c = c_ref[...]
    o_ref[...] = _dot3(_silu(c_ref[...]), w_ref[...]) + b_ref[...]


def _modulation(c, w_ada, b_ada):
    rows, d = c.shape
    n = w_ada.shape[1]
    tn = 2 * d
    return pl.pallas_call(
        _mod_kernel,
        grid=(n // tn,),
        in_specs=[pl.BlockSpec((rows, d), lambda j: (0, 0)),
                  pl.BlockSpec((d, tn), lambda j: (0, j)),
                  pl.BlockSpec((1, tn), lambda j: (0, j))],
        out_specs=pl.BlockSpec((rows, tn), lambda j: (0, j)),
        out_shape=jax.ShapeDtypeStruct((rows, n), F32),
        compiler_params=_cparams(("arbitrary",)),
        name="adaln_modulation",
    )(c, w_ada, b_ada.reshape(1, n))


def _mod_rows(mod_ref, which, d, per_row, b):
    if per_row:
        return mod_ref[:, which * d:(which + 1) * d]
    return mod_ref[pl.ds(b, 1), which * d:(which + 1) * d]


def _in_proj_kernel(x_ref, mod_ref, g_ref, w_ref, *outs, per_row, tiles_per_seq, split):
    i = pl.program_id(0)
    d = x_ref.shape[1]
    b = i // tiles_per_seq
    sh = _mod_rows(mod_ref, 0, d, per_row, b)
    sc = _mod_rows(mod_ref, 1, d, per_row, b)
    h = (_rms(x_ref[...]) * g_ref[...] * (1.0 + sc) + sh).astype(BF16)
    for c in range(w_ref.shape[1] // d):
        acc = _dot(h, w_ref[:, c * d:(c + 1) * d])
        if not split:
            outs[0][:, c * d:(c + 1) * d] = acc
        elif c == 0:
            outs[0][...] = acc
        else:
            outs[1][:, (c - 1) * d:c * d] = acc.astype(BF16)


def _in_proj(x, mod, g, w_bf, *, tm, per_row, tiles_per_seq, split):
    n, d = x.shape
    width = w_bf.shape[1]
    kern = functools.partial(_in_proj_kernel, per_row=per_row, tiles_per_seq=tiles_per_seq, split=split)
    mod_spec = (pl.BlockSpec((tm, mod.shape[1]), lambda i: (i, 0)) if per_row
                else pl.BlockSpec(mod.shape, lambda i: (0, 0)))
    if split:
        out_specs = [pl.BlockSpec((tm, d), lambda i: (i, 0)),
                     pl.BlockSpec((tm, width - d), lambda i: (i, 0))]
        out_shape = [jax.ShapeDtypeStruct((n, d), F32), jax.ShapeDtypeStruct((n, width - d), BF16)]
    else:
        out_specs = pl.BlockSpec((tm, width), lambda i: (i, 0))
        out_shape = jax.ShapeDtypeStruct((n, width), F32)
    return pl.pallas_call(
        kern,
        grid=(n // tm,),
        in_specs=[pl.BlockSpec((tm, d), lambda i: (i, 0)),
                  mod_spec,
                  pl.BlockSpec((1, d), lambda i: (0, 0)),
                  pl.BlockSpec((d, width), lambda i: (0, 0), pipeline_mode=pl.Buffered(1))],
        out_specs=out_specs,
        out_shape=out_shape,
        compiler_params=_cparams(("arbitrary",)),
        name="in_proj_per_row" if per_row else "in_proj",
    )(x, mod, g.reshape(1, d), w_bf)


def _pool_prompt_kernel(u_ref, w_ref, s_ref, y_ref, tail_ref, prev_ref, *, tt):
    t = pl.program_id(1)
    nt = pl.num_programs(1)
    hist = prev_ref.shape[0]
    gw = w_ref.shape[1]

    @pl.when(t == 0)
    def _():
        prev_ref[...] = jnp.zeros_like(prev_ref)

    u = u_ref[...]
    ext = jnp.concatenate([prev_ref[...], u], axis=0)
    pos = t * tt + lax.broadcasted_iota(jnp.int32, (tt, 1), 0)
    outs = []
    for gi, w in enumerate(POOL_WINDOWS):
        cols = slice(gi * gw, (gi + 1) * gw)
        run = ext[:, cols]
        span = 1
        while span < w:
            run = run + pltpu.roll(run, span, 0)
            span *= 2
        inv_cnt = 1.0 / jnp.minimum(w, pos + 1).astype(F32)
        pooled = run[hist:, :] * inv_cnt - u[:, cols]
        outs.append(_dot(pooled.astype(BF16), w_ref[gi].astype(BF16)))
    y = jnp.concatenate(outs, axis=-1) * s_ref[...]
    y_ref[...] = y.astype(BF16)
    prev_ref[...] = u[tt - hist:, :]

    @pl.when(t == nt - 1)
    def _():
        tail_ref[0] = u[tt - hist:, :]


def _pool_prompt(proj, w_pool, pool_scale, *, batch, seq, tt):
    p = pool_scale.shape[0]
    hist = POOL_BUF + 1
    nt = seq // tt
    kern = functools.partial(_pool_prompt_kernel, tt=tt)
    return pl.pallas_call(
        kern,
        grid=(batch, nt),
        in_specs=[pl.BlockSpec((tt, p), lambda b, t: (b * nt + t, 0)),
                  pl.BlockSpec(w_pool.shape, lambda b, t: (0, 0, 0)),
                  pl.BlockSpec((1, p), lambda b, t: (0, 0))],
        out_specs=[pl.BlockSpec((tt, p), lambda b, t: (b * nt + t, 0)),
                   pl.BlockSpec((1, hist, p), lambda b, t: (b, 0, 0))],
        out_shape=[jax.ShapeDtypeStruct((batch * seq, p), BF16),
                   jax.ShapeDtypeStruct((batch, hist, p), F32)],
        scratch_shapes=[pltpu.VMEM((hist, p), F32)],
        compiler_params=_cparams(("arbitrary", "arbitrary")),
        name="pool_prompt",
    )(proj, w_pool, pool_scale.reshape(1, p))


def _pool_decode_kernel(st_ref, u_ref, w_ref, s_ref, y_ref, new_ref):
    buf = st_ref.shape[0]
    gw = w_ref.shape[1]
    u = u_ref[...]
    outs = []
    run = u
    used = 0
    for gi, w in enumerate(POOL_WINDOWS):
        while used < w - 1:
            run = run + st_ref[buf - 1 - used]
            used += 1
        cols = slice(gi * gw, (gi + 1) * gw)
        pooled = run[:, cols] / float(min(w, PAST_LEN + 1)) - u[:, cols]
        outs.append(_dot(pooled.astype(BF16), w_ref[gi].astype(BF16)))
    y_ref[...] = (jnp.concatenate(outs, axis=-1) * s_ref[...]).astype(BF16)
    for r in range(buf - 1):
        new_ref[r] = st_ref[r + 1]
    new_ref[buf - 1] = u


def _pool_decode(state_t, proj_s, w_pool, pool_scale):
    buf, n, p = state_t.shape
    return pl.pallas_call(
        _pool_decode_kernel,
        grid=(1,),
        in_specs=[pl.BlockSpec((buf, n, p), lambda i: (0, 0, 0)),
                  pl.BlockSpec((n, p), lambda i: (0, 0)),
                  pl.BlockSpec(w_pool.shape, lambda i: (0, 0, 0)),
                  pl.BlockSpec((1, p), lambda i: (0, 0))],
        out_specs=[pl.BlockSpec((n, p), lambda i: (0, 0)),
                   pl.BlockSpec((buf, n, p), lambda i: (0, 0, 0))],
        out_shape=[jax.ShapeDtypeStruct((n, p), BF16),
                   jax.ShapeDtypeStruct((buf, n, p), F32)],
        compiler_params=_cparams(("arbitrary",)),
        name="pool_decode",
    )(state_t, proj_s, w_pool, pool_scale.reshape(1, p))


def _log_gamma(h):
    return float(np.log(1.0 - 2.0 ** (-5.0 - h)))


def _rope_tables(pos, half):
    inv = np.power(ROPE_BASE, -np.arange(half, dtype=np.float64) / half)
    ang = np.asarray(pos, np.float64)[:, None] * inv[None, :]
    return jnp.asarray(np.cos(ang), F32), jnp.asarray(np.sin(ang), F32)


def _rotary(x, cos, sin):
    half = x.shape[-1] // 2
    x1, x2 = x[:, :half], x[:, half:]
    return jnp.concatenate([x1 * cos - x2 * sin, x2 * cos + x1 * sin], axis=-1)


def _ret_tables(c, key_scale):
    i = np.arange(c, dtype=np.float64)
    diff = i[:, None] - i[None, :]
    dec, qs, ks = [], [], []
    for h in range(RET_HEADS):
        lg = _log_gamma(h)
        dec.append(key_scale * np.where(diff >= 0, np.exp(lg * np.maximum(diff, 0.0)), 0.0))
        qs.append(np.broadcast_to(np.exp(lg * (i + 1.0))[:, None], (c, LANES)))
        ks.append(key_scale * np.broadcast_to(np.exp(lg * (c - 1.0 - i))[:, None], (c, LANES)))
    return (jnp.asarray(np.stack(dec), F32), jnp.asarray(np.stack(qs), F32), jnp.asarray(np.stack(ks), F32))


def _ret_prompt_kernel(q_ref, k_ref, v_ref, g_ref, cos_ref, sin_ref, dec_ref, qs_ref, ks_ref,
                       y_ref, s_out_ref, s_ref, *, chunk_decay):
    c = pl.program_id(1)
    nc = pl.num_programs(1)
    heads, dk, dv = s_ref.shape

    @pl.when(c == 0)
    def _():
        s_ref[...] = jnp.zeros_like(s_ref)

    cos, sin = cos_ref[...], sin_ref[...]
    reps = dk // LANES
    for h in range(heads):
        q = _rotary(q_ref[:, h * dk:(h + 1) * dk].astype(F32), cos, sin)
        k = _rotary(k_ref[:, h * dk:(h + 1) * dk].astype(F32), cos, sin)
        v = v_ref[:, h * dv:(h + 1) * dv]
        qs = jnp.concatenate([qs_ref[h]] * reps, axis=-1)
        ks = jnp.concatenate([ks_ref[h]] * reps, axis=-1)
        scores = lax.dot_general(q.astype(BF16), k.astype(BF16), (((1,), (1,)), ((), ())),
                                 preferred_element_type=F32) * dec_ref[h]
        state = s_ref[h]
        o = _dot(scores.astype(BF16), v) + _dot((q * qs).astype(BF16), state.astype(BF16))
        kv = lax.dot_general((k * ks).astype(BF16), v, (((0,), (0,)), ((), ())), preferred_element_type=F32)
        new_state = chunk_decay[h] * state + kv
        s_ref[h] = new_state
        g = g_ref[:, h * dv:(h + 1) * dv].astype(F32)
        y_ref[:, h * dv:(h + 1) * dv] = (_silu(g) * _rms(o)).astype(BF16)

        @pl.when(c == nc - 1)
        def _():
            s_out_ref[0, h] = new_state


def _ret_prompt(qkvg, *, batch, seq, d_model, chunk):
    heads = RET_HEADS
    dk = d_model // heads
    dv = 2 * d_model // heads
    nc = seq // chunk
    cos, sin = _rope_tables(np.arange(seq), dk // 2)
    key_scale = dk ** -0.5
    assert np.log2(key_scale) == np.round(np.log2(key_scale))
    dec, qs, ks = _ret_tables(chunk, key_scale)
    chunk_decay = tuple(float(np.exp(_log_gamma(h) * chunk)) for h in range(heads))
    kern = functools.partial(_ret_prompt_kernel, chunk_decay=chunk_decay)
    const = lambda a: pl.BlockSpec(a.shape, lambda b, c: (0,) * a.ndim)
    return pl.pallas_call(
        kern,
        grid=(batch, nc),
        in_specs=[pl.BlockSpec((chunk, d_model), lambda b, c: (b * nc + c, 0)),
                  pl.BlockSpec((chunk, d_model), lambda b, c: (b * nc + c, 1)),
                  pl.BlockSpec((chunk, 2 * d_model), lambda b, c: (b * nc + c, 1)),
                  pl.BlockSpec((chunk, 2 * d_model), lambda b, c: (b * nc + c, 2)),
                  pl.BlockSpec((chunk, dk // 2), lambda b, c: (c, 0)),
                  pl.BlockSpec((chunk, dk // 2), lambda b, c: (c, 0)),
                  const(dec), const(qs), const(ks)],
        out_specs=[pl.BlockSpec((chunk, heads * dv), lambda b, c: (b * nc + c, 0)),
                   pl.BlockSpec((1, heads, dk, dv), lambda b, c: (b, 0, 0, 0))],
        out_shape=[jax.ShapeDtypeStruct((batch * seq, heads * dv), BF16),
                   jax.ShapeDtypeStruct((batch, heads, dk, dv), F32)],
        scratch_shapes=[pltpu.VMEM((heads, dk, dv), F32)],
        compiler_params=_cparams(("arbitrary", "arbitrary")),
        name="retention_prompt",
    )(qkvg, qkvg, qkvg, qkvg, cos, sin, dec, qs, ks)


def _ret_decode_kernel(q_ref, k_ref, v_ref, g_ref, cos_ref, sin_ref, s_ref, y_ref, s_out_ref,
                       kt_ref, qd_ref, o_ref, *, gammas, bs):
    h = pl.program_id(0)
    blk = pl.program_id(1)
    nblk = pl.num_programs(1)
    n, dk = q_ref.shape
    gamma = jnp.where(h == 0, gammas[0], jnp.where(h == 1, gammas[1], jnp.where(h == 2, gammas[2], gammas[3])))

    @pl.when(blk == 0)
    def _():
        cos, sin = cos_ref[...], sin_ref[...]
        q = _rotary(q_ref[...], cos, sin)
        k = _rotary(k_ref[...], cos, sin) * (dk ** -0.5)
        kt_ref[...] = k.T
        qd_ref[...] = (q * gamma).astype(BF16)
        o_ref[...] = jnp.sum(q * k, axis=-1, keepdims=True) * v_ref[...]

    vb = v_ref[...].astype(BF16)
    rows = lax.broadcasted_iota(jnp.int32, (n, 1), 0)
    lanes = lax.broadcasted_iota(jnp.int32, (1, n), 1)
    for i in range(bs):
        tok = blk * bs + i
        state = s_ref[i, 0]
        o_ref[...] += jnp.where(rows == tok, _dot(qd_ref[...], state.astype(BF16)), 0.0)
        k_col = jnp.where(lanes == tok, kt_ref[...], 0.0).astype(BF16)
        s_out_ref[i, 0] = gamma * state + _dot(k_col, vb)

    @pl.when(blk == nblk - 1)
    def _():
        g = g_ref[...]
        y_ref[...] = (_silu(g) * _rms(o_ref[...])).astype(BF16)


def _ret_decode(proj_s, state, *, d_model, bs):
    n = proj_s.shape[0]
    heads = RET_HEADS
    dk = d_model // heads
    dv = 2 * d_model // heads
    cos, sin = _rope_tables(np.array([PAST_LEN]), dk // 2)
    gammas = tuple(float(np.exp(_log_gamma(h))) for h in range(heads))
    q0, k0, v0, g0 = d_model // dk, 2 * d_model // dk, 3 * d_model // dv, 5 * d_model // dv
    kern = functools.partial(_ret_decode_kernel, gammas=gammas, bs=bs)
    return pl.pallas_call(
        kern,
        grid=(heads, n // bs),
        in_specs=[pl.BlockSpec((n, dk), lambda h, b: (0, q0 + h)),
                  pl.BlockSpec((n, dk), lambda h, b: (0, k0 + h)),
                  pl.BlockSpec((n, dv), lambda h, b: (0, v0 + h)),
                  pl.BlockSpec((n, dv), lambda h, b: (0, g0 + h)),
                  pl.BlockSpec((1, dk // 2), lambda h, b: (0, 0)),
                  pl.BlockSpec((1, dk // 2), lambda h, b: (0, 0)),
                  pl.BlockSpec((bs, 1, dk, dv), lambda h, b: (b, h, 0, 0))],
        out_specs=[pl.BlockSpec((n, dv), lambda h, b: (0, h)),
                   pl.BlockSpec((bs, 1, dk, dv), lambda h, b: (b, h, 0, 0))],
        out_shape=[jax.ShapeDtypeStruct((n, heads * dv), BF16),
                   jax.ShapeDtypeStruct(state.shape, F32)],
        scratch_shapes=[pltpu.VMEM((dk, n), F32), pltpu.VMEM((n, dk), BF16), pltpu.VMEM((n, dv), F32)],
        compiler_params=_cparams(("arbitrary", "arbitrary")),
        name="retention_decode",
    )(proj_s, proj_s, proj_s, proj_s, cos, sin, state)


def _mix_kernel(x_ref, ya_ref, yb_ref, ga_ref, gb_ref, mod_ref, g2_ref, wa_ref, wb_ref, wo_ref,
                wr_ref, br_ref, x1_ref, h2_ref, route_ref, *, per_row, tiles_per_seq, n_experts):
    i = pl.program_id(0)
    tm, d = x_ref.shape
    b = i // tiles_per_seq

    merged = (_sigmoid(ga_ref[...].astype(F32)) * _dot(ya_ref[...], wa_ref[...])
              + _sigmoid(gb_ref[...].astype(F32)) * _dot(yb_ref[...], wb_ref[...]))
    mix = _dot(merged.astype(BF16), wo_ref[...])
    x1 = x_ref[...] + _mod_rows(mod_ref, 2, d, per_row, b) * mix
    x1_ref[...] = x1
    h2 = (_rms(x1) * g2_ref[...] * (1.0 + _mod_rows(mod_ref, 4, d, per_row, b))
          + _mod_rows(mod_ref, 3, d, per_row, b))
    h2_ref[...] = h2.astype(BF16)

    logits_t = _dot3_nt(wr_ref[...], h2) + br_ref[:, 0:1]
    row = lax.broadcasted_iota(jnp.int32, (n_experts, tm), 0)
    neg = jnp.float32(-jnp.inf)
    work = logits_t
    vals, idxs = [], []
    for _ in range(TOP_K):
        m = jnp.max(work, axis=0, keepdims=True)
        idx = jnp.min(jnp.where(work == m, row, n_experts), axis=0, keepdims=True)
        vals.append(m)
        idxs.append(idx)
        work = jnp.where(row == idx, neg, work)
    exps = [jnp.exp(v - vals[0]) for v in vals]
    denom = exps[0] + exps[1] + exps[2] + exps[3]
    route_ref[...] = jnp.concatenate([idx.astype(F32) for idx in idxs] + [e / denom for e in exps]
                                     + [jnp.zeros((LANES - 2 * TOP_K, tm), F32)], axis=0)


def _mix(x, ya, yb, proj, mod, g2, wa_bf, wb_bf, wo_bf, wr_pad, br_pad, *, tm, per_row, tiles_per_seq,
         n_experts, ga_blk):
    n, d = x.shape
    gb_blk = ga_blk + 1
    kern = functools.partial(_mix_kernel, per_row=per_row, tiles_per_seq=tiles_per_seq, n_experts=n_experts)
    mod_spec = (pl.BlockSpec((tm, mod.shape[1]), lambda i: (i, 0)) if per_row
                else pl.BlockSpec(mod.shape, lambda i: (0, 0)))
    full = lambda a: pl.BlockSpec(a.shape, lambda i: (0,) * a.ndim)
    return pl.pallas_call(
        kern,
        grid=(n // tm,),
        in_specs=[pl.BlockSpec((tm, d), lambda i: (i, 0)),
                  pl.BlockSpec((tm, ya.shape[1]), lambda i: (i, 0)),
                  pl.BlockSpec((tm, yb.shape[1]), lambda i: (i, 0)),
                  pl.BlockSpec((tm, d), lambda i: (i, ga_blk)),
                  pl.BlockSpec((tm, d), lambda i: (i, gb_blk)),
                  mod_spec,
                  pl.BlockSpec((1, d), lambda i: (0, 0)),
                  full(wa_bf), full(wb_bf), full(wo_bf), full(wr_pad), full(br_pad)],
        out_specs=[pl.BlockSpec((tm, d), lambda i: (i, 0)),
                   pl.BlockSpec((tm, d), lambda i: (i, 0)),
                   pl.BlockSpec((LANES, tm), lambda i: (0, i))],
        out_shape=[jax.ShapeDtypeStruct((n, d), F32),
                   jax.ShapeDtypeStruct((n, d), BF16),
                   jax.ShapeDtypeStruct((LANES, n), F32)],
        compiler_params=_cparams(("arbitrary",)),
        name="mix_route_per_row" if per_row else "mix_route",
    )(x, ya, yb, proj, proj, mod, g2.reshape(1, d), wa_bf, wb_bf, wo_bf, wr_pad, br_pad)


def _sorted_rows(tt, n_experts):
    return _round_up(tt * TOP_K + n_experts * (ROW_ALIGN - 1) + ROW_ALIGN, SORT_BLOCK)


def _dispatch_kernel(hp_ref, rp_ref, hd_ref, rd_ref, before_ref, below_ref, xs_ref, r2_ref, cnt_ref, *, lmax):
    tt = hp_ref.shape[0]
    n_e = cnt_ref.shape[0]
    is_dec = pl.program_id(0) == pl.num_programs(0) - 1
    route_t = jnp.where(is_dec, rd_ref[...], rp_ref[...])
    expert = lax.broadcasted_iota(jnp.int32, (n_e, tt), 0).astype(F32)
    hits = [expert == route_t[kk:kk + 1, :] for kk in range(TOP_K)]
    sel = jnp.zeros((n_e, tt), F32)
    for hit in hits:
        sel = sel + jnp.where(hit, 1.0, 0.0)
    cnt = jnp.sum(sel, axis=1, keepdims=True)
    seg = jnp.broadcast_to(jnp.ceil(cnt / ROW_ALIGN) * ROW_ALIGN, (n_e, LANES))
    off = _dot3(below_ref[...], seg)[:, 0:1]
    place = _dot(sel.astype(BF16), before_ref[...]) + off
    w_hi, w_lo, pos_a, pos_b, pos_all = [], [], [], [], []
    for kk in range(TOP_K):
        found = jnp.sum(jnp.where(hits[kk], 1.0, 0.0), axis=0, keepdims=True)
        pos = jnp.where(found > 0.0, jnp.sum(jnp.where(hits[kk], place, 0.0), axis=0, keepdims=True), -1.0)
        w = route_t[TOP_K + kk:TOP_K + kk + 1, :]
        hi = w.astype(BF16).astype(F32)
        a = jnp.floor(pos / POS_SPLIT)
        w_hi.append(hi)
        w_lo.append(w - hi)
        pos_a.append(a)
        pos_b.append(pos - POS_SPLIT * a)
        pos_all.append(pos)
    info_t = jnp.concatenate(w_hi + w_lo + pos_a + pos_b + pos_all
                             + [jnp.zeros((LANES - 5 * TOP_K, tt), F32)], axis=0)
    side = info_t.T
    r2_ref[...] = side
    cnt_ref[...] = jnp.broadcast_to(cnt, cnt_ref.shape)
    r2t = info_t.astype(jnp.int32).astype(jnp.int16)
    h = jnp.concatenate([jnp.where(is_dec, hd_ref[...], hp_ref[...]), side.astype(BF16)], axis=-1)
    d = hp_ref.shape[1]
    out_lane = lax.broadcasted_iota(jnp.int32, (SORT_BLOCK, LANES), 1)
    one = jnp.ones((SORT_BLOCK, tt), BF16)
    zero = jnp.zeros((SORT_BLOCK, tt), BF16)
    for rb in range(lmax // SORT_BLOCK):
        rio = (lax.broadcasted_iota(jnp.int32, (SORT_BLOCK, tt), 0) + rb * SORT_BLOCK).astype(jnp.int16)
        hit = rio == r2t[POS_LANE:POS_LANE + 1, :]
        for kk in range(1, TOP_K):
            hit = hit | (rio == r2t[POS_LANE + kk:POS_LANE + kk + 1, :])
        perm = jnp.where(hit, one, zero)
        res = _dot(perm, h)
        info = res[:, d:]
        row = (lax.broadcasted_iota(jnp.int32, (SORT_BLOCK, LANES), 0) + rb * SORT_BLOCK).astype(F32)
        pos = (POS_SPLIT * pltpu.roll(info, LANES - 2 * TOP_K, 1) + pltpu.roll(info, LANES - 3 * TOP_K, 1))
        mine = jnp.where((pos == row) & (out_lane < TOP_K), 1.0, 0.0)
        mine = mine + pltpu.roll(mine, TOP_K, 1)
        rows = slice(rb * SORT_BLOCK, (rb + 1) * SORT_BLOCK)
        xs_ref[rows, :d] = res[:, :d].astype(BF16)
        xs_ref[rows, d:] = (info * mine).astype(BF16)


def _dispatch(h2_p, route_p, h2_d, route_d, *, tt, lmax, n_experts):
    n, d = h2_p.shape
    p_tiles = n // tt
    tiles = p_tiles + 1
    pad = tt - h2_d.shape[0]
    h2_d = jnp.pad(h2_d, ((0, pad), (0, 0)))
    route_d = jnp.pad(route_d, ((0, 0), (0, pad)), constant_values=-1.0)
    kern = functools.partial(_dispatch_kernel, lmax=lmax)
    before = jnp.asarray(np.triu(np.ones((tt, tt), np.float32), 1), BF16)
    below = jnp.asarray(np.tril(np.ones((n_experts, n_experts), np.float32), -1), F32)
    return pl.pallas_call(
        kern,
        grid=(tiles,),
        in_specs=[pl.BlockSpec((tt, d), lambda i: (jnp.minimum(i, p_tiles - 1), 0)),
                  pl.BlockSpec((LANES, tt), lambda i: (0, jnp.minimum(i, p_tiles - 1))),
                  pl.BlockSpec((tt, d), lambda i: (0, 0)),
                  pl.BlockSpec((LANES, tt), lambda i: (0, 0)),
                  pl.BlockSpec((tt, tt), lambda i: (0, 0)),
                  pl.BlockSpec((n_experts, n_experts), lambda i: (0, 0))],
        out_specs=[pl.BlockSpec((lmax, d + LANES), lambda i: (i, 0)),
                   pl.BlockSpec((tt, LANES), lambda i: (i, 0)),
                   pl.BlockSpec((n_experts, LANES), lambda i: (i, 0))],
        out_shape=[jax.ShapeDtypeStruct((tiles * lmax, d + LANES), BF16),
                   jax.ShapeDtypeStruct((tiles * tt, LANES), F32),
                   jax.ShapeDtypeStruct((tiles * n_experts, LANES), F32)],
        compiler_params=_cparams(("arbitrary",)),
        name="dispatch_sort",
    )(h2_p, route_p, h2_d, route_d, before, below)


def _tables_kernel(cnt_ref, te_ref, na_ref, nxt_ref, par_ref, msrc_ref, csrc_ref,
                   before_ref, start_ref, len_ref, first_ref, end_ref, total_ref, *, tiles, n_e, cpt, cstride,
                   lmax, n_tiles, zero_row):
    def nblk(i, e):
        return len_ref[i * n_e + e]

    def per_tile(i, carry):
        def per_expert(e, run):
            blocks = lax.shift_right_logical(cnt_ref[i * n_e + e] + (ROW_ALIGN - 1), ROW_ALIGN.bit_length() - 1)
            len_ref[i * n_e + e] = blocks
            start_ref[i * n_e + e] = run
            return run + blocks
        lax.fori_loop(0, n_e, per_expert, 0)
        return carry

    lax.fori_loop(0, tiles, per_tile, 0)

    def per_expert(e, carry):
        def per_tile(i, run):
            before_ref[i * n_e + e] = run
            return run + nblk(i, e)
        total_ref[e] = lax.fori_loop(0, tiles, per_tile, 0)
        return carry

    lax.fori_loop(0, n_e, per_expert, 0)

    def expert_tiles(e, carry):
        t_first, run = carry
        t_last = t_first + lax.shift_right_logical(total_ref[e] + (cpt - 1), cpt.bit_length() - 1)
        first_ref[e] = t_first
        end_ref[e] = t_last

        def mark(t, carry):
            te_ref[t] = e
            par_ref[t] = run % 2
            return carry

        lax.fori_loop(t_first, t_last, mark, 0)
        return t_last, jnp.where(t_last > t_first, run + 1, run)

    n_act, _ = lax.fori_loop(0, n_e, expert_tiles, (jnp.int32(0), jnp.int32(0)))
    na_ref[0] = n_act
    last = te_ref[jnp.maximum(n_act - 1, 0)]

    def mark_rest(t, carry):
        te_ref[t] = last
        par_ref[t] = 0
        nxt_ref[t] = -1
        return carry

    lax.fori_loop(n_act, n_tiles, mark_rest, 0)

    def mark_next(s, nxt):
        t = n_act - 1 - s
        after = te_ref[jnp.minimum(t + 1, n_tiles - 1)]
        nxt = jnp.where((t + 1 < n_act) & (after != te_ref[t]), after, nxt)
        nxt_ref[t] = nxt
        return nxt

    lax.fori_loop(0, n_act, mark_next, jnp.int32(-1))

    shape = msrc_ref.shape
    n_idx = (lax.broadcasted_iota(jnp.int32, shape, 0) * LANES + lax.broadcasted_iota(jnp.int32, shape, 1))
    m_tile = lax.shift_right_logical(n_idx, cpt.bit_length() - 1)
    expert = jnp.zeros(shape, jnp.int32)
    for e in range(n_e):
        expert = expert + (end_ref[e] <= m_tile).astype(jnp.int32)
    t_first = jnp.zeros(shape, jnp.int32)
    total = jnp.zeros(shape, jnp.int32)
    for e in range(n_e):
        t_first = jnp.where(expert == e, first_ref[e], t_first)
        total = jnp.where(expert == e, total_ref[e], total)
    g = (m_tile - t_first) * cpt + (n_idx & (cpt - 1))
    src = jnp.full(shape, zero_row, jnp.int32)
    for i in range(tiles):
        seg_first = jnp.zeros(shape, jnp.int32)
        seg_len = jnp.zeros(shape, jnp.int32)
        seg_local = jnp.zeros(shape, jnp.int32)
        for e in range(n_e):
            mine = expert == e
            seg_first = jnp.where(mine, before_ref[i * n_e + e], seg_first)
            seg_len = jnp.where(mine, nblk(i, e), seg_len)
            seg_local = jnp.where(mine, start_ref[i * n_e + e], seg_local)
        inside = (g >= seg_first) & (g < seg_first + seg_len) & (g < total) & (m_tile < n_act)
        src = jnp.where(inside, i * lmax + (seg_local + g - seg_first) * ROW_ALIGN, src)
    msrc_ref[...] = src

    rows_per_tile = cstride // LANES
    c_idx = (lax.broadcasted_iota(jnp.int32, (rows_per_tile, LANES), 0) * LANES
             + lax.broadcasted_iota(jnp.int32, (rows_per_tile, LANES), 1))
    for i in range(tiles):
        owner = jnp.zeros((rows_per_tile, LANES), jnp.int32)
        for e in range(n_e):
            owner = owner + ((start_ref[i * n_e + e] + nblk(i, e)) <= c_idx).astype(jnp.int32)
        base = jnp.zeros((rows_per_tile, LANES), jnp.int32)
        for e in range(n_e):
            base = jnp.where(owner == e, first_ref[e] * cpt + before_ref[i * n_e + e] - start_ref[i * n_e + e],
                             base)
        csrc_ref[i * rows_per_tile:(i + 1) * rows_per_tile, :] = jnp.where(owner < n_e,
                                                                         (base + c_idx) * ROW_ALIGN, 0)


def _routing_tables(cnt_flat, *, tiles, n_e, tmo, n_tiles, cstride, lmax):
    cpt = tmo // ROW_ALIGN
    assert (n_tiles * cpt) % LANES == 0 and cstride % LANES == 0
    kern = functools.partial(_tables_kernel, tiles=tiles, n_e=n_e, cpt=cpt, cstride=cstride, lmax=lmax,
                             n_tiles=n_tiles, zero_row=lmax - ROW_ALIGN)
    smem = pl.BlockSpec(memory_space=pltpu.SMEM)
    vmem = pl.BlockSpec(memory_space=pltpu.VMEM)
    te, na, nxt, par, msrc, csrc = pl.pallas_call(
        kern,
        in_specs=[smem],
        out_specs=[smem, smem, smem, smem, vmem, vmem],
        out_shape=[jax.ShapeDtypeStruct((n_tiles,), jnp.int32),
                   jax.ShapeDtypeStruct((1,), jnp.int32),
                   jax.ShapeDtypeStruct((n_tiles,), jnp.int32),
                   jax.ShapeDtypeStruct((n_tiles,), jnp.int32),
                   jax.ShapeDtypeStruct((n_tiles * cpt // LANES, LANES), jnp.int32),
                   jax.ShapeDtypeStruct((tiles * cstride // LANES, LANES), jnp.int32)],
        scratch_shapes=[pltpu.SMEM((tiles * n_e,), jnp.int32), pltpu.SMEM((tiles * n_e,), jnp.int32),
                        pltpu.SMEM((tiles * n_e,), jnp.int32),
                        pltpu.SMEM((n_e,), jnp.int32), pltpu.SMEM((n_e,), jnp.int32),
                        pltpu.SMEM((n_e,), jnp.int32)],
        name="routing_tables",
    )(cnt_flat)
    return te, na, msrc.reshape(-1), csrc.reshape(-1), nxt, par


def _moe_kernel(tile_expert_ref, n_active_ref, src_ref, nxt_ref, par_ref, xs_hbm, wi_hbm, bi_ref, wo_hbm, bo_ref,
                y_ref, xbuf, wi_f32, wo_f32, wi_bf, wo_bf, sem, wsem):
    j = pl.program_id(0)
    tm = xbuf.shape[1]
    d = y_ref.shape[1]
    f = wo_bf.shape[0]
    cpt = tm // ROW_ALIGN
    slot = j % 2
    n_active = n_active_ref[0]
    expert = tile_expert_ref[j]

    def start_tile(tile, s):
        for c in range(cpt):
            row = pl.multiple_of(src_ref[tile * cpt + c], ROW_ALIGN)
            pltpu.make_async_copy(xs_hbm.at[pl.ds(row, ROW_ALIGN), :],
                                  xbuf.at[s, pl.ds(c * ROW_ALIGN, ROW_ALIGN), :], sem.at[s]).start()

    def wait_tile(s):
        pltpu.make_async_copy(xs_hbm.at[pl.ds(0, tm), :], xbuf.at[s], sem.at[s]).wait()

    def weight_copies(e, s):
        return (pltpu.make_async_copy(wi_hbm.at[e], wi_f32.at[s], wsem.at[0, s]),
                pltpu.make_async_copy(wo_hbm.at[e], wo_f32.at[s], wsem.at[1, s]))

    @pl.when(j == 0)
    def _():
        start_tile(0, 0)
        for cp in weight_copies(expert, par_ref[0]):
            cp.start()

    @pl.when(j + 1 < n_active)
    def _():
        start_tile(j + 1, 1 - slot)

    @pl.when(j < n_active)
    def _():
        @pl.when((j == 0) | (expert != tile_expert_ref[jnp.maximum(j - 1, 0)]))
        def _():
            ws = par_ref[j]
            for cp in weight_copies(expert, ws):
                cp.wait()

            @pl.when(nxt_ref[j] >= 0)
            def _():
                for cp in weight_copies(nxt_ref[j], 1 - ws):
                    cp.start()

            wi_bf[...] = wi_f32[ws].astype(BF16)
            wo_bf[...] = wo_f32[ws].astype(BF16)

        wait_tile(slot)
        rows = xbuf[slot]
        hh = _dot(rows[:, :d], wi_bf[...]) + bi_ref[0]
        gate = jnp.minimum(hh[:, :f], SWIGLU_LIMIT)
        lin = jnp.clip(hh[:, f:], -SWIGLU_LIMIT, SWIGLU_LIMIT)
        half_gate = 0.5 * gate
        act = (lin + 1.0) * (half_gate + half_gate * jnp.tanh((0.5 * SWIGLU_ALPHA) * gate))
        weight = jnp.sum(rows[:, d:].astype(F32), axis=-1, keepdims=True)
        y_ref[...] = ((_dot(act.astype(BF16), wo_bf[...]) + bo_ref[0]) * weight).astype(BF16)

    @pl.when(j >= n_active)
    def _():
        y_ref[...] = jnp.zeros_like(y_ref)


def _moe(xs, tables, w_in, b_in, w_out, b_out, *, tm, n_tiles):
    tile_expert, n_active, moe_src, _, nxt, par = tables
    e, d, f2 = w_in.shape
    f = w_out.shape[1]
    bias = lambda width: pl.BlockSpec((1, 1, width), lambda j, te, *_: (te[j], 0, 0))
    grid_spec = pltpu.PrefetchScalarGridSpec(
        num_scalar_prefetch=5,
        grid=(n_tiles,),
        in_specs=[pl.BlockSpec(memory_space=pl.ANY),
                  pl.BlockSpec(memory_space=pl.ANY),
                  bias(f2),
                  pl.BlockSpec(memory_space=pl.ANY),
                  bias(d)],
        out_specs=pl.BlockSpec((tm, d), lambda j, *_: (j, 0)),
        scratch_shapes=[pltpu.VMEM((2, tm, xs.shape[1]), BF16),
                        pltpu.VMEM((2, d, f2), F32),
                        pltpu.VMEM((2, f, d), F32),
                        pltpu.VMEM((d, f2), BF16),
                        pltpu.VMEM((f, d), BF16),
                        pltpu.SemaphoreType.DMA((2,)),
                        pltpu.SemaphoreType.DMA((2, 2))],
    )
    return pl.pallas_call(
        _moe_kernel,
        grid_spec=grid_spec,
        out_shape=jax.ShapeDtypeStruct((n_tiles * tm, d), BF16),
        compiler_params=_cparams(("arbitrary",)),
        name="moe_experts",
    )(tile_expert, n_active, moe_src, nxt, par, xs, w_in, b_in.reshape(e, 1, f2), w_out, b_out.reshape(e, 1, d))


def _combine_kernel(src_ref, x1_ref, r2_ref, mod_ref, gf_ref, y_hbm, o_ref, ybuf, sem,
                    *, per_row, tiles_per_seq, tile0, stride, single):
    i = pl.program_id(0)
    nt = pl.num_programs(0)
    tt, d = x1_ref.shape
    lmax = ybuf.shape[1]
    nch = lmax // ROW_ALIGN
    slot = i % 2
    b = i // tiles_per_seq

    def start_tile(tile, s):
        for c in range(nch):
            row = pl.multiple_of(src_ref[(tile0 + tile) * stride + c], ROW_ALIGN)
            pltpu.make_async_copy(y_hbm.at[pl.ds(row, ROW_ALIGN), :],
                                  ybuf.at[s, pl.ds(c * ROW_ALIGN, ROW_ALIGN), :], sem.at[s]).start()

    def wait_tile(s):
        pltpu.make_async_copy(y_hbm.at[pl.ds(0, lmax), :], ybuf.at[s], sem.at[s]).wait()

    @pl.when(i == 0)
    def _():
        start_tile(0, 0)

    if not single:
        start_tile(jnp.minimum(i + 1, nt - 1), 1 - slot)
    wait_tile(slot)
    r2 = r2_ref[...].astype(jnp.int32).astype(jnp.int16)
    one = jnp.ones((tt, SORT_BLOCK), BF16)
    zero = jnp.zeros((tt, SORT_BLOCK), BF16)
    moe = jnp.zeros((tt, d), F32)
    for cb in range(lmax // SORT_BLOCK):
        col = (lax.broadcasted_iota(jnp.int32, (tt, SORT_BLOCK), 1) + cb * SORT_BLOCK).astype(jnp.int16)
        hit = col == r2[:, POS_LANE:POS_LANE + 1]
        for kk in range(1, TOP_K):
            hit = hit | (col == r2[:, POS_LANE + kk:POS_LANE + kk + 1])
        moe = moe + _dot(jnp.where(hit, one, zero), ybuf[slot, cb * SORT_BLOCK:(cb + 1) * SORT_BLOCK, :])
    x2 = x1_ref[...] + _mod_rows(mod_ref, 5, d, per_row, b) * moe
    o_ref[...] = _rms(x2) * gf_ref[...]

    if not single:
        @pl.when(i == nt - 1)
        def _():
            wait_tile(1 - slot)


def _combine(comb_src, x1, r2, mod, gf, ys, *, tt, lmax, per_row, tiles_per_seq, tile0, stride, r2_block0):
    n, d = x1.shape
    kern = functools.partial(_combine_kernel, per_row=per_row, tiles_per_seq=tiles_per_seq, tile0=tile0,
                             stride=stride, single=(n // tt == 1))
    mod_spec = (pl.BlockSpec((tt, mod.shape[1]), lambda i, s: (i, 0)) if per_row
                else pl.BlockSpec(mod.shape, lambda i, s: (0, 0)))
    grid_spec = pltpu.PrefetchScalarGridSpec(
        num_scalar_prefetch=1,
        grid=(n // tt,),
        in_specs=[pl.BlockSpec((tt, d), lambda i, s: (i, 0)),
                  pl.BlockSpec((tt, LANES), lambda i, s: (r2_block0 + i, 0)),
                  mod_spec,
                  pl.BlockSpec((1, d), lambda i, s: (0, 0)),
                  pl.BlockSpec(memory_space=pl.ANY)],
        out_specs=pl.BlockSpec((tt, d), lambda i, s: (i, 0)),
        scratch_shapes=[pltpu.VMEM((2, lmax, d), BF16), pltpu.SemaphoreType.DMA((2,))],
    )
    return pl.pallas_call(
        kern,
        grid_spec=grid_spec,
        out_shape=jax.ShapeDtypeStruct((n, d), F32),
        compiler_params=_cparams(("arbitrary",)),
        name="combine_final_per_row" if per_row else "combine_final",
    )(comb_src, x1, r2, mod, gf.reshape(1, d), ys)


def kernel(x_prompt, x_sample, c_prompt, c_sample, state_pool, state_ret, w_ada, b_ada, norm1_g, w_in,
           w_pool, pool_scale, w_up_pool, w_up_ret, w_out, norm2_g, w_router, b_router, w_expert_in,
           b_expert_in, w_expert_out, b_expert_out, final_norm_g):
    batch, seq, d = x_prompt.shape
    n_dec = x_sample.shape[0]
    n_prompt = batch * seq
    n_experts = w_router.shape[-1]
    xp = x_prompt.reshape(n_prompt, d)
    xs = x_sample.reshape(n_dec, d)

    w_in_bf = w_in[0].astype(BF16)
    wa_bf = w_up_pool[0].astype(BF16)
    wb_bf = w_up_ret[0].astype(BF16)
    wo_bf = w_out[0].astype(BF16)
    assert n_experts % 8 == 0
    wr_pad = w_router[0].T
    br_pad = jnp.broadcast_to(b_router[0][:, None], (n_experts, LANES))

    c_all = jnp.concatenate([c_prompt, c_sample], axis=0)
    c_all = jnp.pad(c_all, ((0, -c_all.shape[0] % 16), (0, 0)))
    mod = _modulation(c_all, w_ada[0], b_ada[0])
    mod_p, mod_s = mod[:batch], mod[batch:batch + n_dec]

    tm = min(512, seq)
    u_p, qkvg_p = _in_proj(xp, mod_p, norm1_g[0], w_in_bf, tm=tm, per_row=False, tiles_per_seq=seq // tm,
                           split=True)
    proj_s = _in_proj(xs, mod_s, norm1_g[0], w_in_bf, tm=n_dec, per_row=True, tiles_per_seq=1, split=False)

    ya_p, tail_p = _pool_prompt(u_p, w_pool[0], pool_scale[0], batch=batch, seq=seq, tt=min(1024, seq))
    ya_s, pool_s_t = _pool_decode(jnp.transpose(state_pool[0], (1, 0, 2)), proj_s, w_pool[0], pool_scale[0])
    yb_p, ret_p = _ret_prompt(qkvg_p, batch=batch, seq=seq, d_model=d, chunk=min(RET_CHUNK, seq))
    yb_s, ret_s = _ret_decode(proj_s, state_ret[0], d_model=d, bs=min(16, n_dec))

    tmix = min(512, seq)
    x1_p, h2_p, route_p = _mix(xp, ya_p, yb_p, qkvg_p, mod_p, norm2_g[0], wa_bf, wb_bf, wo_bf, wr_pad, br_pad,
                               tm=tmix, per_row=False, tiles_per_seq=seq // tmix, n_experts=n_experts,
                               ga_blk=6)
    x1_s, h2_s, route_s = _mix(xs, ya_s, yb_s, proj_s, mod_s, norm2_g[0], wa_bf, wb_bf, wo_bf, wr_pad, br_pad,
                               tm=n_dec, per_row=True, tiles_per_seq=1, n_experts=n_experts, ga_blk=7)

    tt = min(TOKEN_TILE, seq)
    assert n_dec <= tt
    p_tiles = n_prompt // tt
    lmax = _sorted_rows(tt, n_experts)
    xs_buf, r2, cnt_t = _dispatch(h2_p, route_p, h2_s, route_s, tt=tt, lmax=lmax, n_experts=n_experts)

    tmo = MOE_ROW_TILE
    cnt = cnt_t[:, 0].astype(jnp.int32)
    max_rows = (n_prompt + n_dec) * TOP_K + (p_tiles + 1) * n_experts * (ROW_ALIGN - 1)
    n_tiles = _round_up(max_rows // tmo + n_experts, LANES * ROW_ALIGN // tmo)
    stride = _round_up(lmax // ROW_ALIGN, LANES)
    tables = _routing_tables(cnt, tiles=p_tiles + 1, n_e=n_experts, tmo=tmo, n_tiles=n_tiles, cstride=stride,
                             lmax=lmax)
    comb_src = tables[3]

    ys = _moe(xs_buf, tables, w_expert_in[0], b_expert_in[0], w_expert_out[0], b_expert_out[0], tm=tmo,
              n_tiles=n_tiles)

    assert n_prompt % n_dec == 0
    y_p = _combine(comb_src, x1_p, r2, mod_p, final_norm_g, ys, tt=tt, lmax=lmax, per_row=False,
                   tiles_per_seq=seq // tt, tile0=0, stride=stride, r2_block0=0)
    y_s = _combine(comb_src, x1_s, r2, mod_s, final_norm_g, ys, tt=n_dec, lmax=min(lmax, _sorted_rows(n_dec, n_experts)),
                   per_row=True, tiles_per_seq=1, tile0=p_tiles, stride=stride, r2_block0=n_prompt // n_dec)

    return (y_p.reshape(batch, seq, d),
            y_s.reshape(n_dec, 1, d),
            tail_p[None, :, 1:, :],
            ret_p[None],
            jnp.transpose(pool_s_t, (1, 0, 2))[None],
            ret_s[None])
```

```python
import functools

import numpy as np
import jax
import jax.numpy as jnp
from jax import lax
from jax.experimental import pallas as pl
from jax.experimental.pallas import tpu as pltpu

F32 = jnp.float32
BF16 = jnp.bfloat16

POOL_WINDOWS = (2, 4, 8, 16)
POOL_BUF = max(POOL_WINDOWS) - 1
RET_HEADS = 4
RET_CHUNK = 256
ROPE_BASE = 10000.0
PAST_LEN = 16384
TOP_K = 4
SWIGLU_LIMIT = 7.0
SWIGLU_ALPHA = 1.702
N_MOD = 6
EPS = 1e-6
LANES = 128
ROW_ALIGN = 16
SORT_BLOCK = 512
POS_SPLIT = 64.0
POS_LANE = 16
TOKEN_TILE = 512
MOE_ROW_TILE = 1024
VMEM_LIMIT = 56 * 1024 * 1024


def _cparams(sem):
    return pltpu.CompilerParams(dimension_semantics=sem, vmem_limit_bytes=VMEM_LIMIT)


def _dot(a, b):
    return jnp.dot(a, b, preferred_element_type=F32)


def _dot3(a, b):
    a_hi = a.astype(BF16)
    a_lo = (a - a_hi.astype(F32)).astype(BF16)
    b_hi = b.astype(BF16)
    b_lo = (b - b_hi.astype(F32)).astype(BF16)
    return _dot(a_hi, b_hi) + _dot(a_hi, b_lo) + _dot(a_lo, b_hi)


def _dot3_nt(a, b):
    nt = lambda x, y: lax.dot_general(x, y, (((1,), (1,)), ((), ())), preferred_element_type=F32)
    a_hi = a.astype(BF16)
    a_lo = (a - a_hi.astype(F32)).astype(BF16)
    b_hi = b.astype(BF16)
    b_lo = (b - b_hi.astype(F32)).astype(BF16)
    return nt(a_hi, b_hi) + nt(a_hi, b_lo) + nt(a_lo, b_hi)


def _sigmoid(x):
    return 0.5 + 0.5 * jnp.tanh(0.5 * x)


def _silu(x):
    half = 0.5 * x
    return half + half * jnp.tanh(half)


def _rms(x):
    return x * lax.rsqrt(jnp.mean(x * x, axis=-1, keepdims=True) + EPS)


def _round_up(x, m):
    return (x + m - 1) // m * m


def _mod_kernel(c_ref, w_ref, b_ref, o_ref):
    c = c_ref[...]
    o_ref[...] = _dot3(_silu(c_ref[...]), w_ref[...]) + b_ref[...]


def _modulation(c, w_ada, b_ada):
    rows, d = c.shape
    n = w_ada.shape[1]
    tn = 2 * d
    return pl.pallas_call(
        _mod_kernel,
        grid=(n // tn,),
        in_specs=[pl.BlockSpec((rows, d), lambda j: (0, 0)),
                  pl.BlockSpec((d, tn), lambda j: (0, j)),
                  pl.BlockSpec((1, tn), lambda j: (0, j))],
        out_specs=pl.BlockSpec((rows, tn), lambda j: (0, j)),
        out_shape=jax.ShapeDtypeStruct((rows, n), F32),
        compiler_params=_cparams(("arbitrary",)),
        name="adaln_modulation",
    )(c, w_ada, b_ada.reshape(1, n))


def _mod_rows(mod_ref, which, d, per_row, b):
    if per_row:
        return mod_ref[:, which * d:(which + 1) * d]
    return mod_ref[pl.ds(b, 1), which * d:(which + 1) * d]


def _in_proj_kernel(x_ref, mod_ref, g_ref, w_ref, *outs, per_row, tiles_per_seq, split):
    i = pl.program_id(0)
    d = x_ref.shape[1]
    b = i // tiles_per_seq
    sh = _mod_rows(mod_ref, 0, d, per_row, b)
    sc = _mod_rows(mod_ref, 1, d, per_row, b)
    h = (_rms(x_ref[...]) * g_ref[...] * (1.0 + sc) + sh).astype(BF16)
    for c in range(w_ref.shape[1] // d):
        acc = _dot(h, w_ref[:, c * d:(c + 1) * d])
        if not split:
            outs[0][:, c * d:(c + 1) * d] = acc
        elif c == 0:
            outs[0][...] = acc
        else:
            outs[1][:, (c - 1) * d:c * d] = acc.astype(BF16)


def _in_proj(x, mod, g, w_bf, *, tm, per_row, tiles_per_seq, split):
    n, d = x.shape
    width = w_bf.shape[1]
    kern = functools.partial(_in_proj_kernel, per_row=per_row, tiles_per_seq=tiles_per_seq, split=split)
    mod_spec = (pl.BlockSpec((tm, mod.shape[1]), lambda i: (i, 0)) if per_row
                else pl.BlockSpec(mod.shape, lambda i: (0, 0)))
    if split:
        out_specs = [pl.BlockSpec((tm, d), lambda i: (i, 0)),
                     pl.BlockSpec((tm, width - d), lambda i: (i, 0))]
        out_shape = [jax.ShapeDtypeStruct((n, d), F32), jax.ShapeDtypeStruct((n, width - d), BF16)]
    else:
        out_specs = pl.BlockSpec((tm, width), lambda i: (i, 0))
        out_shape = jax.ShapeDtypeStruct((n, width), F32)
    return pl.pallas_call(
        kern,
        grid=(n // tm,),
        in_specs=[pl.BlockSpec((tm, d), lambda i: (i, 0)),
                  mod_spec,
                  pl.BlockSpec((1, d), lambda i: (0, 0)),
                  pl.BlockSpec((d, width), lambda i: (0, 0), pipeline_mode=pl.Buffered(1))],
        out_specs=out_specs,
        out_shape=out_shape,
        compiler_params=_cparams(("arbitrary",)),
        name="in_proj_per_row" if per_row else "in_proj",
    )(x, mod, g.reshape(1, d), w_bf)


def _pool_prompt_kernel(u_ref, w_ref, s_ref, y_ref, tail_ref, prev_ref, *, tt):
    t = pl.program_id(1)
    nt = pl.num_programs(1)
    hist = prev_ref.shape[0]
    gw = w_ref.shape[1]

    @pl.when(t == 0)
    def _():
        prev_ref[...] = jnp.zeros_like(prev_ref)

    u = u_ref[...]
    ext = jnp.concatenate([prev_ref[...], u], axis=0)
    pos = t * tt + lax.broadcasted_iota(jnp.int32, (tt, 1), 0)
    outs = []
    for gi, w in enumerate(POOL_WINDOWS):
        cols = slice(gi * gw, (gi + 1) * gw)
        run = ext[:, cols]
        span = 1
        while span < w:
            run = run + pltpu.roll(run, span, 0)
            span *= 2
        inv_cnt = 1.0 / jnp.minimum(w, pos + 1).astype(F32)
        pooled = run[hist:, :] * inv_cnt - u[:, cols]
        outs.append(_dot(pooled.astype(BF16), w_ref[gi].astype(BF16)))
    y = jnp.concatenate(outs, axis=-1) * s_ref[...]
    y_ref[...] = y.astype(BF16)
    prev_ref[...] = u[tt - hist:, :]

    @pl.when(t == nt - 1)
    def _():
        tail_ref[0] = u[tt - hist:, :]


def _pool_prompt(proj, w_pool, pool_scale, *, batch, seq, tt):
    p = pool_scale.shape[0]
    hist = POOL_BUF + 1
    nt = seq // tt
    kern = functools.partial(_pool_prompt_kernel, tt=tt)
    return pl.pallas_call(
        kern,
        grid=(batch, nt),
        in_specs=[pl.BlockSpec((tt, p), lambda b, t: (b * nt + t, 0)),
                  pl.BlockSpec(w_pool.shape, lambda b, t: (0, 0, 0)),
                  pl.BlockSpec((1, p), lambda b, t: (0, 0))],
        out_specs=[pl.BlockSpec((tt, p), lambda b, t: (b * nt + t, 0)),
                   pl.BlockSpec((1, hist, p), lambda b, t: (b, 0, 0))],
        out_shape=[jax.ShapeDtypeStruct((batch * seq, p), BF16),
                   jax.ShapeDtypeStruct((batch, hist, p), F32)],
        scratch_shapes=[pltpu.VMEM((hist, p), F32)],
        compiler_params=_cparams(("arbitrary", "arbitrary")),
        name="pool_prompt",
    )(proj, w_pool, pool_scale.reshape(1, p))


def _pool_decode_kernel(st_ref, u_ref, w_ref, s_ref, y_ref, new_ref):
    buf = st_ref.shape[0]
    gw = w_ref.shape[1]
    u = u_ref[...]
    outs = []
    run = u
    used = 0
    for gi, w in enumerate(POOL_WINDOWS):
        while used < w - 1:
            run = run + st_ref[buf - 1 - used]
            used += 1
        cols = slice(gi * gw, (gi + 1) * gw)
        pooled = run[:, cols] / float(min(w, PAST_LEN + 1)) - u[:, cols]
        outs.append(_dot(pooled.astype(BF16), w_ref[gi].astype(BF16)))
    y_ref[...] = (jnp.concatenate(outs, axis=-1) * s_ref[...]).astype(BF16)
    for r in range(buf - 1):
        new_ref[r] = st_ref[r + 1]
    new_ref[buf - 1] = u


def _pool_decode(state_t, proj_s, w_pool, pool_scale):
    buf, n, p = state_t.shape
    return pl.pallas_call(
        _pool_decode_kernel,
        grid=(1,),
        in_specs=[pl.BlockSpec((buf, n, p), lambda i: (0, 0, 0)),
                  pl.BlockSpec((n, p), lambda i: (0, 0)),
                  pl.BlockSpec(w_pool.shape, lambda i: (0, 0, 0)),
                  pl.BlockSpec((1, p), lambda i: (0, 0))],
        out_specs=[pl.BlockSpec((n, p), lambda i: (0, 0)),
                   pl.BlockSpec((buf, n, p), lambda i: (0, 0, 0))],
        out_shape=[jax.ShapeDtypeStruct((n, p), BF16),
                   jax.ShapeDtypeStruct((buf, n, p), F32)],
        compiler_params=_cparams(("arbitrary",)),
        name="pool_decode",
    )(state_t, proj_s, w_pool, pool_scale.reshape(1, p))


def _log_gamma(h):
    return float(np.log(1.0 - 2.0 ** (-5.0 - h)))


def _rope_tables(pos, half):
    inv = np.power(ROPE_BASE, -np.arange(half, dtype=np.float64) / half)
    ang = np.asarray(pos, np.float64)[:, None] * inv[None, :]
    return jnp.asarray(np.cos(ang), F32), jnp.asarray(np.sin(ang), F32)


def _rotary(x, cos, sin):
    half = x.shape[-1] // 2
    x1, x2 = x[:, :half], x[:, half:]
    return jnp.concatenate([x1 * cos - x2 * sin, x2 * cos + x1 * sin], axis=-1)


def _ret_tables(c, key_scale):
    i = np.arange(c, dtype=np.float64)
    diff = i[:, None] - i[None, :]
    dec, qs, ks = [], [], []
    for h in range(RET_HEADS):
        lg = _log_gamma(h)
        dec.append(key_scale * np.where(diff >= 0, np.exp(lg * np.maximum(diff, 0.0)), 0.0))
        qs.append(np.broadcast_to(np.exp(lg * (i + 1.0))[:, None], (c, LANES)))
        ks.append(key_scale * np.broadcast_to(np.exp(lg * (c - 1.0 - i))[:, None], (c, LANES)))
    return (jnp.asarray(np.stack(dec), F32), jnp.asarray(np.stack(qs), F32), jnp.asarray(np.stack(ks), F32))


def _ret_prompt_kernel(q_ref, k_ref, v_ref, g_ref, cos_ref, sin_ref, dec_ref, qs_ref, ks_ref,
                       y_ref, s_out_ref, s_ref, *, chunk_decay):
    c = pl.program_id(1)
    nc = pl.num_programs(1)
    heads, dk, dv = s_ref.shape

    @pl.when(c == 0)
    def _():
        s_ref[...] = jnp.zeros_like(s_ref)

    cos, sin = cos_ref[...], sin_ref[...]
    reps = dk // LANES
    for h in range(heads):
        q = _rotary(q_ref[:, h * dk:(h + 1) * dk].astype(F32), cos, sin)
        k = _rotary(k_ref[:, h * dk:(h + 1) * dk].astype(F32), cos, sin)
        v = v_ref[:, h * dv:(h + 1) * dv]
        qs = jnp.concatenate([qs_ref[h]] * reps, axis=-1)
        ks = jnp.concatenate([ks_ref[h]] * reps, axis=-1)
        scores = lax.dot_general(q.astype(BF16), k.astype(BF16), (((1,), (1,)), ((), ())),
                                 preferred_element_type=F32) * dec_ref[h]
        state = s_ref[h]
        o = _dot(scores.astype(BF16), v) + _dot((q * qs).astype(BF16), state.astype(BF16))
        kv = lax.dot_general((k * ks).astype(BF16), v, (((0,), (0,)), ((), ())), preferred_element_type=F32)
        new_state = chunk_decay[h] * state + kv
        s_ref[h] = new_state
        g = g_ref[:, h * dv:(h + 1) * dv].astype(F32)
        y_ref[:, h * dv:(h + 1) * dv] = (_silu(g) * _rms(o)).astype(BF16)

        @pl.when(c == nc - 1)
        def _():
            s_out_ref[0, h] = new_state


def _ret_prompt(qkvg, *, batch, seq, d_model, chunk):
    heads = RET_HEADS
    dk = d_model // heads
    dv = 2 * d_model // heads
    nc = seq // chunk
    cos, sin = _rope_tables(np.arange(seq), dk // 2)
    key_scale = dk ** -0.5
    assert np.log2(key_scale) == np.round(np.log2(key_scale))
    dec, qs, ks = _ret_tables(chunk, key_scale)
    chunk_decay = tuple(float(np.exp(_log_gamma(h) * chunk)) for h in range(heads))
    kern = functools.partial(_ret_prompt_kernel, chunk_decay=chunk_decay)
    const = lambda a: pl.BlockSpec(a.shape, lambda b, c: (0,) * a.ndim)
    return pl.pallas_call(
        kern,
        grid=(batch, nc),
        in_specs=[pl.BlockSpec((chunk, d_model), lambda b, c: (b * nc + c, 0)),
                  pl.BlockSpec((chunk, d_model), lambda b, c: (b * nc + c, 1)),
                  pl.BlockSpec((chunk, 2 * d_model), lambda b, c: (b * nc + c, 1)),
                  pl.BlockSpec((chunk, 2 * d_model), lambda b, c: (b * nc + c, 2)),
                  pl.BlockSpec((chunk, dk // 2), lambda b, c: (c, 0)),
                  pl.BlockSpec((chunk, dk // 2), lambda b, c: (c, 0)),
                  const(dec), const(qs), const(ks)],
        out_specs=[pl.BlockSpec((chunk, heads * dv), lambda b, c: (b * nc + c, 0)),
                   pl.BlockSpec((1, heads, dk, dv), lambda b, c: (b, 0, 0, 0))],
        out_shape=[jax.ShapeDtypeStruct((batch * seq, heads * dv), BF16),
                   jax.ShapeDtypeStruct((batch, heads, dk, dv), F32)],
        scratch_shapes=[pltpu.VMEM((heads, dk, dv), F32)],
        compiler_params=_cparams(("arbitrary", "arbitrary")),
        name="retention_prompt",
    )(qkvg, qkvg, qkvg, qkvg, cos, sin, dec, qs, ks)


def _ret_decode_kernel(q_ref, k_ref, v_ref, g_ref, cos_ref, sin_ref, s_ref, y_ref, s_out_ref,
                       kt_ref, qd_ref, o_ref, *, gammas, bs):
    h = pl.program_id(0)
    blk = pl.program_id(1)
    nblk = pl.num_programs(1)
    n, dk = q_ref.shape
    gamma = jnp.where(h == 0, gammas[0], jnp.where(h == 1, gammas[1], jnp.where(h == 2, gammas[2], gammas[3])))

    @pl.when(blk == 0)
    def _():
        cos, sin = cos_ref[...], sin_ref[...]
        q = _rotary(q_ref[...], cos, sin)
        k = _rotary(k_ref[...], cos, sin) * (dk ** -0.5)
        kt_ref[...] = k.T
        qd_ref[...] = (q * gamma).astype(BF16)
        o_ref[...] = jnp.sum(q * k, axis=-1, keepdims=True) * v_ref[...]

    vb = v_ref[...].astype(BF16)
    rows = lax.broadcasted_iota(jnp.int32, (n, 1), 0)
    lanes = lax.broadcasted_iota(jnp.int32, (1, n), 1)
    for i in range(bs):
        tok = blk * bs + i
        state = s_ref[i, 0]
        o_ref[...] += jnp.where(rows == tok, _dot(qd_ref[...], state.astype(BF16)), 0.0)
        k_col = jnp.where(lanes == tok, kt_ref[...], 0.0).astype(BF16)
        s_out_ref[i, 0] = gamma * state + _dot(k_col, vb)

    @pl.when(blk == nblk - 1)
    def _():
        g = g_ref[...]
        y_ref[...] = (_silu(g) * _rms(o_ref[...])).astype(BF16)


def _ret_decode(proj_s, state, *, d_model, bs):
    n = proj_s.shape[0]
    heads = RET_HEADS
    dk = d_model // heads
    dv = 2 * d_model // heads
    cos, sin = _rope_tables(np.array([PAST_LEN]), dk // 2)
    gammas = tuple(float(np.exp(_log_gamma(h))) for h in range(heads))
    q0, k0, v0, g0 = d_model // dk, 2 * d_model // dk, 3 * d_model // dv, 5 * d_model // dv
    kern = functools.partial(_ret_decode_kernel, gammas=gammas, bs=bs)
    return pl.pallas_call(
        kern,
        grid=(heads, n // bs),
        in_specs=[pl.BlockSpec((n, dk), lambda h, b: (0, q0 + h)),
                  pl.BlockSpec((n, dk), lambda h, b: (0, k0 + h)),
                  pl.BlockSpec((n, dv), lambda h, b: (0, v0 + h)),
                  pl.BlockSpec((n, dv), lambda h, b: (0, g0 + h)),
                  pl.BlockSpec((1, dk // 2), lambda h, b: (0, 0)),
                  pl.BlockSpec((1, dk // 2), lambda h, b: (0, 0)),
                  pl.BlockSpec((bs, 1, dk, dv), lambda h, b: (b, h, 0, 0))],
        out_specs=[pl.BlockSpec((n, dv), lambda h, b: (0, h)),
                   pl.BlockSpec((bs, 1, dk, dv), lambda h, b: (b, h, 0, 0))],
        out_shape=[jax.ShapeDtypeStruct((n, heads * dv), BF16),
                   jax.ShapeDtypeStruct(state.shape, F32)],
        scratch_shapes=[pltpu.VMEM((dk, n), F32), pltpu.VMEM((n, dk), BF16), pltpu.VMEM((n, dv), F32)],
        compiler_params=_cparams(("arbitrary", "arbitrary")),
        name="retention_decode",
    )(proj_s, proj_s, proj_s, proj_s, cos, sin, state)


def _mix_kernel(x_ref, ya_ref, yb_ref, ga_ref, gb_ref, mod_ref, g2_ref, wa_ref, wb_ref, wo_ref,
                wr_ref, br_ref, x1_ref, h2_ref, route_ref, *, per_row, tiles_per_seq, n_experts):
    i = pl.program_id(0)
    tm, d = x_ref.shape
    b = i // tiles_per_seq

    merged = (_sigmoid(ga_ref[...].astype(F32)) * _dot(ya_ref[...], wa_ref[...])
              + _sigmoid(gb_ref[...].astype(F32)) * _dot(yb_ref[...], wb_ref[...]))
    mix = _dot(merged.astype(BF16), wo_ref[...])
    x1 = x_ref[...] + _mod_rows(mod_ref, 2, d, per_row, b) * mix
    x1_ref[...] = x1
    h2 = (_rms(x1) * g2_ref[...] * (1.0 + _mod_rows(mod_ref, 4, d, per_row, b))
          + _mod_rows(mod_ref, 3, d, per_row, b))
    h2_ref[...] = h2.astype(BF16)

    logits_t = _dot3_nt(wr_ref[...], h2) + br_ref[:, 0:1]
    row = lax.broadcasted_iota(jnp.int32, (n_experts, tm), 0)
    neg = jnp.float32(-jnp.inf)
    work = logits_t
    vals, idxs = [], []
    for _ in range(TOP_K):
        m = jnp.max(work, axis=0, keepdims=True)
        idx = jnp.min(jnp.where(work == m, row, n_experts), axis=0, keepdims=True)
        vals.append(m)
        idxs.append(idx)
        work = jnp.where(row == idx, neg, work)
    exps = [jnp.exp(v - vals[0]) for v in vals]
    denom = exps[0] + exps[1] + exps[2] + exps[3]
    route_ref[...] = jnp.concatenate([idx.astype(F32) for idx in idxs] + [e / denom for e in exps]
                                     + [jnp.zeros((LANES - 2 * TOP_K, tm), F32)], axis=0)


def _mix(x, ya, yb, proj, mod, g2, wa_bf, wb_bf, wo_bf, wr_pad, br_pad, *, tm, per_row, tiles_per_seq,
         n_experts, ga_blk):
    n, d = x.shape
    gb_blk = ga_blk + 1
    kern = functools.partial(_mix_kernel, per_row=per_row, tiles_per_seq=tiles_per_seq, n_experts=n_experts)
    mod_spec = (pl.BlockSpec((tm, mod.shape[1]), lambda i: (i, 0)) if per_row
                else pl.BlockSpec(mod.shape, lambda i: (0, 0)))
    full = lambda a: pl.BlockSpec(a.shape, lambda i: (0,) * a.ndim)
    return pl.pallas_call(
        kern,
        grid=(n // tm,),
        in_specs=[pl.BlockSpec((tm, d), lambda i: (i, 0)),
                  pl.BlockSpec((tm, ya.shape[1]), lambda i: (i, 0)),
                  pl.BlockSpec((tm, yb.shape[1]), lambda i: (i, 0)),
                  pl.BlockSpec((tm, d), lambda i: (i, ga_blk)),
                  pl.BlockSpec((tm, d), lambda i: (i, gb_blk)),
                  mod_spec,
                  pl.BlockSpec((1, d), lambda i: (0, 0)),
                  full(wa_bf), full(wb_bf), full(wo_bf), full(wr_pad), full(br_pad)],
        out_specs=[pl.BlockSpec((tm, d), lambda i: (i, 0)),
                   pl.BlockSpec((tm, d), lambda i: (i, 0)),
                   pl.BlockSpec((LANES, tm), lambda i: (0, i))],
        out_shape=[jax.ShapeDtypeStruct((n, d), F32),
                   jax.ShapeDtypeStruct((n, d), BF16),
                   jax.ShapeDtypeStruct((LANES, n), F32)],
        compiler_params=_cparams(("arbitrary",)),
        name="mix_route_per_row" if per_row else "mix_route",
    )(x, ya, yb, proj, proj, mod, g2.reshape(1, d), wa_bf, wb_bf, wo_bf, wr_pad, br_pad)


def _sorted_rows(tt, n_experts):
    return _round_up(tt * TOP_K + n_experts * (ROW_ALIGN - 1) + ROW_ALIGN, SORT_BLOCK)


def _dispatch_kernel(hp_ref, rp_ref, hd_ref, rd_ref, before_ref, below_ref, xs_ref, r2_ref, cnt_ref, *, lmax):
    tt = hp_ref.shape[0]
    n_e = cnt_ref.shape[0]
    is_dec = pl.program_id(0) == pl.num_programs(0) - 1
    route_t = jnp.where(is_dec, rd_ref[...], rp_ref[...])
    expert = lax.broadcasted_iota(jnp.int32, (n_e, tt), 0).astype(F32)
    hits = [expert == route_t[kk:kk + 1, :] for kk in range(TOP_K)]
    sel = jnp.zeros((n_e, tt), F32)
    for hit in hits:
        sel = sel + jnp.where(hit, 1.0, 0.0)
    cnt = jnp.sum(sel, axis=1, keepdims=True)
    seg = jnp.broadcast_to(jnp.ceil(cnt / ROW_ALIGN) * ROW_ALIGN, (n_e, LANES))
    off = _dot3(below_ref[...], seg)[:, 0:1]
    place = _dot(sel.astype(BF16), before_ref[...]) + off
    w_hi, w_lo, pos_a, pos_b, pos_all = [], [], [], [], []
    for kk in range(TOP_K):
        found = jnp.sum(jnp.where(hits[kk], 1.0, 0.0), axis=0, keepdims=True)
        pos = jnp.where(found > 0.0, jnp.sum(jnp.where(hits[kk], place, 0.0), axis=0, keepdims=True), -1.0)
        w = route_t[TOP_K + kk:TOP_K + kk + 1, :]
        hi = w.astype(BF16).astype(F32)
        a = jnp.floor(pos / POS_SPLIT)
        w_hi.append(hi)
        w_lo.append(w - hi)
        pos_a.append(a)
        pos_b.append(pos - POS_SPLIT * a)
        pos_all.append(pos)
    info_t = jnp.concatenate(w_hi + w_lo + pos_a + pos_b + pos_all
                             + [jnp.zeros((LANES - 5 * TOP_K, tt), F32)], axis=0)
    side = info_t.T
    r2_ref[...] = side
    cnt_ref[...] = jnp.broadcast_to(cnt, cnt_ref.shape)
    r2t = info_t.astype(jnp.int32).astype(jnp.int16)
    h = jnp.concatenate([jnp.where(is_dec, hd_ref[...], hp_ref[...]), side.astype(BF16)], axis=-1)
    d = hp_ref.shape[1]
    out_lane = lax.broadcasted_iota(jnp.int32, (SORT_BLOCK, LANES), 1)
    one = jnp.ones((SORT_BLOCK, tt), BF16)
    zero = jnp.zeros((SORT_BLOCK, tt), BF16)
    for rb in range(lmax // SORT_BLOCK):
        rio = (lax.broadcasted_iota(jnp.int32, (SORT_BLOCK, tt), 0) + rb * SORT_BLOCK).astype(jnp.int16)
        hit = rio == r2t[POS_LANE:POS_LANE + 1, :]
        for kk in range(1, TOP_K):
            hit = hit | (rio == r2t[POS_LANE + kk:POS_LANE + kk + 1, :])
        perm = jnp.where(hit, one, zero)
        res = _dot(perm, h)
        info = res[:, d:]
        row = (lax.broadcasted_iota(jnp.int32, (SORT_BLOCK, LANES), 0) + rb * SORT_BLOCK).astype(F32)
        pos = (POS_SPLIT * pltpu.roll(info, LANES - 2 * TOP_K, 1) + pltpu.roll(info, LANES - 3 * TOP_K, 1))
        mine = jnp.where((pos == row) & (out_lane < TOP_K), 1.0, 0.0)
        mine = mine + pltpu.roll(mine, TOP_K, 1)
        rows = slice(rb * SORT_BLOCK, (rb + 1) * SORT_BLOCK)
        xs_ref[rows, :d] = res[:, :d].astype(BF16)
        xs_ref[rows, d:] = (info * mine).astype(BF16)


def _dispatch(h2_p, route_p, h2_d, route_d, *, tt, lmax, n_experts):
    n, d = h2_p.shape
    p_tiles = n // tt
    tiles = p_tiles + 1
    pad = tt - h2_d.shape[0]
    h2_d = jnp.pad(h2_d, ((0, pad), (0, 0)))
    route_d = jnp.pad(route_d, ((0, 0), (0, pad)), constant_values=-1.0)
    kern = functools.partial(_dispatch_kernel, lmax=lmax)
    before = jnp.asarray(np.triu(np.ones((tt, tt), np.float32), 1), BF16)
    below = jnp.asarray(np.tril(np.ones((n_experts, n_experts), np.float32), -1), F32)
    return pl.pallas_call(
        kern,
        grid=(tiles,),
        in_specs=[pl.BlockSpec((tt, d), lambda i: (jnp.minimum(i, p_tiles - 1), 0)),
                  pl.BlockSpec((LANES, tt), lambda i: (0, jnp.minimum(i, p_tiles - 1))),
                  pl.BlockSpec((tt, d), lambda i: (0, 0)),
                  pl.BlockSpec((LANES, tt), lambda i: (0, 0)),
                  pl.BlockSpec((tt, tt), lambda i: (0, 0)),
                  pl.BlockSpec((n_experts, n_experts), lambda i: (0, 0))],
        out_specs=[pl.BlockSpec((lmax, d + LANES), lambda i: (i, 0)),
                   pl.BlockSpec((tt, LANES), lambda i: (i, 0)),
                   pl.BlockSpec((n_experts, LANES), lambda i: (i, 0))],
        out_shape=[jax.ShapeDtypeStruct((tiles * lmax, d + LANES), BF16),
                   jax.ShapeDtypeStruct((tiles * tt, LANES), F32),
                   jax.ShapeDtypeStruct((tiles * n_experts, LANES), F32)],
        compiler_params=_cparams(("arbitrary",)),
        name="dispatch_sort",
    )(h2_p, route_p, h2_d, route_d, before, below)


def _tables_kernel(cnt_ref, te_ref, na_ref, nxt_ref, par_ref, full_ref, msrc_ref, csrc_ref,
                   before_ref, start_ref, len_ref, first_ref, end_ref, total_ref, *, tiles, n_e, cpt, cstride,
                   lmax, n_tiles, zero_row):
    def nblk(i, e):
        return len_ref[i * n_e + e]

    def per_tile(i, carry):
        def per_expert(e, run):
            blocks = lax.shift_right_logical(cnt_ref[i * n_e + e] + (ROW_ALIGN - 1), ROW_ALIGN.bit_length() - 1)
            len_ref[i * n_e + e] = blocks
            start_ref[i * n_e + e] = run
            return run + blocks
        lax.fori_loop(0, n_e, per_expert, 0)
        return carry

    lax.fori_loop(0, tiles, per_tile, 0)

    def per_expert(e, carry):
        def per_tile(i, run):
            before_ref[i * n_e + e] = run
            return run + nblk(i, e)
        total_ref[e] = lax.fori_loop(0, tiles, per_tile, 0)
        return carry

    lax.fori_loop(0, n_e, per_expert, 0)

    def expert_tiles(e, carry):
        t_first, run = carry
        t_last = t_first + lax.shift_right_logical(total_ref[e] + (cpt - 1), cpt.bit_length() - 1)
        first_ref[e] = t_first
        end_ref[e] = t_last

        def mark(t, carry):
            te_ref[t] = e
            par_ref[t] = run % 2
            full_ref[t] = (total_ref[e] - (t - t_first) * cpt > cpt // 2).astype(jnp.int32)
            return carry

        lax.fori_loop(t_first, t_last, mark, 0)
        return t_last, jnp.where(t_last > t_first, run + 1, run)

    n_act, _ = lax.fori_loop(0, n_e, expert_tiles, (jnp.int32(0), jnp.int32(0)))
    na_ref[0] = n_act
    last = te_ref[jnp.maximum(n_act - 1, 0)]

    def mark_rest(t, carry):
        te_ref[t] = last
        par_ref[t] = 0
        full_ref[t] = 0
        nxt_ref[t] = -1
        return carry

    lax.fori_loop(n_act, n_tiles, mark_rest, 0)

    def mark_next(s, nxt):
        t = n_act - 1 - s
        after = te_ref[jnp.minimum(t + 1, n_tiles - 1)]
        nxt = jnp.where((t + 1 < n_act) & (after != te_ref[t]), after, nxt)
        nxt_ref[t] = nxt
        return nxt

    lax.fori_loop(0, n_act, mark_next, jnp.int32(-1))

    shape = msrc_ref.shape
    n_idx = (lax.broadcasted_iota(jnp.int32, shape, 0) * LANES + lax.broadcasted_iota(jnp.int32, shape, 1))
    m_tile = lax.shift_right_logical(n_idx, cpt.bit_length() - 1)
    expert = jnp.zeros(shape, jnp.int32)
    for e in range(n_e):
        expert = expert + (end_ref[e] <= m_tile).astype(jnp.int32)
    t_first = jnp.zeros(shape, jnp.int32)
    total = jnp.zeros(shape, jnp.int32)
    for e in range(n_e):
        t_first = jnp.where(expert == e, first_ref[e], t_first)
        total = jnp.where(expert == e, total_ref[e], total)
    g = (m_tile - t_first) * cpt + (n_idx & (cpt - 1))
    src = jnp.full(shape, zero_row, jnp.int32)
    for i in range(tiles):
        seg_first = jnp.zeros(shape, jnp.int32)
        seg_len = jnp.zeros(shape, jnp.int32)
        seg_local = jnp.zeros(shape, jnp.int32)
        for e in range(n_e):
            mine = expert == e
            seg_first = jnp.where(mine, before_ref[i * n_e + e], seg_first)
            seg_len = jnp.where(mine, nblk(i, e), seg_len)
            seg_local = jnp.where(mine, start_ref[i * n_e + e], seg_local)
        inside = (g >= seg_first) & (g < seg_first + seg_len) & (g < total) & (m_tile < n_act)
        src = jnp.where(inside, i * lmax + (seg_local + g - seg_first) * ROW_ALIGN, src)
    msrc_ref[...] = src

    rows_per_tile = cstride // LANES
    c_idx = (lax.broadcasted_iota(jnp.int32, (rows_per_tile, LANES), 0) * LANES
             + lax.broadcasted_iota(jnp.int32, (rows_per_tile, LANES), 1))
    for i in range(tiles):
        owner = jnp.zeros((rows_per_tile, LANES), jnp.int32)
        for e in range(n_e):
            owner = owner + ((start_ref[i * n_e + e] + nblk(i, e)) <= c_idx).astype(jnp.int32)
        base = jnp.zeros((rows_per_tile, LANES), jnp.int32)
        for e in range(n_e):
            base = jnp.where(owner == e, first_ref[e] * cpt + before_ref[i * n_e + e] - start_ref[i * n_e + e],
                             base)
        csrc_ref[i * rows_per_tile:(i + 1) * rows_per_tile, :] = jnp.where(owner < n_e,
                                                                         (base + c_idx) * ROW_ALIGN, 0)


def _routing_tables(cnt_flat, *, tiles, n_e, tmo, n_tiles, cstride, lmax):
    cpt = tmo // ROW_ALIGN
    assert (n_tiles * cpt) % LANES == 0 and cstride % LANES == 0
    kern = functools.partial(_tables_kernel, tiles=tiles, n_e=n_e, cpt=cpt, cstride=cstride, lmax=lmax,
                             n_tiles=n_tiles, zero_row=lmax - ROW_ALIGN)
    smem = pl.BlockSpec(memory_space=pltpu.SMEM)
    vmem = pl.BlockSpec(memory_space=pltpu.VMEM)
    te, na, nxt, par, full, msrc, csrc = pl.pallas_call(
        kern,
        in_specs=[smem],
        out_specs=[smem, smem, smem, smem, smem, vmem, vmem],
        out_shape=[jax.ShapeDtypeStruct((n_tiles,), jnp.int32),
                   jax.ShapeDtypeStruct((1,), jnp.int32),
                   jax.ShapeDtypeStruct((n_tiles,), jnp.int32),
                   jax.ShapeDtypeStruct((n_tiles,), jnp.int32),
                   jax.ShapeDtypeStruct((n_tiles,), jnp.int32),
                   jax.ShapeDtypeStruct((n_tiles * cpt // LANES, LANES), jnp.int32),
                   jax.ShapeDtypeStruct((tiles * cstride // LANES, LANES), jnp.int32)],
        scratch_shapes=[pltpu.SMEM((tiles * n_e,), jnp.int32), pltpu.SMEM((tiles * n_e,), jnp.int32),
                        pltpu.SMEM((tiles * n_e,), jnp.int32),
                        pltpu.SMEM((n_e,), jnp.int32), pltpu.SMEM((n_e,), jnp.int32),
                        pltpu.SMEM((n_e,), jnp.int32)],
        name="routing_tables",
    )(cnt_flat)
    return te, na, msrc.reshape(-1), csrc.reshape(-1), nxt, par, full


def _moe_kernel(tile_expert_ref, n_active_ref, src_ref, nxt_ref, par_ref, full_ref, xs_hbm, wi_hbm, bi_ref, wo_hbm,
                bo_ref, y_ref, xbuf, wi_f32, wo_f32, wi_bf, wo_bf, sem, wsem):
    j = pl.program_id(0)
    tm = xbuf.shape[1]
    d = y_ref.shape[1]
    f = wo_bf.shape[0]
    cpt = tm // ROW_ALIGN
    slot = j % 2
    n_active = n_active_ref[0]
    expert = tile_expert_ref[j]

    def start_tile(tile, s):
        for c in range(cpt):
            row = pl.multiple_of(src_ref[tile * cpt + c], ROW_ALIGN)
            pltpu.make_async_copy(xs_hbm.at[pl.ds(row, ROW_ALIGN), :],
                                  xbuf.at[s, pl.ds(c * ROW_ALIGN, ROW_ALIGN), :], sem.at[s]).start()

    def wait_tile(s):
        pltpu.make_async_copy(xs_hbm.at[pl.ds(0, tm), :], xbuf.at[s], sem.at[s]).wait()

    def weight_copies(e, s):
        return (pltpu.make_async_copy(wi_hbm.at[e], wi_f32.at[s], wsem.at[0, s]),
                pltpu.make_async_copy(wo_hbm.at[e], wo_f32.at[s], wsem.at[1, s]))

    @pl.when(j == 0)
    def _():
        start_tile(0, 0)
        for cp in weight_copies(expert, par_ref[0]):
            cp.start()

    @pl.when(j + 1 < n_active)
    def _():
        start_tile(j + 1, 1 - slot)

    @pl.when(j < n_active)
    def _():
        @pl.when((j == 0) | (expert != tile_expert_ref[jnp.maximum(j - 1, 0)]))
        def _():
            ws = par_ref[j]
            for cp in weight_copies(expert, ws):
                cp.wait()

            @pl.when(nxt_ref[j] >= 0)
            def _():
                for cp in weight_copies(nxt_ref[j], 1 - ws):
                    cp.start()

            wi_bf[...] = wi_f32[ws].astype(BF16)
            wo_bf[...] = wo_f32[ws].astype(BF16)

        wait_tile(slot)
        half = tm // 2

        def expert_mlp(r0):
            rows = xbuf[slot, r0:r0 + half, :]
            hh = _dot(rows[:, :d], wi_bf[...]) + bi_ref[0]
            gate = jnp.minimum(hh[:, :f], SWIGLU_LIMIT)
            lin = jnp.clip(hh[:, f:], -SWIGLU_LIMIT, SWIGLU_LIMIT)
            half_gate = 0.5 * gate
            act = (lin + 1.0) * (half_gate + half_gate * jnp.tanh((0.5 * SWIGLU_ALPHA) * gate))
            weight = jnp.sum(rows[:, d:].astype(F32), axis=-1, keepdims=True)
            y_ref[r0:r0 + half, :] = ((_dot(act.astype(BF16), wo_bf[...]) + bo_ref[0]) * weight).astype(BF16)

        expert_mlp(0)

        @pl.when(full_ref[j] > 0)
        def _():
            expert_mlp(half)

        @pl.when(full_ref[j] == 0)
        def _():
            y_ref[half:, :] = jnp.zeros((half, d), BF16)

    @pl.when(j >= n_active)
    def _():
        y_ref[...] = jnp.zeros_like(y_ref)


def _moe(xs, tables, w_in, b_in, w_out, b_out, *, tm, n_tiles):
    tile_expert, n_active, moe_src, _, nxt, par, full = tables
    e, d, f2 = w_in.shape
    f = w_out.shape[1]
    bias = lambda width: pl.BlockSpec((1, 1, width), lambda j, te, *_: (te[j], 0, 0))
    grid_spec = pltpu.PrefetchScalarGridSpec(
        num_scalar_prefetch=6,
        grid=(n_tiles,),
        in_specs=[pl.BlockSpec(memory_space=pl.ANY),
                  pl.BlockSpec(memory_space=pl.ANY),
                  bias(f2),
                  pl.BlockSpec(memory_space=pl.ANY),
                  bias(d)],
        out_specs=pl.BlockSpec((tm, d), lambda j, *_: (j, 0)),
        scratch_shapes=[pltpu.VMEM((2, tm, xs.shape[1]), BF16),
                        pltpu.VMEM((2, d, f2), F32),
                        pltpu.VMEM((2, f, d), F32),
                        pltpu.VMEM((d, f2), BF16),
                        pltpu.VMEM((f, d), BF16),
                        pltpu.SemaphoreType.DMA((2,)),
                        pltpu.SemaphoreType.DMA((2, 2))],
    )
    return pl.pallas_call(
        _moe_kernel,
        grid_spec=grid_spec,
        out_shape=jax.ShapeDtypeStruct((n_tiles * tm, d), BF16),
        compiler_params=_cparams(("arbitrary",)),
        name="moe_experts",
    )(tile_expert, n_active, moe_src, nxt, par, full, xs, w_in, b_in.reshape(e, 1, f2), w_out,
      b_out.reshape(e, 1, d))


def _combine_kernel(src_ref, x1_ref, r2_ref, mod_ref, gf_ref, y_hbm, o_ref, ybuf, sem,
                    *, per_row, tiles_per_seq, tile0, stride, single):
    i = pl.program_id(0)
    nt = pl.num_programs(0)
    tt, d = x1_ref.shape
    lmax = ybuf.shape[1]
    nch = lmax // ROW_ALIGN
    slot = i % 2
    b = i // tiles_per_seq

    def start_tile(tile, s):
        for c in range(nch):
            row = pl.multiple_of(src_ref[(tile0 + tile) * stride + c], ROW_ALIGN)
            pltpu.make_async_copy(y_hbm.at[pl.ds(row, ROW_ALIGN), :],
                                  ybuf.at[s, pl.ds(c * ROW_ALIGN, ROW_ALIGN), :], sem.at[s]).start()

    def wait_tile(s):
        pltpu.make_async_copy(y_hbm.at[pl.ds(0, lmax), :], ybuf.at[s], sem.at[s]).wait()

    @pl.when(i == 0)
    def _():
        start_tile(0, 0)

    if not single:
        start_tile(jnp.minimum(i + 1, nt - 1), 1 - slot)
    wait_tile(slot)
    r2 = r2_ref[...].astype(jnp.int32).astype(jnp.int16)
    one = jnp.ones((tt, SORT_BLOCK), BF16)
    zero = jnp.zeros((tt, SORT_BLOCK), BF16)
    moe = jnp.zeros((tt, d), F32)
    for cb in range(lmax // SORT_BLOCK):
        col = (lax.broadcasted_iota(jnp.int32, (tt, SORT_BLOCK), 1) + cb * SORT_BLOCK).astype(jnp.int16)
        hit = col == r2[:, POS_LANE:POS_LANE + 1]
        for kk in range(1, TOP_K):
            hit = hit | (col == r2[:, POS_LANE + kk:POS_LANE + kk + 1])
        moe = moe + _dot(jnp.where(hit, one, zero), ybuf[slot, cb * SORT_BLOCK:(cb + 1) * SORT_BLOCK, :])
    x2 = x1_ref[...] + _mod_rows(mod_ref, 5, d, per_row, b) * moe
    o_ref[...] = _rms(x2) * gf_ref[...]

    if not single:
        @pl.when(i == nt - 1)
        def _():
            wait_tile(1 - slot)


def _combine(comb_src, x1, r2, mod, gf, ys, *, tt, lmax, per_row, tiles_per_seq, tile0, stride, r2_block0):
    n, d = x1.shape
    kern = functools.partial(_combine_kernel, per_row=per_row, tiles_per_seq=tiles_per_seq, tile0=tile0,
                             stride=stride, single=(n // tt == 1))
    mod_spec = (pl.BlockSpec((tt, mod.shape[1]), lambda i, s: (i, 0)) if per_row
                else pl.BlockSpec(mod.shape, lambda i, s: (0, 0)))
    grid_spec = pltpu.PrefetchScalarGridSpec(
        num_scalar_prefetch=1,
        grid=(n // tt,),
        in_specs=[pl.BlockSpec((tt, d), lambda i, s: (i, 0)),
                  pl.BlockSpec((tt, LANES), lambda i, s: (r2_block0 + i, 0)),
                  mod_spec,
                  pl.BlockSpec((1, d), lambda i, s: (0, 0)),
                  pl.BlockSpec(memory_space=pl.ANY)],
        out_specs=pl.BlockSpec((tt, d), lambda i, s: (i, 0)),
        scratch_shapes=[pltpu.VMEM((2, lmax, d), BF16), pltpu.SemaphoreType.DMA((2,))],
    )
    return pl.pallas_call(
        kern,
        grid_spec=grid_spec,
        out_shape=jax.ShapeDtypeStruct((n, d), F32),
        compiler_params=_cparams(("arbitrary",)),
        name="combine_final_per_row" if per_row else "combine_final",
    )(comb_src, x1, r2, mod, gf.reshape(1, d), ys)


def kernel(x_prompt, x_sample, c_prompt, c_sample, state_pool, state_ret, w_ada, b_ada, norm1_g, w_in,
           w_pool, pool_scale, w_up_pool, w_up_ret, w_out, norm2_g, w_router, b_router, w_expert_in,
           b_expert_in, w_expert_out, b_expert_out, final_norm_g):
    batch, seq, d = x_prompt.shape
    n_dec = x_sample.shape[0]
    n_prompt = batch * seq
    n_experts = w_router.shape[-1]
    xp = x_prompt.reshape(n_prompt, d)
    xs = x_sample.reshape(n_dec, d)

    w_in_bf = w_in[0].astype(BF16)
    wa_bf = w_up_pool[0].astype(BF16)
    wb_bf = w_up_ret[0].astype(BF16)
    wo_bf = w_out[0].astype(BF16)
    assert n_experts % 8 == 0
    wr_pad = w_router[0].T
    br_pad = jnp.broadcast_to(b_router[0][:, None], (n_experts, LANES))

    c_all = jnp.concatenate([c_prompt, c_sample], axis=0)
    c_all = jnp.pad(c_all, ((0, -c_all.shape[0] % 16), (0, 0)))
    mod = _modulation(c_all, w_ada[0], b_ada[0])
    mod_p, mod_s = mod[:batch], mod[batch:batch + n_dec]

    tm = min(512, seq)
    u_p, qkvg_p = _in_proj(xp, mod_p, norm1_g[0], w_in_bf, tm=tm, per_row=False, tiles_per_seq=seq // tm,
                           split=True)
    proj_s = _in_proj(xs, mod_s, norm1_g[0], w_in_bf, tm=n_dec, per_row=True, tiles_per_seq=1, split=False)

    ya_p, tail_p = _pool_prompt(u_p, w_pool[0], pool_scale[0], batch=batch, seq=seq, tt=min(1024, seq))
    ya_s, pool_s_t = _pool_decode(jnp.transpose(state_pool[0], (1, 0, 2)), proj_s, w_pool[0], pool_scale[0])
    yb_p, ret_p = _ret_prompt(qkvg_p, batch=batch, seq=seq, d_model=d, chunk=min(RET_CHUNK, seq))
    yb_s, ret_s = _ret_decode(proj_s, state_ret[0], d_model=d, bs=min(16, n_dec))

    tmix = min(512, seq)
    x1_p, h2_p, route_p = _mix(xp, ya_p, yb_p, qkvg_p, mod_p, norm2_g[0], wa_bf, wb_bf, wo_bf, wr_pad, br_pad,
                               tm=tmix, per_row=False, tiles_per_seq=seq // tmix, n_experts=n_experts,
                               ga_blk=6)
    x1_s, h2_s, route_s = _mix(xs, ya_s, yb_s, proj_s, mod_s, norm2_g[0], wa_bf, wb_bf, wo_bf, wr_pad, br_pad,
                               tm=n_dec, per_row=True, tiles_per_seq=1, n_experts=n_experts, ga_blk=7)

    tt = min(TOKEN_TILE, seq)
    assert n_dec <= tt
    p_tiles = n_prompt // tt
    lmax = _sorted_rows(tt, n_experts)
    xs_buf, r2, cnt_t = _dispatch(h2_p, route_p, h2_s, route_s, tt=tt, lmax=lmax, n_experts=n_experts)

    tmo = MOE_ROW_TILE
    cnt = cnt_t[:, 0].astype(jnp.int32)
    max_rows = (n_prompt + n_dec) * TOP_K + (p_tiles + 1) * n_experts * (ROW_ALIGN - 1)
    n_tiles = _round_up(max_rows // tmo + n_experts, LANES * ROW_ALIGN // tmo)
    stride = _round_up(lmax // ROW_ALIGN, LANES)
    tables = _routing_tables(cnt, tiles=p_tiles + 1, n_e=n_experts, tmo=tmo, n_tiles=n_tiles, cstride=stride,
                             lmax=lmax)
    comb_src = tables[3]

    ys = _moe(xs_buf, tables, w_expert_in[0], b_expert_in[0], w_expert_out[0], b_expert_out[0], tm=tmo,
              n_tiles=n_tiles)

    assert n_prompt % n_dec == 0
    y_p = _combine(comb_src, x1_p, r2, mod_p, final_norm_g, ys, tt=tt, lmax=lmax, per_row=False,
                   tiles_per_seq=seq // tt, tile0=0, stride=stride, r2_block0=0)
    y_s = _combine(comb_src, x1_s, r2, mod_s, final_norm_g, ys, tt=n_dec, lmax=min(lmax, _sorted_rows(n_dec, n_experts)),
                   per_row=True, tiles_per_seq=1, tile0=p_tiles, stride=stride, r2_block0=n_prompt // n_dec)

    return (y_p.reshape(batch, seq, d),
            y_s.reshape(n_dec, 1, d),
            tail_p[None, :, 1:, :],
            ret_p[None],
            jnp.transpose(pool_s_t, (1, 0, 2))[None],
            ret_s[None])
```

```python
import functools

import numpy as np
import jax
import jax.numpy as jnp
from jax import lax
from jax.experimental import pallas as pl
from jax.experimental.pallas import tpu as pltpu

F32 = jnp.float32
BF16 = jnp.bfloat16

POOL_WINDOWS = (2, 4, 8, 16)
POOL_BUF = max(POOL_WINDOWS) - 1
RET_HEADS = 4
RET_CHUNK = 256
ROPE_BASE = 10000.0
PAST_LEN = 16384
TOP_K = 4
SWIGLU_LIMIT = 7.0
SWIGLU_ALPHA = 1.702
EPS = 1e-6
LANES = 128
ROW_ALIGN = 16
SORT_BLOCK = 512
POS_SPLIT = 64.0
POS_LANE = 16
TOKEN_TILE = 512
MOE_ROW_TILE = 512
IN_PROJ_ROWS = 512
MIX_ROWS = 512
POOL_ROWS = 1024
RET_DECODE_SEQS = 16
VMEM_LIMIT = 56 * 1024 * 1024


def _cparams(sem):
    return pltpu.CompilerParams(dimension_semantics=sem, vmem_limit_bytes=VMEM_LIMIT)


def _dot(a, b):
    return jnp.dot(a, b, preferred_element_type=F32)


def _dot3(a, b):
    a_hi = a.astype(BF16)
    a_lo = (a - a_hi.astype(F32)).astype(BF16)
    b_hi = b.astype(BF16)
    b_lo = (b - b_hi.astype(F32)).astype(BF16)
    return _dot(a_hi, b_hi) + _dot(a_hi, b_lo) + _dot(a_lo, b_hi)


def _dot3_nt(a, b):
    nt = lambda x, y: lax.dot_general(x, y, (((1,), (1,)), ((), ())), preferred_element_type=F32)
    a_hi = a.astype(BF16)
    a_lo = (a - a_hi.astype(F32)).astype(BF16)
    b_hi = b.astype(BF16)
    b_lo = (b - b_hi.astype(F32)).astype(BF16)
    return nt(a_hi, b_hi) + nt(a_hi, b_lo) + nt(a_lo, b_hi)


def _sigmoid(x):
    return 0.5 + 0.5 * jnp.tanh(0.5 * x)


def _silu(x):
    half = 0.5 * x
    return half + half * jnp.tanh(half)


def _rms(x):
    return x * lax.rsqrt(jnp.mean(x * x, axis=-1, keepdims=True) + EPS)


def _round_up(x, m):
    return (x + m - 1) // m * m


def _mod_kernel(c_ref, w_ref, b_ref, o_ref):
    c = c_ref[...]
    o_ref[...] = _dot3(_silu(c_ref[...]), w_ref[...]) + b_ref[...]


def _modulation(c, w_ada, b_ada):
    rows, d = c.shape
    n = w_ada.shape[1]
    tn = d
    return pl.pallas_call(
        _mod_kernel,
        grid=(n // tn,),
        in_specs=[pl.BlockSpec((rows, d), lambda j: (0, 0)),
                  pl.BlockSpec((d, tn), lambda j: (0, j)),
                  pl.BlockSpec((1, tn), lambda j: (0, j))],
        out_specs=pl.BlockSpec((rows, tn), lambda j: (0, j)),
        out_shape=jax.ShapeDtypeStruct((rows, n), F32),
        compiler_params=_cparams(("arbitrary",)),
        name="adaln_modulation",
    )(c, w_ada, b_ada.reshape(1, n))


def _mod_rows(mod_ref, which, d, per_row, b):
    if per_row:
        return mod_ref[:, which * d:(which + 1) * d]
    return mod_ref[pl.ds(b, 1), which * d:(which + 1) * d]


def _in_proj_kernel(x_ref, mod_ref, g_ref, w_ref, *outs, per_row, tiles_per_seq, split):
    i = pl.program_id(0)
    d = x_ref.shape[1]
    b = i // tiles_per_seq
    sh = _mod_rows(mod_ref, 0, d, per_row, b)
    sc = _mod_rows(mod_ref, 1, d, per_row, b)
    h = (_rms(x_ref[...]) * g_ref[...] * (1.0 + sc) + sh).astype(BF16)
    for c in range(w_ref.shape[1] // d):
        acc = _dot(h, w_ref[:, c * d:(c + 1) * d])
        if not split:
            outs[0][:, c * d:(c + 1) * d] = acc
        elif c == 0:
            outs[0][...] = acc
        else:
            outs[1][:, (c - 1) * d:c * d] = acc.astype(BF16)


def _in_proj(x, mod, g, w_bf, *, tm, per_row, tiles_per_seq, split):
    n, d = x.shape
    width = w_bf.shape[1]
    kern = functools.partial(_in_proj_kernel, per_row=per_row, tiles_per_seq=tiles_per_seq, split=split)
    mod_spec = (pl.BlockSpec((tm, mod.shape[1]), lambda i: (i, 0)) if per_row
                else pl.BlockSpec(mod.shape, lambda i: (0, 0)))
    if split:
        out_specs = [pl.BlockSpec((tm, d), lambda i: (i, 0)),
                     pl.BlockSpec((tm, width - d), lambda i: (i, 0))]
        out_shape = [jax.ShapeDtypeStruct((n, d), F32), jax.ShapeDtypeStruct((n, width - d), BF16)]
    else:
        out_specs = pl.BlockSpec((tm, width), lambda i: (i, 0))
        out_shape = jax.ShapeDtypeStruct((n, width), F32)
    return pl.pallas_call(
        kern,
        grid=(n // tm,),
        in_specs=[pl.BlockSpec((tm, d), lambda i: (i, 0)),
                  mod_spec,
                  pl.BlockSpec((1, d), lambda i: (0, 0)),
                  pl.BlockSpec((d, width), lambda i: (0, 0), pipeline_mode=pl.Buffered(1))],
        out_specs=out_specs,
        out_shape=out_shape,
        compiler_params=_cparams(("arbitrary",)),
        name="in_proj_per_row" if per_row else "in_proj",
    )(x, mod, g.reshape(1, d), w_bf)


def _pool_prompt_kernel(u_ref, w_ref, s_ref, y_ref, tail_ref, prev_ref, *, tt):
    t = pl.program_id(1)
    nt = pl.num_programs(1)
    hist = prev_ref.shape[0]
    gw = w_ref.shape[1]

    @pl.when(t == 0)
    def _():
        prev_ref[...] = jnp.zeros_like(prev_ref)

    u = u_ref[...]
    ext = jnp.concatenate([prev_ref[...], u], axis=0)
    pos = t * tt + lax.broadcasted_iota(jnp.int32, (tt, 1), 0)
    outs = []
    for gi, w in enumerate(POOL_WINDOWS):
        cols = slice(gi * gw, (gi + 1) * gw)
        run = ext[:, cols]
        span = 1
        while span < w:
            run = run + pltpu.roll(run, span, 0)
            span *= 2
        inv_cnt = 1.0 / jnp.minimum(w, pos + 1).astype(F32)
        pooled = run[hist:, :] * inv_cnt - u[:, cols]
        outs.append(_dot(pooled.astype(BF16), w_ref[gi].astype(BF16)))
    y = jnp.concatenate(outs, axis=-1) * s_ref[...]
    y_ref[...] = y.astype(BF16)
    prev_ref[...] = u[tt - hist:, :]

    @pl.when(t == nt - 1)
    def _():
        tail_ref[0] = u[tt - hist:, :]


def _pool_prompt(proj, w_pool, pool_scale, *, batch, seq, tt):
    p = pool_scale.shape[0]
    hist = POOL_BUF + 1
    nt = seq // tt
    kern = functools.partial(_pool_prompt_kernel, tt=tt)
    return pl.pallas_call(
        kern,
        grid=(batch, nt),
        in_specs=[pl.BlockSpec((tt, p), lambda b, t: (b * nt + t, 0)),
                  pl.BlockSpec(w_pool.shape, lambda b, t: (0, 0, 0)),
                  pl.BlockSpec((1, p), lambda b, t: (0, 0))],
        out_specs=[pl.BlockSpec((tt, p), lambda b, t: (b * nt + t, 0)),
                   pl.BlockSpec((1, hist, p), lambda b, t: (b, 0, 0))],
        out_shape=[jax.ShapeDtypeStruct((batch * seq, p), BF16),
                   jax.ShapeDtypeStruct((batch, hist, p), F32)],
        scratch_shapes=[pltpu.VMEM((hist, p), F32)],
        compiler_params=_cparams(("arbitrary", "arbitrary")),
        name="pool_prompt",
    )(proj, w_pool, pool_scale.reshape(1, p))


def _pool_decode_kernel(st_ref, u_ref, w_ref, s_ref, y_ref, new_ref):
    buf = st_ref.shape[0]
    gw = w_ref.shape[1]
    u = u_ref[...]
    outs = []
    run = u
    used = 0
    for gi, w in enumerate(POOL_WINDOWS):
        while used < w - 1:
            run = run + st_ref[buf - 1 - used]
            used += 1
        cols = slice(gi * gw, (gi + 1) * gw)
        pooled = run[:, cols] / float(min(w, PAST_LEN + 1)) - u[:, cols]
        outs.append(_dot(pooled.astype(BF16), w_ref[gi].astype(BF16)))
    y_ref[...] = (jnp.concatenate(outs, axis=-1) * s_ref[...]).astype(BF16)
    for r in range(buf - 1):
        new_ref[r] = st_ref[r + 1]
    new_ref[buf - 1] = u


def _pool_decode(state_t, proj_s, w_pool, pool_scale):
    buf, n, p = state_t.shape
    return pl.pallas_call(
        _pool_decode_kernel,
        grid=(1,),
        in_specs=[pl.BlockSpec((buf, n, p), lambda i: (0, 0, 0)),
                  pl.BlockSpec((n, p), lambda i: (0, 0)),
                  pl.BlockSpec(w_pool.shape, lambda i: (0, 0, 0)),
                  pl.BlockSpec((1, p), lambda i: (0, 0))],
        out_specs=[pl.BlockSpec((n, p), lambda i: (0, 0)),
                   pl.BlockSpec((buf, n, p), lambda i: (0, 0, 0))],
        out_shape=[jax.ShapeDtypeStruct((n, p), BF16),
                   jax.ShapeDtypeStruct((buf, n, p), F32)],
        compiler_params=_cparams(("arbitrary",)),
        name="pool_decode",
    )(state_t, proj_s, w_pool, pool_scale.reshape(1, p))


def _log_gamma(h):
    return float(np.log(1.0 - 2.0 ** (-5.0 - h)))


def _rope_tables(pos, half):
    inv = np.power(ROPE_BASE, -np.arange(half, dtype=np.float64) / half)
    ang = np.asarray(pos, np.float64)[:, None] * inv[None, :]
    return jnp.asarray(np.cos(ang), F32), jnp.asarray(np.sin(ang), F32)


def _rotary(x, cos, sin):
    half = x.shape[-1] // 2
    x1, x2 = x[:, :half], x[:, half:]
    return jnp.concatenate([x1 * cos - x2 * sin, x2 * cos + x1 * sin], axis=-1)


def _ret_tables(c, key_scale):
    i = np.arange(c, dtype=np.float64)
    diff = i[:, None] - i[None, :]
    dec, qs, ks = [], [], []
    for h in range(RET_HEADS):
        lg = _log_gamma(h)
        dec.append(key_scale * np.where(diff >= 0, np.exp(lg * np.maximum(diff, 0.0)), 0.0))
        qs.append(np.broadcast_to(np.exp(lg * (i + 1.0))[:, None], (c, LANES)))
        ks.append(key_scale * np.broadcast_to(np.exp(lg * (c - 1.0 - i))[:, None], (c, LANES)))
    return (jnp.asarray(np.stack(dec), F32), jnp.asarray(np.stack(qs), F32), jnp.asarray(np.stack(ks), F32))


def _ret_prompt_kernel(q_ref, k_ref, v_ref, g_ref, cos_ref, sin_ref, dec_ref, qs_ref, ks_ref,
                       y_ref, s_out_ref, s_ref, *, chunk_decay):
    c = pl.program_id(1)
    nc = pl.num_programs(1)
    heads, dk, dv = s_ref.shape

    @pl.when(c == 0)
    def _():
        s_ref[...] = jnp.zeros_like(s_ref)

    cos, sin = cos_ref[...], sin_ref[...]
    reps = dk // LANES
    for h in range(heads):
        q = _rotary(q_ref[:, h * dk:(h + 1) * dk].astype(F32), cos, sin)
        k = _rotary(k_ref[:, h * dk:(h + 1) * dk].astype(F32), cos, sin)
        v = v_ref[:, h * dv:(h + 1) * dv]
        qs = jnp.concatenate([qs_ref[h]] * reps, axis=-1)
        ks = jnp.concatenate([ks_ref[h]] * reps, axis=-1)
        scores = lax.dot_general(q.astype(BF16), k.astype(BF16), (((1,), (1,)), ((), ())),
                                 preferred_element_type=F32) * dec_ref[h]
        state = s_ref[h]
        o = _dot(scores.astype(BF16), v) + _dot((q * qs).astype(BF16), state.astype(BF16))
        kv = lax.dot_general((k * ks).astype(BF16), v, (((0,), (0,)), ((), ())), preferred_element_type=F32)
        new_state = chunk_decay[h] * state + kv
        s_ref[h] = new_state
        g = g_ref[:, h * dv:(h + 1) * dv].astype(F32)
        y_ref[:, h * dv:(h + 1) * dv] = (_silu(g) * _rms(o)).astype(BF16)

        @pl.when(c == nc - 1)
        def _():
            s_out_ref[0, h] = new_state


def _ret_prompt(qkvg, *, batch, seq, d_model, chunk):
    heads = RET_HEADS
    dk = d_model // heads
    dv = 2 * d_model // heads
    nc = seq // chunk
    cos, sin = _rope_tables(np.arange(seq), dk // 2)
    key_scale = dk ** -0.5
    assert np.log2(key_scale) == np.round(np.log2(key_scale))
    dec, qs, ks = _ret_tables(chunk, key_scale)
    chunk_decay = tuple(float(np.exp(_log_gamma(h) * chunk)) for h in range(heads))
    kern = functools.partial(_ret_prompt_kernel, chunk_decay=chunk_decay)
    const = lambda a: pl.BlockSpec(a.shape, lambda b, c: (0,) * a.ndim)
    return pl.pallas_call(
        kern,
        grid=(batch, nc),
        in_specs=[pl.BlockSpec((chunk, d_model), lambda b, c: (b * nc + c, 0)),
                  pl.BlockSpec((chunk, d_model), lambda b, c: (b * nc + c, 1)),
                  pl.BlockSpec((chunk, 2 * d_model), lambda b, c: (b * nc + c, 1)),
                  pl.BlockSpec((chunk, 2 * d_model), lambda b, c: (b * nc + c, 2)),
                  pl.BlockSpec((chunk, dk // 2), lambda b, c: (c, 0)),
                  pl.BlockSpec((chunk, dk // 2), lambda b, c: (c, 0)),
                  const(dec), const(qs), const(ks)],
        out_specs=[pl.BlockSpec((chunk, heads * dv), lambda b, c: (b * nc + c, 0)),
                   pl.BlockSpec((1, heads, dk, dv), lambda b, c: (b, 0, 0, 0))],
        out_shape=[jax.ShapeDtypeStruct((batch * seq, heads * dv), BF16),
                   jax.ShapeDtypeStruct((batch, heads, dk, dv), F32)],
        scratch_shapes=[pltpu.VMEM((heads, dk, dv), F32)],
        compiler_params=_cparams(("arbitrary", "arbitrary")),
        name="retention_prompt",
    )(qkvg, qkvg, qkvg, qkvg, cos, sin, dec, qs, ks)


def _ret_decode_kernel(q_ref, k_ref, v_ref, g_ref, cos_ref, sin_ref, s_ref, y_ref, s_out_ref,
                       kt_ref, qd_ref, o_ref, *, gammas, bs):
    h = pl.program_id(0)
    blk = pl.program_id(1)
    nblk = pl.num_programs(1)
    n, dk = q_ref.shape
    gamma = jnp.where(h == 0, gammas[0], jnp.where(h == 1, gammas[1], jnp.where(h == 2, gammas[2], gammas[3])))

    @pl.when(blk == 0)
    def _():
        cos, sin = cos_ref[...], sin_ref[...]
        q = _rotary(q_ref[...], cos, sin)
        k = _rotary(k_ref[...], cos, sin) * (dk ** -0.5)
        kt_ref[...] = k.T
        qd_ref[...] = (q * gamma).astype(BF16)
        o_ref[...] = jnp.sum(q * k, axis=-1, keepdims=True) * v_ref[...]

    vb = v_ref[...].astype(BF16)
    rows = lax.broadcasted_iota(jnp.int32, (n, 1), 0)
    lanes = lax.broadcasted_iota(jnp.int32, (1, n), 1)
    for i in range(bs):
        tok = blk * bs + i
        state = s_ref[i, 0]
        o_ref[...] += jnp.where(rows == tok, _dot(qd_ref[...], state.astype(BF16)), 0.0)
        k_col = jnp.where(lanes == tok, kt_ref[...], 0.0).astype(BF16)
        s_out_ref[i, 0] = gamma * state + _dot(k_col, vb)

    @pl.when(blk == nblk - 1)
    def _():
        g = g_ref[...]
        y_ref[...] = (_silu(g) * _rms(o_ref[...])).astype(BF16)


def _ret_decode(proj_s, state, *, d_model, bs):
    n = proj_s.shape[0]
    heads = RET_HEADS
    dk = d_model // heads
    dv = 2 * d_model // heads
    cos, sin = _rope_tables(np.array([PAST_LEN]), dk // 2)
    gammas = tuple(float(np.exp(_log_gamma(h))) for h in range(heads))
    q0, k0, v0, g0 = d_model // dk, 2 * d_model // dk, 3 * d_model // dv, 5 * d_model // dv
    kern = functools.partial(_ret_decode_kernel, gammas=gammas, bs=bs)
    return pl.pallas_call(
        kern,
        grid=(heads, n // bs),
        in_specs=[pl.BlockSpec((n, dk), lambda h, b: (0, q0 + h)),
                  pl.BlockSpec((n, dk), lambda h, b: (0, k0 + h)),
                  pl.BlockSpec((n, dv), lambda h, b: (0, v0 + h)),
                  pl.BlockSpec((n, dv), lambda h, b: (0, g0 + h)),
                  pl.BlockSpec((1, dk // 2), lambda h, b: (0, 0)),
                  pl.BlockSpec((1, dk // 2), lambda h, b: (0, 0)),
                  pl.BlockSpec((bs, 1, dk, dv), lambda h, b: (b, h, 0, 0))],
        out_specs=[pl.BlockSpec((n, dv), lambda h, b: (0, h)),
                   pl.BlockSpec((bs, 1, dk, dv), lambda h, b: (b, h, 0, 0))],
        out_shape=[jax.ShapeDtypeStruct((n, heads * dv), BF16),
                   jax.ShapeDtypeStruct(state.shape, F32)],
        scratch_shapes=[pltpu.VMEM((dk, n), F32), pltpu.VMEM((n, dk), BF16), pltpu.VMEM((n, dv), F32)],
        compiler_params=_cparams(("arbitrary", "arbitrary")),
        name="retention_decode",
    )(proj_s, proj_s, proj_s, proj_s, cos, sin, state)


def _mix_kernel(x_ref, ya_ref, yb_ref, ga_ref, gb_ref, mod_ref, g2_ref, wa_ref, wb_ref, wo_ref,
                wr_ref, br_ref, x1_ref, h2_ref, route_ref, *, per_row, tiles_per_seq, n_experts):
    i = pl.program_id(0)
    tm, d = x_ref.shape
    b = i // tiles_per_seq

    merged = (_sigmoid(ga_ref[...].astype(F32)) * _dot(ya_ref[...], wa_ref[...])
              + _sigmoid(gb_ref[...].astype(F32)) * _dot(yb_ref[...], wb_ref[...]))
    mix = _dot(merged.astype(BF16), wo_ref[...])
    x1 = x_ref[...] + _mod_rows(mod_ref, 2, d, per_row, b) * mix
    x1_ref[...] = x1
    h2 = (_rms(x1) * g2_ref[...] * (1.0 + _mod_rows(mod_ref, 4, d, per_row, b))
          + _mod_rows(mod_ref, 3, d, per_row, b))
    h2_ref[...] = h2.astype(BF16)

    logits_t = _dot3_nt(wr_ref[...], h2) + br_ref[:, 0:1]
    row = lax.broadcasted_iota(jnp.int32, (n_experts, tm), 0)
    neg = jnp.float32(-jnp.inf)
    work = logits_t
    vals, idxs = [], []
    for _ in range(TOP_K):
        m = jnp.max(work, axis=0, keepdims=True)
        idx = jnp.min(jnp.where(work == m, row, n_experts), axis=0, keepdims=True)
        vals.append(m)
        idxs.append(idx)
        work = jnp.where(row == idx, neg, work)
    exps = [jnp.exp(v - vals[0]) for v in vals]
    denom = exps[0] + exps[1] + exps[2] + exps[3]
    route_ref[...] = jnp.concatenate([idx.astype(F32) for idx in idxs] + [e / denom for e in exps]
                                     + [jnp.zeros((LANES - 2 * TOP_K, tm), F32)], axis=0)


def _mix(x, ya, yb, proj, mod, g2, wa_bf, wb_bf, wo_bf, wr_pad, br_pad, *, tm, per_row, tiles_per_seq,
         n_experts, ga_blk):
    n, d = x.shape
    gb_blk = ga_blk + 1
    kern = functools.partial(_mix_kernel, per_row=per_row, tiles_per_seq=tiles_per_seq, n_experts=n_experts)
    mod_spec = (pl.BlockSpec((tm, mod.shape[1]), lambda i: (i, 0)) if per_row
                else pl.BlockSpec(mod.shape, lambda i: (0, 0)))
    full = lambda a: pl.BlockSpec(a.shape, lambda i: (0,) * a.ndim)
    return pl.pallas_call(
        kern,
        grid=(n // tm,),
        in_specs=[pl.BlockSpec((tm, d), lambda i: (i, 0)),
                  pl.BlockSpec((tm, ya.shape[1]), lambda i: (i, 0)),
                  pl.BlockSpec((tm, yb.shape[1]), lambda i: (i, 0)),
                  pl.BlockSpec((tm, d), lambda i: (i, ga_blk)),
                  pl.BlockSpec((tm, d), lambda i: (i, gb_blk)),
                  mod_spec,
                  pl.BlockSpec((1, d), lambda i: (0, 0)),
                  full(wa_bf), full(wb_bf), full(wo_bf), full(wr_pad), full(br_pad)],
        out_specs=[pl.BlockSpec((tm, d), lambda i: (i, 0)),
                   pl.BlockSpec((tm, d), lambda i: (i, 0)),
                   pl.BlockSpec((LANES, tm), lambda i: (0, i))],
        out_shape=[jax.ShapeDtypeStruct((n, d), F32),
                   jax.ShapeDtypeStruct((n, d), BF16),
                   jax.ShapeDtypeStruct((LANES, n), F32)],
        compiler_params=_cparams(("arbitrary",)),
        name="mix_route_per_row" if per_row else "mix_route",
    )(x, ya, yb, proj, proj, mod, g2.reshape(1, d), wa_bf, wb_bf, wo_bf, wr_pad, br_pad)


def _sorted_rows(tt, n_experts):
    return _round_up(tt * TOP_K + n_experts * (ROW_ALIGN - 1) + ROW_ALIGN, SORT_BLOCK)


def _dispatch_kernel(hp_ref, rp_ref, hd_ref, rd_ref, before_ref, below_ref, xs_ref, r2_ref, cnt_ref, *, lmax):
    tt = hp_ref.shape[0]
    n_e = cnt_ref.shape[0]
    is_dec = pl.program_id(0) == pl.num_programs(0) - 1
    route_t = jnp.where(is_dec, rd_ref[...], rp_ref[...])
    expert = lax.broadcasted_iota(jnp.int32, (n_e, tt), 0).astype(F32)
    hits = [expert == route_t[kk:kk + 1, :] for kk in range(TOP_K)]
    sel = jnp.zeros((n_e, tt), F32)
    for hit in hits:
        sel = sel + jnp.where(hit, 1.0, 0.0)
    cnt = jnp.sum(sel, axis=1, keepdims=True)
    seg = jnp.broadcast_to(jnp.ceil(cnt / ROW_ALIGN) * ROW_ALIGN, (n_e, LANES))
    off = _dot3(below_ref[...], seg)[:, 0:1]
    place = _dot(sel.astype(BF16), before_ref[...]) + off
    w_hi, w_lo, pos_a, pos_b, pos_all = [], [], [], [], []
    for kk in range(TOP_K):
        found = jnp.sum(jnp.where(hits[kk], 1.0, 0.0), axis=0, keepdims=True)
        pos = jnp.where(found > 0.0, jnp.sum(jnp.where(hits[kk], place, 0.0), axis=0, keepdims=True), -1.0)
        w = route_t[TOP_K + kk:TOP_K + kk + 1, :]
        hi = w.astype(BF16).astype(F32)
        a = jnp.floor(pos / POS_SPLIT)
        w_hi.append(hi)
        w_lo.append(w - hi)
        pos_a.append(a)
        pos_b.append(pos - POS_SPLIT * a)
        pos_all.append(pos)
    info_t = jnp.concatenate(w_hi + w_lo + pos_a + pos_b + pos_all
                             + [jnp.zeros((LANES - 5 * TOP_K, tt), F32)], axis=0)
    side = info_t.T
    r2_ref[...] = side
    cnt_ref[...] = jnp.broadcast_to(cnt, cnt_ref.shape)
    r2t = info_t.astype(jnp.int32).astype(jnp.int16)
    h = jnp.concatenate([jnp.where(is_dec, hd_ref[...], hp_ref[...]), side.astype(BF16)], axis=-1)
    d = hp_ref.shape[1]
    out_lane = lax.broadcasted_iota(jnp.int32, (SORT_BLOCK, LANES), 1)
    one = jnp.ones((SORT_BLOCK, tt), BF16)
    zero = jnp.zeros((SORT_BLOCK, tt), BF16)
    for rb in range(lmax // SORT_BLOCK):
        rio = (lax.broadcasted_iota(jnp.int32, (SORT_BLOCK, tt), 0) + rb * SORT_BLOCK).astype(jnp.int16)
        hit = rio == r2t[POS_LANE:POS_LANE + 1, :]
        for kk in range(1, TOP_K):
            hit = hit | (rio == r2t[POS_LANE + kk:POS_LANE + kk + 1, :])
        perm = jnp.where(hit, one, zero)
        res = _dot(perm, h)
        info = res[:, d:]
        row = (lax.broadcasted_iota(jnp.int32, (SORT_BLOCK, LANES), 0) + rb * SORT_BLOCK).astype(F32)
        pos = (POS_SPLIT * pltpu.roll(info, LANES - 2 * TOP_K, 1) + pltpu.roll(info, LANES - 3 * TOP_K, 1))
        mine = jnp.where((pos == row) & (out_lane < TOP_K), 1.0, 0.0)
        mine = mine + pltpu.roll(mine, TOP_K, 1)
        rows = slice(rb * SORT_BLOCK, (rb + 1) * SORT_BLOCK)
        xs_ref[rows, :d] = res[:, :d].astype(BF16)
        xs_ref[rows, d:] = (info * mine).astype(BF16)


def _dispatch(h2_p, route_p, h2_d, route_d, *, tt, lmax, n_experts):
    n, d = h2_p.shape
    p_tiles = n // tt
    tiles = p_tiles + 1
    pad = tt - h2_d.shape[0]
    h2_d = jnp.pad(h2_d, ((0, pad), (0, 0)))
    route_d = jnp.pad(route_d, ((0, 0), (0, pad)), constant_values=-1.0)
    kern = functools.partial(_dispatch_kernel, lmax=lmax)
    before = jnp.asarray(np.triu(np.ones((tt, tt), np.float32), 1), BF16)
    below = jnp.asarray(np.tril(np.ones((n_experts, n_experts), np.float32), -1), F32)
    return pl.pallas_call(
        kern,
        grid=(tiles,),
        in_specs=[pl.BlockSpec((tt, d), lambda i: (jnp.minimum(i, p_tiles - 1), 0)),
                  pl.BlockSpec((LANES, tt), lambda i: (0, jnp.minimum(i, p_tiles - 1))),
                  pl.BlockSpec((tt, d), lambda i: (0, 0)),
                  pl.BlockSpec((LANES, tt), lambda i: (0, 0)),
                  pl.BlockSpec((tt, tt), lambda i: (0, 0)),
                  pl.BlockSpec((n_experts, n_experts), lambda i: (0, 0))],
        out_specs=[pl.BlockSpec((lmax, d + LANES), lambda i: (i, 0)),
                   pl.BlockSpec((tt, LANES), lambda i: (i, 0)),
                   pl.BlockSpec((n_experts, LANES), lambda i: (i, 0))],
        out_shape=[jax.ShapeDtypeStruct((tiles * lmax, d + LANES), BF16),
                   jax.ShapeDtypeStruct((tiles * tt, LANES), F32),
                   jax.ShapeDtypeStruct((tiles * n_experts, LANES), F32)],
        compiler_params=_cparams(("arbitrary",)),
        name="dispatch_sort",
    )(h2_p, route_p, h2_d, route_d, before, below)


def _tables_kernel(cnt_ref, te_ref, na_ref, nxt_ref, par_ref, msrc_ref, csrc_ref,
                   before_ref, start_ref, len_ref, first_ref, end_ref, total_ref, *, tiles, n_e, cpt, cstride,
                   lmax, n_tiles, zero_row):
    def nblk(i, e):
        return len_ref[i * n_e + e]

    def per_tile(i, carry):
        def per_expert(e, run):
            blocks = lax.shift_right_logical(cnt_ref[i * n_e + e] + (ROW_ALIGN - 1), ROW_ALIGN.bit_length() - 1)
            len_ref[i * n_e + e] = blocks
            start_ref[i * n_e + e] = run
            return run + blocks
        lax.fori_loop(0, n_e, per_expert, 0)
        return carry

    lax.fori_loop(0, tiles, per_tile, 0)

    def per_expert(e, carry):
        def per_tile(i, run):
            before_ref[i * n_e + e] = run
            return run + nblk(i, e)
        total_ref[e] = lax.fori_loop(0, tiles, per_tile, 0)
        return carry

    lax.fori_loop(0, n_e, per_expert, 0)

    def expert_tiles(e, carry):
        t_first, run = carry
        t_last = t_first + lax.shift_right_logical(total_ref[e] + (cpt - 1), cpt.bit_length() - 1)
        first_ref[e] = t_first
        end_ref[e] = t_last

        def mark(t, carry):
            te_ref[t] = e
            par_ref[t] = run % 2
            return carry

        lax.fori_loop(t_first, t_last, mark, 0)
        return t_last, jnp.where(t_last > t_first, run + 1, run)

    n_act, _ = lax.fori_loop(0, n_e, expert_tiles, (jnp.int32(0), jnp.int32(0)))
    na_ref[0] = n_act
    last = te_ref[jnp.maximum(n_act - 1, 0)]

    def mark_rest(t, carry):
        te_ref[t] = last
        par_ref[t] = 0
        nxt_ref[t] = -1
        return carry

    lax.fori_loop(n_act, n_tiles, mark_rest, 0)

    def mark_next(s, nxt):
        t = n_act - 1 - s
        after = te_ref[jnp.minimum(t + 1, n_tiles - 1)]
        nxt = jnp.where((t + 1 < n_act) & (after != te_ref[t]), after, nxt)
        nxt_ref[t] = nxt
        return nxt

    lax.fori_loop(0, n_act, mark_next, jnp.int32(-1))

    shape = msrc_ref.shape
    n_idx = (lax.broadcasted_iota(jnp.int32, shape, 0) * LANES + lax.broadcasted_iota(jnp.int32, shape, 1))
    m_tile = lax.shift_right_logical(n_idx, cpt.bit_length() - 1)
    expert = jnp.zeros(shape, jnp.int32)
    for e in range(n_e):
        expert = expert + (end_ref[e] <= m_tile).astype(jnp.int32)
    t_first = jnp.zeros(shape, jnp.int32)
    total = jnp.zeros(shape, jnp.int32)
    for e in range(n_e):
        t_first = jnp.where(expert == e, first_ref[e], t_first)
        total = jnp.where(expert == e, total_ref[e], total)
    g = (m_tile - t_first) * cpt + (n_idx & (cpt - 1))
    src = jnp.full(shape, zero_row, jnp.int32)
    for i in range(tiles):
        seg_first = jnp.zeros(shape, jnp.int32)
        seg_len = jnp.zeros(shape, jnp.int32)
        seg_local = jnp.zeros(shape, jnp.int32)
        for e in range(n_e):
            mine = expert == e
            seg_first = jnp.where(mine, before_ref[i * n_e + e], seg_first)
            seg_len = jnp.where(mine, nblk(i, e), seg_len)
            seg_local = jnp.where(mine, start_ref[i * n_e + e], seg_local)
        inside = (g >= seg_first) & (g < seg_first + seg_len) & (g < total) & (m_tile < n_act)
        src = jnp.where(inside, i * lmax + (seg_local + g - seg_first) * ROW_ALIGN, src)
    msrc_ref[...] = src

    rows_per_tile = cstride // LANES
    c_idx = (lax.broadcasted_iota(jnp.int32, (rows_per_tile, LANES), 0) * LANES
             + lax.broadcasted_iota(jnp.int32, (rows_per_tile, LANES), 1))
    for i in range(tiles):
        owner = jnp.zeros((rows_per_tile, LANES), jnp.int32)
        for e in range(n_e):
            owner = owner + ((start_ref[i * n_e + e] + nblk(i, e)) <= c_idx).astype(jnp.int32)
        base = jnp.zeros((rows_per_tile, LANES), jnp.int32)
        for e in range(n_e):
            base = jnp.where(owner == e, first_ref[e] * cpt + before_ref[i * n_e + e] - start_ref[i * n_e + e],
                             base)
        csrc_ref[i * rows_per_tile:(i + 1) * rows_per_tile, :] = jnp.where(owner < n_e,
                                                                         (base + c_idx) * ROW_ALIGN, 0)


def _routing_tables(cnt_flat, *, tiles, n_e, tmo, n_tiles, cstride, lmax):
    cpt = tmo // ROW_ALIGN
    assert (n_tiles * cpt) % LANES == 0 and cstride % LANES == 0
    kern = functools.partial(_tables_kernel, tiles=tiles, n_e=n_e, cpt=cpt, cstride=cstride, lmax=lmax,
                             n_tiles=n_tiles, zero_row=lmax - ROW_ALIGN)
    smem = pl.BlockSpec(memory_space=pltpu.SMEM)
    vmem = pl.BlockSpec(memory_space=pltpu.VMEM)
    te, na, nxt, par, msrc, csrc = pl.pallas_call(
        kern,
        in_specs=[smem],
        out_specs=[smem, smem, smem, smem, vmem, vmem],
        out_shape=[jax.ShapeDtypeStruct((n_tiles,), jnp.int32),
                   jax.ShapeDtypeStruct((1,), jnp.int32),
                   jax.ShapeDtypeStruct((n_tiles,), jnp.int32),
                   jax.ShapeDtypeStruct((n_tiles,), jnp.int32),
                   jax.ShapeDtypeStruct((n_tiles * cpt // LANES, LANES), jnp.int32),
                   jax.ShapeDtypeStruct((tiles * cstride // LANES, LANES), jnp.int32)],
        scratch_shapes=[pltpu.SMEM((tiles * n_e,), jnp.int32), pltpu.SMEM((tiles * n_e,), jnp.int32),
                        pltpu.SMEM((tiles * n_e,), jnp.int32),
                        pltpu.SMEM((n_e,), jnp.int32), pltpu.SMEM((n_e,), jnp.int32),
                        pltpu.SMEM((n_e,), jnp.int32)],
        name="routing_tables",
    )(cnt_flat)
    return te, na, msrc.reshape(-1), csrc.reshape(-1), nxt, par


def _moe_kernel(tile_expert_ref, n_active_ref, src_ref, nxt_ref, par_ref, xs_hbm, wi_hbm, bi_ref, wo_hbm, bo_ref,
                y_ref, xbuf, wi_f32, wo_f32, wi_bf, wo_bf, sem, wsem):
    j = pl.program_id(0)
    tm = xbuf.shape[1]
    d = y_ref.shape[1]
    f = wo_bf.shape[0]
    cpt = tm // ROW_ALIGN
    slot = j % 2
    n_active = n_active_ref[0]
    expert = tile_expert_ref[j]

    def start_tile(tile, s):
        for c in range(cpt):
            row = pl.multiple_of(src_ref[tile * cpt + c], ROW_ALIGN)
            pltpu.make_async_copy(xs_hbm.at[pl.ds(row, ROW_ALIGN), :],
                                  xbuf.at[s, pl.ds(c * ROW_ALIGN, ROW_ALIGN), :], sem.at[s]).start()

    def wait_tile(s):
        pltpu.make_async_copy(xs_hbm.at[pl.ds(0, tm), :], xbuf.at[s], sem.at[s]).wait()

    def weight_copies(e, s):
        return (pltpu.make_async_copy(wi_hbm.at[e], wi_f32.at[s], wsem.at[0, s]),
                pltpu.make_async_copy(wo_hbm.at[e], wo_f32.at[s], wsem.at[1, s]))

    @pl.when(j == 0)
    def _():
        start_tile(0, 0)
        for cp in weight_copies(expert, par_ref[0]):
            cp.start()

    @pl.when(j + 1 < n_active)
    def _():
        start_tile(j + 1, 1 - slot)

    @pl.when(j < n_active)
    def _():
        @pl.when((j == 0) | (expert != tile_expert_ref[jnp.maximum(j - 1, 0)]))
        def _():
            ws = par_ref[j]
            for cp in weight_copies(expert, ws):
                cp.wait()

            @pl.when(nxt_ref[j] >= 0)
            def _():
                for cp in weight_copies(nxt_ref[j], 1 - ws):
                    cp.start()

            wi_bf[...] = wi_f32[ws].astype(BF16)
            wo_bf[...] = wo_f32[ws].astype(BF16)

        wait_tile(slot)
        rows = xbuf[slot]
        hh = _dot(rows[:, :d], wi_bf[...]) + bi_ref[0]
        gate = jnp.minimum(hh[:, :f], SWIGLU_LIMIT)
        lin = jnp.clip(hh[:, f:], -SWIGLU_LIMIT, SWIGLU_LIMIT)
        half_gate = 0.5 * gate
        act = (lin + 1.0) * (half_gate + half_gate * jnp.tanh((0.5 * SWIGLU_ALPHA) * gate))
        weight = jnp.sum(rows[:, d:].astype(F32), axis=-1, keepdims=True)
        y_ref[...] = ((_dot(act.astype(BF16), wo_bf[...]) + bo_ref[0]) * weight).astype(BF16)

    @pl.when(j >= n_active)
    def _():
        y_ref[...] = jnp.zeros_like(y_ref)


def _moe(xs, tables, w_in, b_in, w_out, b_out, *, tm, n_tiles):
    tile_expert, n_active, moe_src, _, nxt, par = tables
    e, d, f2 = w_in.shape
    f = w_out.shape[1]
    bias = lambda width: pl.BlockSpec((1, 1, width), lambda j, te, *_: (te[j], 0, 0))
    grid_spec = pltpu.PrefetchScalarGridSpec(
        num_scalar_prefetch=5,
        grid=(n_tiles,),
        in_specs=[pl.BlockSpec(memory_space=pl.ANY),
                  pl.BlockSpec(memory_space=pl.ANY),
                  bias(f2),
                  pl.BlockSpec(memory_space=pl.ANY),
                  bias(d)],
        out_specs=pl.BlockSpec((tm, d), lambda j, *_: (j, 0)),
        scratch_shapes=[pltpu.VMEM((2, tm, xs.shape[1]), BF16),
                        pltpu.VMEM((2, d, f2), F32),
                        pltpu.VMEM((2, f, d), F32),
                        pltpu.VMEM((d, f2), BF16),
                        pltpu.VMEM((f, d), BF16),
                        pltpu.SemaphoreType.DMA((2,)),
                        pltpu.SemaphoreType.DMA((2, 2))],
    )
    return pl.pallas_call(
        _moe_kernel,
        grid_spec=grid_spec,
        out_shape=jax.ShapeDtypeStruct((n_tiles * tm, d), BF16),
        compiler_params=_cparams(("arbitrary",)),
        name="moe_experts",
    )(tile_expert, n_active, moe_src, nxt, par, xs, w_in, b_in.reshape(e, 1, f2), w_out, b_out.reshape(e, 1, d))


def _combine_kernel(src_ref, x1_ref, r2_ref, mod_ref, gf_ref, y_hbm, o_ref, ybuf, sem,
                    *, per_row, tiles_per_seq, tile0, stride, single):
    i = pl.program_id(0)
    nt = pl.num_programs(0)
    tt, d = x1_ref.shape
    lmax = ybuf.shape[1]
    nch = lmax // ROW_ALIGN
    slot = i % 2
    b = i // tiles_per_seq

    def start_tile(tile, s):
        for c in range(nch):
            row = pl.multiple_of(src_ref[(tile0 + tile) * stride + c], ROW_ALIGN)
            pltpu.make_async_copy(y_hbm.at[pl.ds(row, ROW_ALIGN), :],
                                  ybuf.at[s, pl.ds(c * ROW_ALIGN, ROW_ALIGN), :], sem.at[s]).start()

    def wait_tile(s):
        pltpu.make_async_copy(y_hbm.at[pl.ds(0, lmax), :], ybuf.at[s], sem.at[s]).wait()

    @pl.when(i == 0)
    def _():
        start_tile(0, 0)

    if not single:
        start_tile(jnp.minimum(i + 1, nt - 1), 1 - slot)
    wait_tile(slot)
    r2 = r2_ref[...].astype(jnp.int32).astype(jnp.int16)
    one = jnp.ones((tt, SORT_BLOCK), BF16)
    zero = jnp.zeros((tt, SORT_BLOCK), BF16)
    moe = jnp.zeros((tt, d), F32)
    for cb in range(lmax // SORT_BLOCK):
        col = (lax.broadcasted_iota(jnp.int32, (tt, SORT_BLOCK), 1) + cb * SORT_BLOCK).astype(jnp.int16)
        hit = col == r2[:, POS_LANE:POS_LANE + 1]
        for kk in range(1, TOP_K):
            hit = hit | (col == r2[:, POS_LANE + kk:POS_LANE + kk + 1])
        moe = moe + _dot(jnp.where(hit, one, zero), ybuf[slot, cb * SORT_BLOCK:(cb + 1) * SORT_BLOCK, :])
    x2 = x1_ref[...] + _mod_rows(mod_ref, 5, d, per_row, b) * moe
    o_ref[...] = _rms(x2) * gf_ref[...]

    if not single:
        @pl.when(i == nt - 1)
        def _():
            wait_tile(1 - slot)


def _combine(comb_src, x1, r2, mod, gf, ys, *, tt, lmax, per_row, tiles_per_seq, tile0, stride, r2_block0):
    n, d = x1.shape
    kern = functools.partial(_combine_kernel, per_row=per_row, tiles_per_seq=tiles_per_seq, tile0=tile0,
                             stride=stride, single=(n // tt == 1))
    mod_spec = (pl.BlockSpec((tt, mod.shape[1]), lambda i, s: (i, 0)) if per_row
                else pl.BlockSpec(mod.shape, lambda i, s: (0, 0)))
    grid_spec = pltpu.PrefetchScalarGridSpec(
        num_scalar_prefetch=1,
        grid=(n // tt,),
        in_specs=[pl.BlockSpec((tt, d), lambda i, s: (i, 0)),
                  pl.BlockSpec((tt, LANES), lambda i, s: (r2_block0 + i, 0)),
                  mod_spec,
                  pl.BlockSpec((1, d), lambda i, s: (0, 0)),
                  pl.BlockSpec(memory_space=pl.ANY)],
        out_specs=pl.BlockSpec((tt, d), lambda i, s: (i, 0)),
        scratch_shapes=[pltpu.VMEM((2, lmax, d), BF16), pltpu.SemaphoreType.DMA((2,))],
    )
    return pl.pallas_call(
        kern,
        grid_spec=grid_spec,
        out_shape=jax.ShapeDtypeStruct((n, d), F32),
        compiler_params=_cparams(("arbitrary",)),
        name="combine_final_per_row" if per_row else "combine_final",
    )(comb_src, x1, r2, mod, gf.reshape(1, d), ys)


def kernel(x_prompt, x_sample, c_prompt, c_sample, state_pool, state_ret, w_ada, b_ada, norm1_g, w_in,
           w_pool, pool_scale, w_up_pool, w_up_ret, w_out, norm2_g, w_router, b_router, w_expert_in,
           b_expert_in, w_expert_out, b_expert_out, final_norm_g):
    batch, seq, d = x_prompt.shape
    n_dec = x_sample.shape[0]
    n_prompt = batch * seq
    n_experts = w_router.shape[-1]
    xp = x_prompt.reshape(n_prompt, d)
    xs = x_sample.reshape(n_dec, d)

    w_in_bf = w_in[0].astype(BF16)
    wa_bf = w_up_pool[0].astype(BF16)
    wb_bf = w_up_ret[0].astype(BF16)
    wo_bf = w_out[0].astype(BF16)
    assert n_experts % 8 == 0
    wr_pad = w_router[0].T
    br_pad = jnp.broadcast_to(b_router[0][:, None], (n_experts, LANES))

    c_all = jnp.concatenate([c_prompt, c_sample], axis=0)
    c_all = jnp.pad(c_all, ((0, -c_all.shape[0] % 16), (0, 0)))
    mod = _modulation(c_all, w_ada[0], b_ada[0])
    mod_p, mod_s = mod[:batch], mod[batch:batch + n_dec]

    tm = min(IN_PROJ_ROWS, seq)
    u_p, qkvg_p = _in_proj(xp, mod_p, norm1_g[0], w_in_bf, tm=tm, per_row=False, tiles_per_seq=seq // tm,
                           split=True)
    proj_s = _in_proj(xs, mod_s, norm1_g[0], w_in_bf, tm=n_dec, per_row=True, tiles_per_seq=1, split=False)

    ya_p, tail_p = _pool_prompt(u_p, w_pool[0], pool_scale[0], batch=batch, seq=seq, tt=min(POOL_ROWS, seq))
    ya_s, pool_s_t = _pool_decode(jnp.transpose(state_pool[0], (1, 0, 2)), proj_s, w_pool[0], pool_scale[0])
    yb_p, ret_p = _ret_prompt(qkvg_p, batch=batch, seq=seq, d_model=d, chunk=min(RET_CHUNK, seq))
    yb_s, ret_s = _ret_decode(proj_s, state_ret[0], d_model=d, bs=min(RET_DECODE_SEQS, n_dec))

    tmix = min(MIX_ROWS, seq)
    x1_p, h2_p, route_p = _mix(xp, ya_p, yb_p, qkvg_p, mod_p, norm2_g[0], wa_bf, wb_bf, wo_bf, wr_pad, br_pad,
                               tm=tmix, per_row=False, tiles_per_seq=seq // tmix, n_experts=n_experts,
                               ga_blk=6)
    x1_s, h2_s, route_s = _mix(xs, ya_s, yb_s, proj_s, mod_s, norm2_g[0], wa_bf, wb_bf, wo_bf, wr_pad, br_pad,
                               tm=n_dec, per_row=True, tiles_per_seq=1, n_experts=n_experts, ga_blk=7)

    tt = min(TOKEN_TILE, seq)
    assert n_dec <= tt
    p_tiles = n_prompt // tt
    lmax = _sorted_rows(tt, n_experts)
    xs_buf, r2, cnt_t = _dispatch(h2_p, route_p, h2_s, route_s, tt=tt, lmax=lmax, n_experts=n_experts)

    tmo = MOE_ROW_TILE
    cnt = cnt_t[:, 0].astype(jnp.int32)
    max_rows = (n_prompt + n_dec) * TOP_K + (p_tiles + 1) * n_experts * (ROW_ALIGN - 1)
    n_tiles = _round_up(max_rows // tmo + n_experts, LANES * ROW_ALIGN // tmo)
    stride = _round_up(lmax // ROW_ALIGN, LANES)
    tables = _routing_tables(cnt, tiles=p_tiles + 1, n_e=n_experts, tmo=tmo, n_tiles=n_tiles, cstride=stride,
                             lmax=lmax)
    comb_src = tables[3]

    ys = _moe(xs_buf, tables, w_expert_in[0], b_expert_in[0], w_expert_out[0], b_expert_out[0], tm=tmo,
              n_tiles=n_tiles)

    assert n_prompt % n_dec == 0
    y_p = _combine(comb_src, x1_p, r2, mod_p, final_norm_g, ys, tt=tt, lmax=lmax, per_row=False,
                   tiles_per_seq=seq // tt, tile0=0, stride=stride, r2_block0=0)
    y_s = _combine(comb_src, x1_s, r2, mod_s, final_norm_g, ys, tt=n_dec, lmax=min(lmax, _sorted_rows(n_dec, n_experts)),
                   per_row=True, tiles_per_seq=1, tile0=p_tiles, stride=stride, r2_block0=n_prompt // n_dec)

    return (y_p.reshape(batch, seq, d),
            y_s.reshape(n_dec, 1, d),
            tail_p[None, :, 1:, :],
            ret_p[None],
            jnp.transpose(pool_s_t, (1, 0, 2))[None],
            ret_s[None])
```

```python
import functools

import numpy as np
import jax
import jax.numpy as jnp
from jax import lax
from jax.experimental import pallas as pl
from jax.experimental.pallas import tpu as pltpu

F32 = jnp.float32
BF16 = jnp.bfloat16

POOL_WINDOWS = (2, 4, 8, 16)
POOL_BUF = max(POOL_WINDOWS) - 1
RET_HEADS = 4
RET_CHUNK = 256
RET_STEP_CHUNKS = 4
DISPATCH_STEP_TILES = 2
ROPE_BASE = 10000.0
PAST_LEN = 16384
TOP_K = 4
SWIGLU_LIMIT = 7.0
SWIGLU_ALPHA = 1.702
EPS = 1e-6
LANES = 128
ROW_ALIGN = 16
SORT_BLOCK = 512
POS_SPLIT = 64.0
POS_LANE = 16
TOKEN_TILE = 512
MOE_ROW_TILE = 512
IN_PROJ_ROWS = 512
MIX_ROWS = 512
POOL_ROWS = 1024
RET_DECODE_SEQS = 16
VMEM_LIMIT = 56 * 1024 * 1024


def _cparams(sem):
    return pltpu.CompilerParams(dimension_semantics=sem, vmem_limit_bytes=VMEM_LIMIT)


def _dot(a, b):
    return jnp.dot(a, b, preferred_element_type=F32)


def _dot3(a, b):
    a_hi = a.astype(BF16)
    a_lo = (a - a_hi.astype(F32)).astype(BF16)
    b_hi = b.astype(BF16)
    b_lo = (b - b_hi.astype(F32)).astype(BF16)
    return _dot(a_hi, b_hi) + _dot(a_hi, b_lo) + _dot(a_lo, b_hi)


def _dot3_nt(a, b):
    nt = lambda x, y: lax.dot_general(x, y, (((1,), (1,)), ((), ())), preferred_element_type=F32)
    a_hi = a.astype(BF16)
    a_lo = (a - a_hi.astype(F32)).astype(BF16)
    b_hi = b.astype(BF16)
    b_lo = (b - b_hi.astype(F32)).astype(BF16)
    return nt(a_hi, b_hi) + nt(a_hi, b_lo) + nt(a_lo, b_hi)


def _sigmoid(x):
    return 0.5 + 0.5 * jnp.tanh(0.5 * x)


def _silu(x):
    half = 0.5 * x
    return half + half * jnp.tanh(half)


def _rms(x):
    return x * lax.rsqrt(jnp.mean(x * x, axis=-1, keepdims=True) + EPS)


def _round_up(x, m):
    return (x + m - 1) // m * m


def _mod_kernel(c_ref, w_ref, b_ref, o_ref):
    c = c_ref[...]
    o_ref[...] = _dot3(_silu(c_ref[...]), w_ref[...]) + b_ref[...]


def _modulation(c, w_ada, b_ada):
    rows, d = c.shape
    n = w_ada.shape[1]
    tn = d
    return pl.pallas_call(
        _mod_kernel,
        grid=(n // tn,),
        in_specs=[pl.BlockSpec((rows, d), lambda j: (0, 0)),
                  pl.BlockSpec((d, tn), lambda j: (0, j)),
                  pl.BlockSpec((1, tn), lambda j: (0, j))],
        out_specs=pl.BlockSpec((rows, tn), lambda j: (0, j)),
        out_shape=jax.ShapeDtypeStruct((rows, n), F32),
        compiler_params=_cparams(("arbitrary",)),
        name="adaln_modulation",
    )(c, w_ada, b_ada.reshape(1, n))


def _mod_rows(mod_ref, which, d, per_row, b):
    if per_row:
        return mod_ref[:, which * d:(which + 1) * d]
    return mod_ref[pl.ds(b, 1), which * d:(which + 1) * d]


def _in_proj_kernel(x_ref, mod_ref, g_ref, w_ref, *outs, per_row, tiles_per_seq, split):
    i = pl.program_id(0)
    d = x_ref.shape[1]
    b = i // tiles_per_seq
    sh = _mod_rows(mod_ref, 0, d, per_row, b)
    sc = _mod_rows(mod_ref, 1, d, per_row, b)
    h = (_rms(x_ref[...]) * g_ref[...] * (1.0 + sc) + sh).astype(BF16)
    for c in range(w_ref.shape[1] // d):
        acc = _dot(h, w_ref[:, c * d:(c + 1) * d])
        if not split:
            outs[0][:, c * d:(c + 1) * d] = acc
        elif c == 0:
            outs[0][...] = acc
        else:
            outs[1][:, (c - 1) * d:c * d] = acc.astype(BF16)


def _in_proj(x, mod, g, w_bf, *, tm, per_row, tiles_per_seq, split):
    n, d = x.shape
    width = w_bf.shape[1]
    kern = functools.partial(_in_proj_kernel, per_row=per_row, tiles_per_seq=tiles_per_seq, split=split)
    mod_spec = (pl.BlockSpec((tm, mod.shape[1]), lambda i: (i, 0)) if per_row
                else pl.BlockSpec(mod.shape, lambda i: (0, 0)))
    if split:
        out_specs = [pl.BlockSpec((tm, d), lambda i: (i, 0)),
                     pl.BlockSpec((tm, width - d), lambda i: (i, 0))]
        out_shape = [jax.ShapeDtypeStruct((n, d), F32), jax.ShapeDtypeStruct((n, width - d), BF16)]
    else:
        out_specs = pl.BlockSpec((tm, width), lambda i: (i, 0))
        out_shape = jax.ShapeDtypeStruct((n, width), F32)
    return pl.pallas_call(
        kern,
        grid=(n // tm,),
        in_specs=[pl.BlockSpec((tm, d), lambda i: (i, 0)),
                  mod_spec,
                  pl.BlockSpec((1, d), lambda i: (0, 0)),
                  pl.BlockSpec((d, width), lambda i: (0, 0), pipeline_mode=pl.Buffered(1))],
        out_specs=out_specs,
        out_shape=out_shape,
        compiler_params=_cparams(("arbitrary",)),
        name="in_proj_per_row" if per_row else "in_proj",
    )(x, mod, g.reshape(1, d), w_bf)


def _pool_prompt_kernel(u_ref, w_ref, s_ref, y_ref, tail_ref, prev_ref, *, tt):
    t = pl.program_id(1)
    nt = pl.num_programs(1)
    hist = prev_ref.shape[0]
    gw = w_ref.shape[1]

    @pl.when(t == 0)
    def _():
        prev_ref[...] = jnp.zeros_like(prev_ref)

    u = u_ref[...]
    ext = jnp.concatenate([prev_ref[...], u], axis=0)
    pos = t * tt + lax.broadcasted_iota(jnp.int32, (tt, 1), 0)
    outs = []
    for gi, w in enumerate(POOL_WINDOWS):
        cols = slice(gi * gw, (gi + 1) * gw)
        run = ext[:, cols]
        span = 1
        while span < w:
            run = run + pltpu.roll(run, span, 0)
            span *= 2
        inv_cnt = 1.0 / jnp.minimum(w, pos + 1).astype(F32)
        pooled = run[hist:, :] * inv_cnt - u[:, cols]
        outs.append(_dot(pooled.astype(BF16), w_ref[gi].astype(BF16)))
    y = jnp.concatenate(outs, axis=-1) * s_ref[...]
    y_ref[...] = y.astype(BF16)
    prev_ref[...] = u[tt - hist:, :]

    @pl.when(t == nt - 1)
    def _():
        tail_ref[0] = u[tt - hist:, :]


def _pool_prompt(proj, w_pool, pool_scale, *, batch, seq, tt):
    p = pool_scale.shape[0]
    hist = POOL_BUF + 1
    nt = seq // tt
    kern = functools.partial(_pool_prompt_kernel, tt=tt)
    return pl.pallas_call(
        kern,
        grid=(batch, nt),
        in_specs=[pl.BlockSpec((tt, p), lambda b, t: (b * nt + t, 0)),
                  pl.BlockSpec(w_pool.shape, lambda b, t: (0, 0, 0)),
                  pl.BlockSpec((1, p), lambda b, t: (0, 0))],
        out_specs=[pl.BlockSpec((tt, p), lambda b, t: (b * nt + t, 0)),
                   pl.BlockSpec((1, hist, p), lambda b, t: (b, 0, 0))],
        out_shape=[jax.ShapeDtypeStruct((batch * seq, p), BF16),
                   jax.ShapeDtypeStruct((batch, hist, p), F32)],
        scratch_shapes=[pltpu.VMEM((hist, p), F32)],
        compiler_params=_cparams(("arbitrary", "arbitrary")),
        name="pool_prompt",
    )(proj, w_pool, pool_scale.reshape(1, p))


def _pool_decode_kernel(st_ref, u_ref, w_ref, s_ref, y_ref, new_ref):
    buf = st_ref.shape[0]
    gw = w_ref.shape[1]
    u = u_ref[...]
    outs = []
    run = u
    used = 0
    for gi, w in enumerate(POOL_WINDOWS):
        while used < w - 1:
            run = run + st_ref[buf - 1 - used]
            used += 1
        cols = slice(gi * gw, (gi + 1) * gw)
        pooled = run[:, cols] / float(min(w, PAST_LEN + 1)) - u[:, cols]
        outs.append(_dot(pooled.astype(BF16), w_ref[gi].astype(BF16)))
    y_ref[...] = (jnp.concatenate(outs, axis=-1) * s_ref[...]).astype(BF16)
    for r in range(buf - 1):
        new_ref[r] = st_ref[r + 1]
    new_ref[buf - 1] = u


def _pool_decode(state_t, proj_s, w_pool, pool_scale):
    buf, n, p = state_t.shape
    return pl.pallas_call(
        _pool_decode_kernel,
        grid=(1,),
        in_specs=[pl.BlockSpec((buf, n, p), lambda i: (0, 0, 0)),
                  pl.BlockSpec((n, p), lambda i: (0, 0)),
                  pl.BlockSpec(w_pool.shape, lambda i: (0, 0, 0)),
                  pl.BlockSpec((1, p), lambda i: (0, 0))],
        out_specs=[pl.BlockSpec((n, p), lambda i: (0, 0)),
                   pl.BlockSpec((buf, n, p), lambda i: (0, 0, 0))],
        out_shape=[jax.ShapeDtypeStruct((n, p), BF16),
                   jax.ShapeDtypeStruct((buf, n, p), F32)],
        compiler_params=_cparams(("arbitrary",)),
        name="pool_decode",
    )(state_t, proj_s, w_pool, pool_scale.reshape(1, p))


def _log_gamma(h):
    return float(np.log(1.0 - 2.0 ** (-5.0 - h)))


def _rope_tables(pos, half):
    inv = np.power(ROPE_BASE, -np.arange(half, dtype=np.float64) / half)
    ang = np.asarray(pos, np.float64)[:, None] * inv[None, :]
    return jnp.asarray(np.cos(ang), F32), jnp.asarray(np.sin(ang), F32)


def _rotary(x, cos, sin):
    half = x.shape[-1] // 2
    x1, x2 = x[:, :half], x[:, half:]
    return jnp.concatenate([x1 * cos - x2 * sin, x2 * cos + x1 * sin], axis=-1)


def _ret_tables(c, key_scale):
    i = np.arange(c, dtype=np.float64)
    diff = i[:, None] - i[None, :]
    dec, qs, ks = [], [], []
    for h in range(RET_HEADS):
        lg = _log_gamma(h)
        dec.append(key_scale * np.where(diff >= 0, np.exp(lg * np.maximum(diff, 0.0)), 0.0))
        qs.append(np.broadcast_to(np.exp(lg * (i + 1.0))[:, None], (c, LANES)))
        ks.append(key_scale * np.broadcast_to(np.exp(lg * (c - 1.0 - i))[:, None], (c, LANES)))
    return (jnp.asarray(np.stack(dec), F32), jnp.asarray(np.stack(qs), F32), jnp.asarray(np.stack(ks), F32))


def _ret_prompt_kernel(q_ref, k_ref, v_ref, g_ref, cos_ref, sin_ref, dec_ref, qs_ref, ks_ref,
                       y_ref, s_out_ref, s_ref, *, chunk_decay):
    c = pl.program_id(1)
    nc = pl.num_programs(1)
    heads, dk, dv = s_ref.shape

    @pl.when(c == 0)
    def _():
        s_ref[...] = jnp.zeros_like(s_ref)

    reps = dk // LANES
    chunk = dec_ref.shape[1]
    for r0 in range(0, q_ref.shape[0], chunk):
        rows = slice(r0, r0 + chunk)
        cos, sin = cos_ref[rows, :], sin_ref[rows, :]
        for h in range(heads):
            q = _rotary(q_ref[rows, h * dk:(h + 1) * dk].astype(F32), cos, sin)
            k = _rotary(k_ref[rows, h * dk:(h + 1) * dk].astype(F32), cos, sin)
            v = v_ref[rows, h * dv:(h + 1) * dv]
            qs = jnp.concatenate([qs_ref[h]] * reps, axis=-1)
            ks = jnp.concatenate([ks_ref[h]] * reps, axis=-1)
            scores = lax.dot_general(q.astype(BF16), k.astype(BF16), (((1,), (1,)), ((), ())),
                                     preferred_element_type=F32) * dec_ref[h]
            state = s_ref[h]
            o = _dot(scores.astype(BF16), v) + _dot((q * qs).astype(BF16), state.astype(BF16))
            kv = lax.dot_general((k * ks).astype(BF16), v, (((0,), (0,)), ((), ())), preferred_element_type=F32)
            s_ref[h] = chunk_decay[h] * state + kv
            g = g_ref[rows, h * dv:(h + 1) * dv].astype(F32)
            y_ref[rows, h * dv:(h + 1) * dv] = (_silu(g) * _rms(o)).astype(BF16)

    @pl.when(c == nc - 1)
    def _():
        s_out_ref[0] = s_ref[...]


def _ret_prompt(qkvg, *, batch, seq, d_model, chunk):
    heads = RET_HEADS
    dk = d_model // heads
    dv = 2 * d_model // heads
    rows = min(RET_STEP_CHUNKS * chunk, seq)
    nc = seq // rows
    cos, sin = _rope_tables(np.arange(seq), dk // 2)
    key_scale = dk ** -0.5
    assert np.log2(key_scale) == np.round(np.log2(key_scale))
    dec, qs, ks = _ret_tables(chunk, key_scale)
    chunk_decay = tuple(float(np.exp(_log_gamma(h) * chunk)) for h in range(heads))
    kern = functools.partial(_ret_prompt_kernel, chunk_decay=chunk_decay)
    const = lambda a: pl.BlockSpec(a.shape, lambda b, c: (0,) * a.ndim)
    return pl.pallas_call(
        kern,
        grid=(batch, nc),
        in_specs=[pl.BlockSpec((rows, d_model), lambda b, c: (b * nc + c, 0)),
                  pl.BlockSpec((rows, d_model), lambda b, c: (b * nc + c, 1)),
                  pl.BlockSpec((rows, 2 * d_model), lambda b, c: (b * nc + c, 1)),
                  pl.BlockSpec((rows, 2 * d_model), lambda b, c: (b * nc + c, 2)),
                  pl.BlockSpec((rows, dk // 2), lambda b, c: (c, 0)),
                  pl.BlockSpec((rows, dk // 2), lambda b, c: (c, 0)),
                  const(dec), const(qs), const(ks)],
        out_specs=[pl.BlockSpec((rows, heads * dv), lambda b, c: (b * nc + c, 0)),
                   pl.BlockSpec((1, heads, dk, dv), lambda b, c: (b, 0, 0, 0))],
        out_shape=[jax.ShapeDtypeStruct((batch * seq, heads * dv), BF16),
                   jax.ShapeDtypeStruct((batch, heads, dk, dv), F32)],
        scratch_shapes=[pltpu.VMEM((heads, dk, dv), F32)],
        compiler_params=_cparams(("arbitrary", "arbitrary")),
        name="retention_prompt",
    )(qkvg, qkvg, qkvg, qkvg, cos, sin, dec, qs, ks)


def _ret_decode_kernel(q_ref, k_ref, v_ref, g_ref, cos_ref, sin_ref, s_ref, y_ref, s_out_ref,
                       kt_ref, qd_ref, o_ref, *, gammas, bs):
    h = pl.program_id(0)
    blk = pl.program_id(1)
    nblk = pl.num_programs(1)
    n, dk = q_ref.shape
    gamma = jnp.where(h == 0, gammas[0], jnp.where(h == 1, gammas[1], jnp.where(h == 2, gammas[2], gammas[3])))

    @pl.when(blk == 0)
    def _():
        cos, sin = cos_ref[...], sin_ref[...]
        q = _rotary(q_ref[...], cos, sin)
        k = _rotary(k_ref[...], cos, sin) * (dk ** -0.5)
        kt_ref[...] = k.T
        qd_ref[...] = (q * gamma).astype(BF16)
        o_ref[...] = jnp.sum(q * k, axis=-1, keepdims=True) * v_ref[...]

    vb = v_ref[...].astype(BF16)
    rows = lax.broadcasted_iota(jnp.int32, (n, 1), 0)
    lanes = lax.broadcasted_iota(jnp.int32, (1, n), 1)
    for i in range(bs):
        tok = blk * bs + i
        state = s_ref[i, 0]
        o_ref[...] += jnp.where(rows == tok, _dot(qd_ref[...], state.astype(BF16)), 0.0)
        k_col = jnp.where(lanes == tok, kt_ref[...], 0.0).astype(BF16)
        s_out_ref[i, 0] = gamma * state + _dot(k_col, vb)

    @pl.when(blk == nblk - 1)
    def _():
        g = g_ref[...]
        y_ref[...] = (_silu(g) * _rms(o_ref[...])).astype(BF16)


def _ret_decode(proj_s, state, *, d_model, bs):
    n = proj_s.shape[0]
    heads = RET_HEADS
    dk = d_model // heads
    dv = 2 * d_model // heads
    cos, sin = _rope_tables(np.array([PAST_LEN]), dk // 2)
    gammas = tuple(float(np.exp(_log_gamma(h))) for h in range(heads))
    q0, k0, v0, g0 = d_model // dk, 2 * d_model // dk, 3 * d_model // dv, 5 * d_model // dv
    kern = functools.partial(_ret_decode_kernel, gammas=gammas, bs=bs)
    return pl.pallas_call(
        kern,
        grid=(heads, n // bs),
        in_specs=[pl.BlockSpec((n, dk), lambda h, b: (0, q0 + h)),
                  pl.BlockSpec((n, dk), lambda h, b: (0, k0 + h)),
                  pl.BlockSpec((n, dv), lambda h, b: (0, v0 + h)),
                  pl.BlockSpec((n, dv), lambda h, b: (0, g0 + h)),
                  pl.BlockSpec((1, dk // 2), lambda h, b: (0, 0)),
                  pl.BlockSpec((1, dk // 2), lambda h, b: (0, 0)),
                  pl.BlockSpec((bs, 1, dk, dv), lambda h, b: (b, h, 0, 0))],
        out_specs=[pl.BlockSpec((n, dv), lambda h, b: (0, h)),
                   pl.BlockSpec((bs, 1, dk, dv), lambda h, b: (b, h, 0, 0))],
        out_shape=[jax.ShapeDtypeStruct((n, heads * dv), BF16),
                   jax.ShapeDtypeStruct(state.shape, F32)],
        scratch_shapes=[pltpu.VMEM((dk, n), F32), pltpu.VMEM((n, dk), BF16), pltpu.VMEM((n, dv), F32)],
        compiler_params=_cparams(("arbitrary", "arbitrary")),
        name="retention_decode",
    )(proj_s, proj_s, proj_s, proj_s, cos, sin, state)


def _mix_kernel(x_ref, ya_ref, yb_ref, ga_ref, gb_ref, mod_ref, g2_ref, wa_ref, wb_ref, wo_ref,
                wr_ref, br_ref, x1_ref, h2_ref, route_ref, *, per_row, tiles_per_seq, n_experts):
    i = pl.program_id(0)
    tm, d = x_ref.shape
    b = i // tiles_per_seq

    merged = (_sigmoid(ga_ref[...].astype(F32)) * _dot(ya_ref[...], wa_ref[...])
              + _sigmoid(gb_ref[...].astype(F32)) * _dot(yb_ref[...], wb_ref[...]))
    mix = _dot(merged.astype(BF16), wo_ref[...])
    x1 = x_ref[...] + _mod_rows(mod_ref, 2, d, per_row, b) * mix
    x1_ref[...] = x1
    h2 = (_rms(x1) * g2_ref[...] * (1.0 + _mod_rows(mod_ref, 4, d, per_row, b))
          + _mod_rows(mod_ref, 3, d, per_row, b))
    h2_ref[...] = h2.astype(BF16)

    logits_t = _dot3_nt(wr_ref[...], h2) + br_ref[:, 0:1]
    row = lax.broadcasted_iota(jnp.int32, (n_experts, tm), 0)
    neg = jnp.float32(-jnp.inf)
    work = logits_t
    vals, idxs = [], []
    for _ in range(TOP_K):
        m = jnp.max(work, axis=0, keepdims=True)
        idx = jnp.min(jnp.where(work == m, row, n_experts), axis=0, keepdims=True)
        vals.append(m)
        idxs.append(idx)
        work = jnp.where(row == idx, neg, work)
    exps = [jnp.exp(v - vals[0]) for v in vals]
    denom = exps[0] + exps[1] + exps[2] + exps[3]
    route_ref[...] = jnp.concatenate([idx.astype(F32) for idx in idxs] + [e / denom for e in exps]
                                     + [jnp.zeros((LANES - 2 * TOP_K, tm), F32)], axis=0)


def _mix(x, ya, yb, proj, mod, g2, wa_bf, wb_bf, wo_bf, wr_pad, br_pad, *, tm, per_row, tiles_per_seq,
         n_experts, ga_blk):
    n, d = x.shape
    gb_blk = ga_blk + 1
    kern = functools.partial(_mix_kernel, per_row=per_row, tiles_per_seq=tiles_per_seq, n_experts=n_experts)
    mod_spec = (pl.BlockSpec((tm, mod.shape[1]), lambda i: (i, 0)) if per_row
                else pl.BlockSpec(mod.shape, lambda i: (0, 0)))
    full = lambda a: pl.BlockSpec(a.shape, lambda i: (0,) * a.ndim)
    return pl.pallas_call(
        kern,
        grid=(n // tm,),
        in_specs=[pl.BlockSpec((tm, d), lambda i: (i, 0)),
                  pl.BlockSpec((tm, ya.shape[1]), lambda i: (i, 0)),
                  pl.BlockSpec((tm, yb.shape[1]), lambda i: (i, 0)),
                  pl.BlockSpec((tm, d), lambda i: (i, ga_blk)),
                  pl.BlockSpec((tm, d), lambda i: (i, gb_blk)),
                  mod_spec,
                  pl.BlockSpec((1, d), lambda i: (0, 0)),
                  full(wa_bf), full(wb_bf), full(wo_bf), full(wr_pad), full(br_pad)],
        out_specs=[pl.BlockSpec((tm, d), lambda i: (i, 0)),
                   pl.BlockSpec((tm, d), lambda i: (i, 0)),
                   pl.BlockSpec((LANES, tm), lambda i: (0, i))],
        out_shape=[jax.ShapeDtypeStruct((n, d), F32),
                   jax.ShapeDtypeStruct((n, d), BF16),
                   jax.ShapeDtypeStruct((LANES, n), F32)],
        compiler_params=_cparams(("arbitrary",)),
        name="mix_route_per_row" if per_row else "mix_route",
    )(x, ya, yb, proj, proj, mod, g2.reshape(1, d), wa_bf, wb_bf, wo_bf, wr_pad, br_pad)


def _sorted_rows(tt, n_experts):
    return _round_up(tt * TOP_K + n_experts * (ROW_ALIGN - 1) + ROW_ALIGN, SORT_BLOCK)


def _dispatch_kernel(hp_ref, rp_ref, hd_ref, rd_ref, before_ref, below_ref, xs_ref, r2_ref, cnt_ref, *, lmax):
    tt = before_ref.shape[0]
    is_dec = pl.program_id(0) == pl.num_programs(0) - 1
    for sub in range(hp_ref.shape[0] // tt):
        tok = slice(sub * tt, (sub + 1) * tt)
        _sort_tile(jnp.where(is_dec, rd_ref[:, tok], rp_ref[:, tok]),
                   jnp.where(is_dec, hd_ref[tok, :], hp_ref[tok, :]),
                   before_ref, below_ref, xs_ref.at[sub * lmax:(sub + 1) * lmax, :], r2_ref.at[tok, :],
                   cnt_ref.at[sub * below_ref.shape[0]:(sub + 1) * below_ref.shape[0], :], lmax=lmax)


def _sort_tile(route_t, h_tile, before_ref, below_ref, xs_ref, r2_ref, cnt_ref, *, lmax):
    tt = h_tile.shape[0]
    n_e = cnt_ref.shape[0]
    expert = lax.broadcasted_iota(jnp.int32, (n_e, tt), 0).astype(F32)
    hits = [expert == route_t[kk:kk + 1, :] for kk in range(TOP_K)]
    sel = jnp.zeros((n_e, tt), F32)
    for hit in hits:
        sel = sel + jnp.where(hit, 1.0, 0.0)
    cnt = jnp.sum(sel, axis=1, keepdims=True)
    seg = jnp.broadcast_to(jnp.ceil(cnt / ROW_ALIGN) * ROW_ALIGN, (n_e, LANES))
    off = _dot3(below_ref[...], seg)[:, 0:1]
    place = _dot(sel.astype(BF16), before_ref[...]) + off
    w_hi, w_lo, pos_a, pos_b, pos_all = [], [], [], [], []
    for kk in range(TOP_K):
        found = jnp.sum(jnp.where(hits[kk], 1.0, 0.0), axis=0, keepdims=True)
        pos = jnp.where(found > 0.0, jnp.sum(jnp.where(hits[kk], place, 0.0), axis=0, keepdims=True), -1.0)
        w = route_t[TOP_K + kk:TOP_K + kk + 1, :]
        hi = w.astype(BF16).astype(F32)
        a = jnp.floor(pos / POS_SPLIT)
        w_hi.append(hi)
        w_lo.append(w - hi)
        pos_a.append(a)
        pos_b.append(pos - POS_SPLIT * a)
        pos_all.append(pos)
    info_t = jnp.concatenate(w_hi + w_lo + pos_a + pos_b + pos_all
                             + [jnp.zeros((LANES - 5 * TOP_K, tt), F32)], axis=0)
    side = info_t.T
    r2_ref[...] = side
    cnt_ref[...] = jnp.broadcast_to(cnt, cnt_ref.shape)
    r2t = info_t.astype(jnp.int32).astype(jnp.int16)
    h = jnp.concatenate([h_tile, side.astype(BF16)], axis=-1)
    d = h_tile.shape[1]
    out_lane = lax.broadcasted_iota(jnp.int32, (SORT_BLOCK, LANES), 1)
    one = jnp.ones((SORT_BLOCK, tt), BF16)
    zero = jnp.zeros((SORT_BLOCK, tt), BF16)
    for rb in range(lmax // SORT_BLOCK):
        rio = (lax.broadcasted_iota(jnp.int32, (SORT_BLOCK, tt), 0) + rb * SORT_BLOCK).astype(jnp.int16)
        hit = rio == r2t[POS_LANE:POS_LANE + 1, :]
        for kk in range(1, TOP_K):
            hit = hit | (rio == r2t[POS_LANE + kk:POS_LANE + kk + 1, :])
        perm = jnp.where(hit, one, zero)
        res = _dot(perm, h)
        info = res[:, d:]
        row = (lax.broadcasted_iota(jnp.int32, (SORT_BLOCK, LANES), 0) + rb * SORT_BLOCK).astype(F32)
        pos = (POS_SPLIT * pltpu.roll(info, LANES - 2 * TOP_K, 1) + pltpu.roll(info, LANES - 3 * TOP_K, 1))
        mine = jnp.where((pos == row) & (out_lane < TOP_K), 1.0, 0.0)
        mine = mine + pltpu.roll(mine, TOP_K, 1)
        rows = slice(rb * SORT_BLOCK, (rb + 1) * SORT_BLOCK)
        xs_ref[rows, :d] = res[:, :d].astype(BF16)
        xs_ref[rows, d:] = (info * mine).astype(BF16)


def _dispatch(h2_p, route_p, h2_d, route_d, *, tt, lmax, n_experts):
    n, d = h2_p.shape
    group = DISPATCH_STEP_TILES if (n // tt) % DISPATCH_STEP_TILES == 0 else 1
    rows = group * tt
    p_steps = n // rows
    tiles = (p_steps + 1) * group
    pad = rows - h2_d.shape[0]
    h2_d = jnp.pad(h2_d, ((0, pad), (0, 0)))
    route_d = jnp.pad(route_d, ((0, 0), (0, pad)), constant_values=-1.0)
    kern = functools.partial(_dispatch_kernel, lmax=lmax)
    before = jnp.asarray(np.triu(np.ones((tt, tt), np.float32), 1), BF16)
    below = jnp.asarray(np.tril(np.ones((n_experts, n_experts), np.float32), -1), F32)
    return (tiles,) + tuple(pl.pallas_call(
        kern,
        grid=(p_steps + 1,),
        in_specs=[pl.BlockSpec((rows, d), lambda i: (jnp.minimum(i, p_steps - 1), 0)),
                  pl.BlockSpec((LANES, rows), lambda i: (0, jnp.minimum(i, p_steps - 1))),
                  pl.BlockSpec((rows, d), lambda i: (0, 0)),
                  pl.BlockSpec((LANES, rows), lambda i: (0, 0)),
                  pl.BlockSpec((tt, tt), lambda i: (0, 0)),
                  pl.BlockSpec((n_experts, n_experts), lambda i: (0, 0))],
        out_specs=[pl.BlockSpec((group * lmax, d + LANES), lambda i: (i, 0)),
                   pl.BlockSpec((rows, LANES), lambda i: (i, 0)),
                   pl.BlockSpec((group * n_experts, LANES), lambda i: (i, 0))],
        out_shape=[jax.ShapeDtypeStruct((tiles * lmax, d + LANES), BF16),
                   jax.ShapeDtypeStruct((tiles * tt, LANES), F32),
                   jax.ShapeDtypeStruct((tiles * n_experts, LANES), F32)],
        compiler_params=_cparams(("arbitrary",)),
        name="dispatch_sort",
    )(h2_p, route_p, h2_d, route_d, before, below))


def _tables_kernel(cnt_ref, te_ref, na_ref, nxt_ref, par_ref, msrc_ref, csrc_ref,
                   before_ref, start_ref, len_ref, first_ref, end_ref, total_ref, *, tiles, n_e, cpt, cstride,
                   lmax, n_tiles, zero_row):
    def nblk(i, e):
        return len_ref[i * n_e + e]

    def per_tile(i, carry):
        def per_expert(e, run):
            blocks = lax.shift_right_logical(cnt_ref[i * n_e + e] + (ROW_ALIGN - 1), ROW_ALIGN.bit_length() - 1)
            len_ref[i * n_e + e] = blocks
            start_ref[i * n_e + e] = run
            return run + blocks
        lax.fori_loop(0, n_e, per_expert, 0)
        return carry

    lax.fori_loop(0, tiles, per_tile, 0)

    def per_expert(e, carry):
        def per_tile(i, run):
            before_ref[i * n_e + e] = run
            return run + nblk(i, e)
        total_ref[e] = lax.fori_loop(0, tiles, per_tile, 0)
        return carry

    lax.fori_loop(0, n_e, per_expert, 0)

    def expert_tiles(e, carry):
        t_first, run = carry
        t_last = t_first + lax.shift_right_logical(total_ref[e] + (cpt - 1), cpt.bit_length() - 1)
        first_ref[e] = t_first
        end_ref[e] = t_last

        def mark(t, carry):
            te_ref[t] = e
            par_ref[t] = run % 2
            return carry

        lax.fori_loop(t_first, t_last, mark, 0)
        return t_last, jnp.where(t_last > t_first, run + 1, run)

    n_act, _ = lax.fori_loop(0, n_e, expert_tiles, (jnp.int32(0), jnp.int32(0)))
    na_ref[0] = n_act
    last = te_ref[jnp.maximum(n_act - 1, 0)]

    def mark_rest(t, carry):
        te_ref[t] = last
        par_ref[t] = 0
        nxt_ref[t] = -1
        return carry

    lax.fori_loop(n_act, n_tiles, mark_rest, 0)

    def mark_next(s, nxt):
        t = n_act - 1 - s
        after = te_ref[jnp.minimum(t + 1, n_tiles - 1)]
        nxt = jnp.where((t + 1 < n_act) & (after != te_ref[t]), after, nxt)
        nxt_ref[t] = nxt
        return nxt

    lax.fori_loop(0, n_act, mark_next, jnp.int32(-1))

    shape = msrc_ref.shape
    n_idx = (lax.broadcasted_iota(jnp.int32, shape, 0) * LANES + lax.broadcasted_iota(jnp.int32, shape, 1))
    m_tile = lax.shift_right_logical(n_idx, cpt.bit_length() - 1)
    expert = jnp.zeros(shape, jnp.int32)
    for e in range(n_e):
        expert = expert + (end_ref[e] <= m_tile).astype(jnp.int32)
    t_first = jnp.zeros(shape, jnp.int32)
    total = jnp.zeros(shape, jnp.int32)
    for e in range(n_e):
        t_first = jnp.where(expert == e, first_ref[e], t_first)
        total = jnp.where(expert == e, total_ref[e], total)
    g = (m_tile - t_first) * cpt + (n_idx & (cpt - 1))
    src = jnp.full(shape, zero_row, jnp.int32)
    for i in range(tiles):
        seg_first = jnp.zeros(shape, jnp.int32)
        seg_len = jnp.zeros(shape, jnp.int32)
        seg_local = jnp.zeros(shape, jnp.int32)
        for e in range(n_e):
            mine = expert == e
            seg_first = jnp.where(mine, before_ref[i * n_e + e], seg_first)
            seg_len = jnp.where(mine, nblk(i, e), seg_len)
            seg_local = jnp.where(mine, start_ref[i * n_e + e], seg_local)
        inside = (g >= seg_first) & (g < seg_first + seg_len) & (g < total) & (m_tile < n_act)
        src = jnp.where(inside, i * lmax + (seg_local + g - seg_first) * ROW_ALIGN, src)
    msrc_ref[...] = src

    rows_per_tile = cstride // LANES
    c_idx = (lax.broadcasted_iota(jnp.int32, (rows_per_tile, LANES), 0) * LANES
             + lax.broadcasted_iota(jnp.int32, (rows_per_tile, LANES), 1))
    for i in range(tiles):
        owner = jnp.zeros((rows_per_tile, LANES), jnp.int32)
        for e in range(n_e):
            owner = owner + ((start_ref[i * n_e + e] + nblk(i, e)) <= c_idx).astype(jnp.int32)
        base = jnp.zeros((rows_per_tile, LANES), jnp.int32)
        for e in range(n_e):
            base = jnp.where(owner == e, first_ref[e] * cpt + before_ref[i * n_e + e] - start_ref[i * n_e + e],
                             base)
        csrc_ref[i * rows_per_tile:(i + 1) * rows_per_tile, :] = jnp.where(owner < n_e,
                                                                         (base + c_idx) * ROW_ALIGN, 0)


def _routing_tables(cnt_flat, *, tiles, n_e, tmo, n_tiles, cstride, lmax):
    cpt = tmo // ROW_ALIGN
    assert (n_tiles * cpt) % LANES == 0 and cstride % LANES == 0
    kern = functools.partial(_tables_kernel, tiles=tiles, n_e=n_e, cpt=cpt, cstride=cstride, lmax=lmax,
                             n_tiles=n_tiles, zero_row=lmax - ROW_ALIGN)
    smem = pl.BlockSpec(memory_space=pltpu.SMEM)
    vmem = pl.BlockSpec(memory_space=pltpu.VMEM)
    te, na, nxt, par, msrc, csrc = pl.pallas_call(
        kern,
        in_specs=[smem],
        out_specs=[smem, smem, smem, smem, vmem, vmem],
        out_shape=[jax.ShapeDtypeStruct((n_tiles,), jnp.int32),
                   jax.ShapeDtypeStruct((1,), jnp.int32),
                   jax.ShapeDtypeStruct((n_tiles,), jnp.int32),
                   jax.ShapeDtypeStruct((n_tiles,), jnp.int32),
                   jax.ShapeDtypeStruct((n_tiles * cpt // LANES, LANES), jnp.int32),
                   jax.ShapeDtypeStruct((tiles * cstride // LANES, LANES), jnp.int32)],
        scratch_shapes=[pltpu.SMEM((tiles * n_e,), jnp.int32), pltpu.SMEM((tiles * n_e,), jnp.int32),
                        pltpu.SMEM((tiles * n_e,), jnp.int32),
                        pltpu.SMEM((n_e,), jnp.int32), pltpu.SMEM((n_e,), jnp.int32),
                        pltpu.SMEM((n_e,), jnp.int32)],
        name="routing_tables",
    )(cnt_flat)
    return te, na, msrc.reshape(-1), csrc.reshape(-1), nxt, par


def _moe_kernel(tile_expert_ref, n_active_ref, src_ref, nxt_ref, par_ref, xs_hbm, wi_hbm, bi_ref, wo_hbm, bo_ref,
                y_ref, xbuf, wi_f32, wo_f32, wi_bf, wo_bf, sem, wsem):
    j = pl.program_id(0)
    tm = xbuf.shape[1]
    d = y_ref.shape[1]
    f = wo_bf.shape[0]
    cpt = tm // ROW_ALIGN
    slot = j % 2
    n_active = n_active_ref[0]
    expert = tile_expert_ref[j]

    def start_tile(tile, s):
        for c in range(cpt):
            row = pl.multiple_of(src_ref[tile * cpt + c], ROW_ALIGN)
            pltpu.make_async_copy(xs_hbm.at[pl.ds(row, ROW_ALIGN), :],
                                  xbuf.at[s, pl.ds(c * ROW_ALIGN, ROW_ALIGN), :], sem.at[s]).start()

    def wait_tile(s):
        pltpu.make_async_copy(xs_hbm.at[pl.ds(0, tm), :], xbuf.at[s], sem.at[s]).wait()

    def weight_copies(e, s):
        return (pltpu.make_async_copy(wi_hbm.at[e], wi_f32.at[s], wsem.at[0, s]),
                pltpu.make_async_copy(wo_hbm.at[e], wo_f32.at[s], wsem.at[1, s]))

    @pl.when(j == 0)
    def _():
        start_tile(0, 0)
        for cp in weight_copies(expert, par_ref[0]):
            cp.start()

    @pl.when(j + 1 < n_active)
    def _():
        start_tile(j + 1, 1 - slot)

    @pl.when(j < n_active)
    def _():
        @pl.when((j == 0) | (expert != tile_expert_ref[jnp.maximum(j - 1, 0)]))
        def _():
            ws = par_ref[j]
            for cp in weight_copies(expert, ws):
                cp.wait()

            @pl.when(nxt_ref[j] >= 0)
            def _():
                for cp in weight_copies(nxt_ref[j], 1 - ws):
                    cp.start()

            wi_bf[...] = wi_f32[ws].astype(BF16)
            wo_bf[...] = wo_f32[ws].astype(BF16)

        wait_tile(slot)
        rows = xbuf[slot]
        hh = _dot(rows[:, :d], wi_bf[...]) + bi_ref[0]
        gate = jnp.minimum(hh[:, :f], SWIGLU_LIMIT)
        lin = jnp.clip(hh[:, f:], -SWIGLU_LIMIT, SWIGLU_LIMIT)
        half_gate = 0.5 * gate
        act = (lin + 1.0) * (half_gate + half_gate * jnp.tanh((0.5 * SWIGLU_ALPHA) * gate))
        weight = jnp.sum(rows[:, d:].astype(F32), axis=-1, keepdims=True)
        y_ref[...] = ((_dot(act.astype(BF16), wo_bf[...]) + bo_ref[0]) * weight).astype(BF16)

    @pl.when(j >= n_active)
    def _():
        y_ref[...] = jnp.zeros_like(y_ref)


def _moe(xs, tables, w_in, b_in, w_out, b_out, *, tm, n_tiles):
    tile_expert, n_active, moe_src, _, nxt, par = tables
    e, d, f2 = w_in.shape
    f = w_out.shape[1]
    bias = lambda width: pl.BlockSpec((1, 1, width), lambda j, te, *_: (te[j], 0, 0))
    grid_spec = pltpu.PrefetchScalarGridSpec(
        num_scalar_prefetch=5,
        grid=(n_tiles,),
        in_specs=[pl.BlockSpec(memory_space=pl.ANY),
                  pl.BlockSpec(memory_space=pl.ANY),
                  bias(f2),
                  pl.BlockSpec(memory_space=pl.ANY),
                  bias(d)],
        out_specs=pl.BlockSpec((tm, d), lambda j, *_: (j, 0)),
        scratch_shapes=[pltpu.VMEM((2, tm, xs.shape[1]), BF16),
                        pltpu.VMEM((2, d, f2), F32),
                        pltpu.VMEM((2, f, d), F32),
                        pltpu.VMEM((d, f2), BF16),
                        pltpu.VMEM((f, d), BF16),
                        pltpu.SemaphoreType.DMA((2,)),
                        pltpu.SemaphoreType.DMA((2, 2))],
    )
    return pl.pallas_call(
        _moe_kernel,
        grid_spec=grid_spec,
        out_shape=jax.ShapeDtypeStruct((n_tiles * tm, d), BF16),
        compiler_params=_cparams(("arbitrary",)),
        name="moe_experts",
    )(tile_expert, n_active, moe_src, nxt, par, xs, w_in, b_in.reshape(e, 1, f2), w_out, b_out.reshape(e, 1, d))


def _combine_kernel(src_ref, x1_ref, r2_ref, mod_ref, gf_ref, y_hbm, o_ref, ybuf, sem,
                    *, per_row, tiles_per_seq, tile0, stride, single):
    i = pl.program_id(0)
    nt = pl.num_programs(0)
    tt, d = x1_ref.shape
    lmax = ybuf.shape[1]
    nch = lmax // ROW_ALIGN
    slot = i % 2
    b = i // tiles_per_seq

    def start_tile(tile, s):
        for c in range(nch):
            row = pl.multiple_of(src_ref[(tile0 + tile) * stride + c], ROW_ALIGN)
            pltpu.make_async_copy(y_hbm.at[pl.ds(row, ROW_ALIGN), :],
                                  ybuf.at[s, pl.ds(c * ROW_ALIGN, ROW_ALIGN), :], sem.at[s]).start()

    def wait_tile(s):
        pltpu.make_async_copy(y_hbm.at[pl.ds(0, lmax), :], ybuf.at[s], sem.at[s]).wait()

    @pl.when(i == 0)
    def _():
        start_tile(0, 0)

    if not single:
        start_tile(jnp.minimum(i + 1, nt - 1), 1 - slot)
    wait_tile(slot)
    r2 = r2_ref[...].astype(jnp.int32).astype(jnp.int16)
    one = jnp.ones((tt, SORT_BLOCK), BF16)
    zero = jnp.zeros((tt, SORT_BLOCK), BF16)
    moe = jnp.zeros((tt, d), F32)
    for cb in range(lmax // SORT_BLOCK):
        col = (lax.broadcasted_iota(jnp.int32, (tt, SORT_BLOCK), 1) + cb * SORT_BLOCK).astype(jnp.int16)
        hit = col == r2[:, POS_LANE:POS_LANE + 1]
        for kk in range(1, TOP_K):
            hit = hit | (col == r2[:, POS_LANE + kk:POS_LANE + kk + 1])
        moe = moe + _dot(jnp.where(hit, one, zero), ybuf[slot, cb * SORT_BLOCK:(cb + 1) * SORT_BLOCK, :])
    x2 = x1_ref[...] + _mod_rows(mod_ref, 5, d, per_row, b) * moe
    o_ref[...] = _rms(x2) * gf_ref[...]

    if not single:
        @pl.when(i == nt - 1)
        def _():
            wait_tile(1 - slot)


def _combine(comb_src, x1, r2, mod, gf, ys, *, tt, lmax, per_row, tiles_per_seq, tile0, stride, r2_block0):
    n, d = x1.shape
    kern = functools.partial(_combine_kernel, per_row=per_row, tiles_per_seq=tiles_per_seq, tile0=tile0,
                             stride=stride, single=(n // tt == 1))
    mod_spec = (pl.BlockSpec((tt, mod.shape[1]), lambda i, s: (i, 0)) if per_row
                else pl.BlockSpec(mod.shape, lambda i, s: (0, 0)))
    grid_spec = pltpu.PrefetchScalarGridSpec(
        num_scalar_prefetch=1,
        grid=(n // tt,),
        in_specs=[pl.BlockSpec((tt, d), lambda i, s: (i, 0)),
                  pl.BlockSpec((tt, LANES), lambda i, s: (r2_block0 + i, 0)),
                  mod_spec,
                  pl.BlockSpec((1, d), lambda i, s: (0, 0)),
                  pl.BlockSpec(memory_space=pl.ANY)],
        out_specs=pl.BlockSpec((tt, d), lambda i, s: (i, 0)),
        scratch_shapes=[pltpu.VMEM((2, lmax, d), BF16), pltpu.SemaphoreType.DMA((2,))],
    )
    return pl.pallas_call(
        kern,
        grid_spec=grid_spec,
        out_shape=jax.ShapeDtypeStruct((n, d), F32),
        compiler_params=_cparams(("arbitrary",)),
        name="combine_final_per_row" if per_row else "combine_final",
    )(comb_src, x1, r2, mod, gf.reshape(1, d), ys)


def kernel(x_prompt, x_sample, c_prompt, c_sample, state_pool, state_ret, w_ada, b_ada, norm1_g, w_in,
           w_pool, pool_scale, w_up_pool, w_up_ret, w_out, norm2_g, w_router, b_router, w_expert_in,
           b_expert_in, w_expert_out, b_expert_out, final_norm_g):
    batch, seq, d = x_prompt.shape
    n_dec = x_sample.shape[0]
    n_prompt = batch * seq
    n_experts = w_router.shape[-1]
    xp = x_prompt.reshape(n_prompt, d)
    xs = x_sample.reshape(n_dec, d)

    w_in_bf = w_in[0].astype(BF16)
    wa_bf = w_up_pool[0].astype(BF16)
    wb_bf = w_up_ret[0].astype(BF16)
    wo_bf = w_out[0].astype(BF16)
    assert n_experts % 8 == 0
    wr_pad = w_router[0].T
    br_pad = jnp.broadcast_to(b_router[0][:, None], (n_experts, LANES))

    c_all = jnp.concatenate([c_prompt, c_sample], axis=0)
    c_all = jnp.pad(c_all, ((0, -c_all.shape[0] % 16), (0, 0)))
    mod = _modulation(c_all, w_ada[0], b_ada[0])
    mod_p, mod_s = mod[:batch], mod[batch:batch + n_dec]

    tm = min(IN_PROJ_ROWS, seq)
    u_p, qkvg_p = _in_proj(xp, mod_p, norm1_g[0], w_in_bf, tm=tm, per_row=False, tiles_per_seq=seq // tm,
                           split=True)
    proj_s = _in_proj(xs, mod_s, norm1_g[0], w_in_bf, tm=n_dec, per_row=True, tiles_per_seq=1, split=False)

    ya_p, tail_p = _pool_prompt(u_p, w_pool[0], pool_scale[0], batch=batch, seq=seq, tt=min(POOL_ROWS, seq))
    ya_s, pool_s_t = _pool_decode(jnp.transpose(state_pool[0], (1, 0, 2)), proj_s, w_pool[0], pool_scale[0])
    yb_p, ret_p = _ret_prompt(qkvg_p, batch=batch, seq=seq, d_model=d, chunk=min(RET_CHUNK, seq))
    yb_s, ret_s = _ret_decode(proj_s, state_ret[0], d_model=d, bs=min(RET_DECODE_SEQS, n_dec))

    tmix = min(MIX_ROWS, seq)
    x1_p, h2_p, route_p = _mix(xp, ya_p, yb_p, qkvg_p, mod_p, norm2_g[0], wa_bf, wb_bf, wo_bf, wr_pad, br_pad,
                               tm=tmix, per_row=False, tiles_per_seq=seq // tmix, n_experts=n_experts,
                               ga_blk=6)
    x1_s, h2_s, route_s = _mix(xs, ya_s, yb_s, proj_s, mod_s, norm2_g[0], wa_bf, wb_bf, wo_bf, wr_pad, br_pad,
                               tm=n_dec, per_row=True, tiles_per_seq=1, n_experts=n_experts, ga_blk=7)

    tt = min(TOKEN_TILE, seq)
    assert n_dec <= tt
    p_tiles = n_prompt // tt
    lmax = _sorted_rows(tt, n_experts)
    tiles, xs_buf, r2, cnt_t = _dispatch(h2_p, route_p, h2_s, route_s, tt=tt, lmax=lmax, n_experts=n_experts)

    tmo = MOE_ROW_TILE
    cnt = cnt_t[:, 0].astype(jnp.int32)
    max_rows = (n_prompt + n_dec) * TOP_K + tiles * n_experts * (ROW_ALIGN - 1)
    n_tiles = _round_up(max_rows // tmo + n_experts, LANES * ROW_ALIGN // tmo)
    stride = _round_up(lmax // ROW_ALIGN, LANES)
    tables = _routing_tables(cnt, tiles=tiles, n_e=n_experts, tmo=tmo, n_tiles=n_tiles, cstride=stride,
                             lmax=lmax)
    comb_src = tables[3]

    ys = _moe(xs_buf, tables, w_expert_in[0], b_expert_in[0], w_expert_out[0], b_expert_out[0], tm=tmo,
              n_tiles=n_tiles)

    assert n_prompt % n_dec == 0
    y_p = _combine(comb_src, x1_p, r2, mod_p, final_norm_g, ys, tt=tt, lmax=lmax, per_row=False,
                   tiles_per_seq=seq // tt, tile0=0, stride=stride, r2_block0=0)
    y_s = _combine(comb_src, x1_s, r2, mod_s, final_norm_g, ys, tt=n_dec, lmax=min(lmax, _sorted_rows(n_dec, n_experts)),
                   per_row=True, tiles_per_seq=1, tile0=p_tiles, stride=stride, r2_block0=n_prompt // n_dec)

    return (y_p.reshape(batch, seq, d),
            y_s.reshape(n_dec, 1, d),
            tail_p[None, :, 1:, :],
            ret_p[None],
            jnp.transpose(pool_s_t, (1, 0, 2))[None],
            ret_s[None])
```

```python
import functools

import numpy as np
import jax
import jax.numpy as jnp
from jax import lax
from jax.experimental import pallas as pl
from jax.experimental.pallas import tpu as pltpu

F32 = jnp.float32
BF16 = jnp.bfloat16

POOL_WINDOWS = (2, 4, 8, 16)
POOL_BUF = max(POOL_WINDOWS) - 1
RET_HEADS = 4
RET_CHUNK = 256
RET_STEP_CHUNKS = 4
DISPATCH_STEP_TILES = 2
ROPE_BASE = 10000.0
PAST_LEN = 16384
TOP_K = 4
SWIGLU_LIMIT = 7.0
SWIGLU_ALPHA = 1.702
EPS = 1e-6
LANES = 128
ROW_ALIGN = 16
SORT_BLOCK = 512
POS_SPLIT = 64.0
POS_LANE = 16
TOKEN_TILE = 512
MOE_ROW_TILE = 512
MOE_HIDDEN_SLICE = 512
IN_PROJ_ROWS = 512
MIX_ROWS = 512
POOL_ROWS = 1024
RET_DECODE_SEQS = 16
VMEM_LIMIT = 56 * 1024 * 1024


def _cparams(sem):
    return pltpu.CompilerParams(dimension_semantics=sem, vmem_limit_bytes=VMEM_LIMIT)


def _dot(a, b):
    return jnp.dot(a, b, preferred_element_type=F32)


def _dot3(a, b):
    a_hi = a.astype(BF16)
    a_lo = (a - a_hi.astype(F32)).astype(BF16)
    b_hi = b.astype(BF16)
    b_lo = (b - b_hi.astype(F32)).astype(BF16)
    return _dot(a_hi, b_hi) + _dot(a_hi, b_lo) + _dot(a_lo, b_hi)


def _dot3_nt(a, b):
    nt = lambda x, y: lax.dot_general(x, y, (((1,), (1,)), ((), ())), preferred_element_type=F32)
    a_hi = a.astype(BF16)
    a_lo = (a - a_hi.astype(F32)).astype(BF16)
    b_hi = b.astype(BF16)
    b_lo = (b - b_hi.astype(F32)).astype(BF16)
    return nt(a_hi, b_hi) + nt(a_hi, b_lo) + nt(a_lo, b_hi)


def _sigmoid(x):
    return 0.5 + 0.5 * jnp.tanh(0.5 * x)


def _silu(x):
    half = 0.5 * x
    return half + half * jnp.tanh(half)


def _rms(x):
    return x * lax.rsqrt(jnp.mean(x * x, axis=-1, keepdims=True) + EPS)


def _round_up(x, m):
    return (x + m - 1) // m * m


def _mod_kernel(c_ref, w_ref, b_ref, o_ref):
    c = c_ref[...]
    o_ref[...] = _dot3(_silu(c_ref[...]), w_ref[...]) + b_ref[...]


def _modulation(c, w_ada, b_ada):
    rows, d = c.shape
    n = w_ada.shape[1]
    tn = d
    return pl.pallas_call(
        _mod_kernel,
        grid=(n // tn,),
        in_specs=[pl.BlockSpec((rows, d), lambda j: (0, 0)),
                  pl.BlockSpec((d, tn), lambda j: (0, j)),
                  pl.BlockSpec((1, tn), lambda j: (0, j))],
        out_specs=pl.BlockSpec((rows, tn), lambda j: (0, j)),
        out_shape=jax.ShapeDtypeStruct((rows, n), F32),
        compiler_params=_cparams(("arbitrary",)),
        name="adaln_modulation",
    )(c, w_ada, b_ada.reshape(1, n))


def _mod_rows(mod_ref, which, d, per_row, b):
    if per_row:
        return mod_ref[:, which * d:(which + 1) * d]
    return mod_ref[pl.ds(b, 1), which * d:(which + 1) * d]


def _in_proj_kernel(x_ref, mod_ref, g_ref, w_ref, *outs, per_row, tiles_per_seq, split):
    i = pl.program_id(0)
    d = x_ref.shape[1]
    b = i // tiles_per_seq
    sh = _mod_rows(mod_ref, 0, d, per_row, b)
    sc = _mod_rows(mod_ref, 1, d, per_row, b)
    h = (_rms(x_ref[...]) * g_ref[...] * (1.0 + sc) + sh).astype(BF16)
    for c in range(w_ref.shape[1] // d):
        acc = _dot(h, w_ref[:, c * d:(c + 1) * d])
        if not split:
            outs[0][:, c * d:(c + 1) * d] = acc
        elif c == 0:
            outs[0][...] = acc
        else:
            outs[1][:, (c - 1) * d:c * d] = acc.astype(BF16)


def _in_proj(x, mod, g, w_bf, *, tm, per_row, tiles_per_seq, split):
    n, d = x.shape
    width = w_bf.shape[1]
    kern = functools.partial(_in_proj_kernel, per_row=per_row, tiles_per_seq=tiles_per_seq, split=split)
    mod_spec = (pl.BlockSpec((tm, mod.shape[1]), lambda i: (i, 0)) if per_row
                else pl.BlockSpec(mod.shape, lambda i: (0, 0)))
    if split:
        out_specs = [pl.BlockSpec((tm, d), lambda i: (i, 0)),
                     pl.BlockSpec((tm, width - d), lambda i: (i, 0))]
        out_shape = [jax.ShapeDtypeStruct((n, d), F32), jax.ShapeDtypeStruct((n, width - d), BF16)]
    else:
        out_specs = pl.BlockSpec((tm, width), lambda i: (i, 0))
        out_shape = jax.ShapeDtypeStruct((n, width), F32)
    return pl.pallas_call(
        kern,
        grid=(n // tm,),
        in_specs=[pl.BlockSpec((tm, d), lambda i: (i, 0)),
                  mod_spec,
                  pl.BlockSpec((1, d), lambda i: (0, 0)),
                  pl.BlockSpec((d, width), lambda i: (0, 0), pipeline_mode=pl.Buffered(1))],
        out_specs=out_specs,
        out_shape=out_shape,
        compiler_params=_cparams(("arbitrary",)),
        name="in_proj_per_row" if per_row else "in_proj",
    )(x, mod, g.reshape(1, d), w_bf)


def _pool_prompt_kernel(u_ref, w_ref, s_ref, y_ref, tail_ref, prev_ref, *, tt):
    t = pl.program_id(1)
    nt = pl.num_programs(1)
    hist = prev_ref.shape[0]
    gw = w_ref.shape[1]

    @pl.when(t == 0)
    def _():
        prev_ref[...] = jnp.zeros_like(prev_ref)

    u = u_ref[...]
    ext = jnp.concatenate([prev_ref[...], u], axis=0)
    pos = t * tt + lax.broadcasted_iota(jnp.int32, (tt, 1), 0)
    outs = []
    for gi, w in enumerate(POOL_WINDOWS):
        cols = slice(gi * gw, (gi + 1) * gw)
        run = ext[:, cols]
        span = 1
        while span < w:
            run = run + pltpu.roll(run, span, 0)
            span *= 2
        inv_cnt = 1.0 / jnp.minimum(w, pos + 1).astype(F32)
        pooled = run[hist:, :] * inv_cnt - u[:, cols]
        outs.append(_dot(pooled.astype(BF16), w_ref[gi].astype(BF16)))
    y = jnp.concatenate(outs, axis=-1) * s_ref[...]
    y_ref[...] = y.astype(BF16)
    prev_ref[...] = u[tt - hist:, :]

    @pl.when(t == nt - 1)
    def _():
        tail_ref[0] = u[tt - hist:, :]


def _pool_prompt(proj, w_pool, pool_scale, *, batch, seq, tt):
    p = pool_scale.shape[0]
    hist = POOL_BUF + 1
    nt = seq // tt
    kern = functools.partial(_pool_prompt_kernel, tt=tt)
    return pl.pallas_call(
        kern,
        grid=(batch, nt),
        in_specs=[pl.BlockSpec((tt, p), lambda b, t: (b * nt + t, 0)),
                  pl.BlockSpec(w_pool.shape, lambda b, t: (0, 0, 0)),
                  pl.BlockSpec((1, p), lambda b, t: (0, 0))],
        out_specs=[pl.BlockSpec((tt, p), lambda b, t: (b * nt + t, 0)),
                   pl.BlockSpec((1, hist, p), lambda b, t: (b, 0, 0))],
        out_shape=[jax.ShapeDtypeStruct((batch * seq, p), BF16),
                   jax.ShapeDtypeStruct((batch, hist, p), F32)],
        scratch_shapes=[pltpu.VMEM((hist, p), F32)],
        compiler_params=_cparams(("arbitrary", "arbitrary")),
        name="pool_prompt",
    )(proj, w_pool, pool_scale.reshape(1, p))


def _pool_decode_kernel(st_ref, u_ref, w_ref, s_ref, y_ref, new_ref):
    buf = st_ref.shape[0]
    gw = w_ref.shape[1]
    u = u_ref[...]
    outs = []
    run = u
    used = 0
    for gi, w in enumerate(POOL_WINDOWS):
        while used < w - 1:
            run = run + st_ref[buf - 1 - used]
            used += 1
        cols = slice(gi * gw, (gi + 1) * gw)
        pooled = run[:, cols] / float(min(w, PAST_LEN + 1)) - u[:, cols]
        outs.append(_dot(pooled.astype(BF16), w_ref[gi].astype(BF16)))
    y_ref[...] = (jnp.concatenate(outs, axis=-1) * s_ref[...]).astype(BF16)
    for r in range(buf - 1):
        new_ref[r] = st_ref[r + 1]
    new_ref[buf - 1] = u


def _pool_decode(state_t, proj_s, w_pool, pool_scale):
    buf, n, p = state_t.shape
    return pl.pallas_call(
        _pool_decode_kernel,
        grid=(1,),
        in_specs=[pl.BlockSpec((buf, n, p), lambda i: (0, 0, 0)),
                  pl.BlockSpec((n, p), lambda i: (0, 0)),
                  pl.BlockSpec(w_pool.shape, lambda i: (0, 0, 0)),
                  pl.BlockSpec((1, p), lambda i: (0, 0))],
        out_specs=[pl.BlockSpec((n, p), lambda i: (0, 0)),
                   pl.BlockSpec((buf, n, p), lambda i: (0, 0, 0))],
        out_shape=[jax.ShapeDtypeStruct((n, p), BF16),
                   jax.ShapeDtypeStruct((buf, n, p), F32)],
        compiler_params=_cparams(("arbitrary",)),
        name="pool_decode",
    )(state_t, proj_s, w_pool, pool_scale.reshape(1, p))


def _log_gamma(h):
    return float(np.log(1.0 - 2.0 ** (-5.0 - h)))


def _rope_tables(pos, half):
    inv = np.power(ROPE_BASE, -np.arange(half, dtype=np.float64) / half)
    ang = np.asarray(pos, np.float64)[:, None] * inv[None, :]
    return jnp.asarray(np.cos(ang), F32), jnp.asarray(np.sin(ang), F32)


def _rotary(x, cos, sin):
    half = x.shape[-1] // 2
    x1, x2 = x[:, :half], x[:, half:]
    return jnp.concatenate([x1 * cos - x2 * sin, x2 * cos + x1 * sin], axis=-1)


def _ret_tables(c, key_scale):
    i = np.arange(c, dtype=np.float64)
    diff = i[:, None] - i[None, :]
    dec, qs, ks = [], [], []
    for h in range(RET_HEADS):
        lg = _log_gamma(h)
        dec.append(key_scale * np.where(diff >= 0, np.exp(lg * np.maximum(diff, 0.0)), 0.0))
        qs.append(np.broadcast_to(np.exp(lg * (i + 1.0))[:, None], (c, LANES)))
        ks.append(key_scale * np.broadcast_to(np.exp(lg * (c - 1.0 - i))[:, None], (c, LANES)))
    return (jnp.asarray(np.stack(dec), F32), jnp.asarray(np.stack(qs), F32), jnp.asarray(np.stack(ks), F32))


def _ret_prompt_kernel(q_ref, k_ref, v_ref, g_ref, cos_ref, sin_ref, dec_ref, qs_ref, ks_ref,
                       y_ref, s_out_ref, s_ref, *, chunk_decay):
    c = pl.program_id(1)
    nc = pl.num_programs(1)
    heads, dk, dv = s_ref.shape

    @pl.when(c == 0)
    def _():
        s_ref[...] = jnp.zeros_like(s_ref)

    reps = dk // LANES
    chunk = dec_ref.shape[1]
    for r0 in range(0, q_ref.shape[0], chunk):
        rows = slice(r0, r0 + chunk)
        cos, sin = cos_ref[rows, :], sin_ref[rows, :]
        for h in range(heads):
            q = _rotary(q_ref[rows, h * dk:(h + 1) * dk].astype(F32), cos, sin)
            k = _rotary(k_ref[rows, h * dk:(h + 1) * dk].astype(F32), cos, sin)
            v = v_ref[rows, h * dv:(h + 1) * dv]
            qs = jnp.concatenate([qs_ref[h]] * reps, axis=-1)
            ks = jnp.concatenate([ks_ref[h]] * reps, axis=-1)
            scores = lax.dot_general(q.astype(BF16), k.astype(BF16), (((1,), (1,)), ((), ())),
                                     preferred_element_type=F32) * dec_ref[h]
            state = s_ref[h]
            o = _dot(scores.astype(BF16), v) + _dot((q * qs).astype(BF16), state.astype(BF16))
            kv = lax.dot_general((k * ks).astype(BF16), v, (((0,), (0,)), ((), ())), preferred_element_type=F32)
            s_ref[h] = chunk_decay[h] * state + kv
            g = g_ref[rows, h * dv:(h + 1) * dv].astype(F32)
            y_ref[rows, h * dv:(h + 1) * dv] = (_silu(g) * _rms(o)).astype(BF16)

    @pl.when(c == nc - 1)
    def _():
        s_out_ref[0] = s_ref[...]


def _ret_prompt(qkvg, *, batch, seq, d_model, chunk):
    heads = RET_HEADS
    dk = d_model // heads
    dv = 2 * d_model // heads
    rows = min(RET_STEP_CHUNKS * chunk, seq)
    nc = seq // rows
    cos, sin = _rope_tables(np.arange(seq), dk // 2)
    key_scale = dk ** -0.5
    assert np.log2(key_scale) == np.round(np.log2(key_scale))
    dec, qs, ks = _ret_tables(chunk, key_scale)
    chunk_decay = tuple(float(np.exp(_log_gamma(h) * chunk)) for h in range(heads))
    kern = functools.partial(_ret_prompt_kernel, chunk_decay=chunk_decay)
    const = lambda a: pl.BlockSpec(a.shape, lambda b, c: (0,) * a.ndim)
    return pl.pallas_call(
        kern,
        grid=(batch, nc),
        in_specs=[pl.BlockSpec((rows, d_model), lambda b, c: (b * nc + c, 0)),
                  pl.BlockSpec((rows, d_model), lambda b, c: (b * nc + c, 1)),
                  pl.BlockSpec((rows, 2 * d_model), lambda b, c: (b * nc + c, 1)),
                  pl.BlockSpec((rows, 2 * d_model), lambda b, c: (b * nc + c, 2)),
                  pl.BlockSpec((rows, dk // 2), lambda b, c: (c, 0)),
                  pl.BlockSpec((rows, dk // 2), lambda b, c: (c, 0)),
                  const(dec), const(qs), const(ks)],
        out_specs=[pl.BlockSpec((rows, heads * dv), lambda b, c: (b * nc + c, 0)),
                   pl.BlockSpec((1, heads, dk, dv), lambda b, c: (b, 0, 0, 0))],
        out_shape=[jax.ShapeDtypeStruct((batch * seq, heads * dv), BF16),
                   jax.ShapeDtypeStruct((batch, heads, dk, dv), F32)],
        scratch_shapes=[pltpu.VMEM((heads, dk, dv), F32)],
        compiler_params=_cparams(("arbitrary", "arbitrary")),
        name="retention_prompt",
    )(qkvg, qkvg, qkvg, qkvg, cos, sin, dec, qs, ks)


def _ret_decode_kernel(q_ref, k_ref, v_ref, g_ref, cos_ref, sin_ref, s_ref, y_ref, s_out_ref,
                       kt_ref, qd_ref, o_ref, *, gammas, bs):
    h = pl.program_id(0)
    blk = pl.program_id(1)
    nblk = pl.num_programs(1)
    n, dk = q_ref.shape
    gamma = jnp.where(h == 0, gammas[0], jnp.where(h == 1, gammas[1], jnp.where(h == 2, gammas[2], gammas[3])))

    @pl.when(blk == 0)
    def _():
        cos, sin = cos_ref[...], sin_ref[...]
        q = _rotary(q_ref[...], cos, sin)
        k = _rotary(k_ref[...], cos, sin) * (dk ** -0.5)
        kt_ref[...] = k.T
        qd_ref[...] = (q * gamma).astype(BF16)
        o_ref[...] = jnp.sum(q * k, axis=-1, keepdims=True) * v_ref[...]

    vb = v_ref[...].astype(BF16)
    rows = lax.broadcasted_iota(jnp.int32, (n, 1), 0)
    lanes = lax.broadcasted_iota(jnp.int32, (1, n), 1)
    for i in range(bs):
        tok = blk * bs + i
        state = s_ref[i, 0]
        o_ref[...] += jnp.where(rows == tok, _dot(qd_ref[...], state.astype(BF16)), 0.0)
        k_col = jnp.where(lanes == tok, kt_ref[...], 0.0).astype(BF16)
        s_out_ref[i, 0] = gamma * state + _dot(k_col, vb)

    @pl.when(blk == nblk - 1)
    def _():
        g = g_ref[...]
        y_ref[...] = (_silu(g) * _rms(o_ref[...])).astype(BF16)


def _ret_decode(proj_s, state, *, d_model, bs):
    n = proj_s.shape[0]
    heads = RET_HEADS
    dk = d_model // heads
    dv = 2 * d_model // heads
    cos, sin = _rope_tables(np.array([PAST_LEN]), dk // 2)
    gammas = tuple(float(np.exp(_log_gamma(h))) for h in range(heads))
    q0, k0, v0, g0 = d_model // dk, 2 * d_model // dk, 3 * d_model // dv, 5 * d_model // dv
    kern = functools.partial(_ret_decode_kernel, gammas=gammas, bs=bs)
    return pl.pallas_call(
        kern,
        grid=(heads, n // bs),
        in_specs=[pl.BlockSpec((n, dk), lambda h, b: (0, q0 + h)),
                  pl.BlockSpec((n, dk), lambda h, b: (0, k0 + h)),
                  pl.BlockSpec((n, dv), lambda h, b: (0, v0 + h)),
                  pl.BlockSpec((n, dv), lambda h, b: (0, g0 + h)),
                  pl.BlockSpec((1, dk // 2), lambda h, b: (0, 0)),
                  pl.BlockSpec((1, dk // 2), lambda h, b: (0, 0)),
                  pl.BlockSpec((bs, 1, dk, dv), lambda h, b: (b, h, 0, 0))],
        out_specs=[pl.BlockSpec((n, dv), lambda h, b: (0, h)),
                   pl.BlockSpec((bs, 1, dk, dv), lambda h, b: (b, h, 0, 0))],
        out_shape=[jax.ShapeDtypeStruct((n, heads * dv), BF16),
                   jax.ShapeDtypeStruct(state.shape, F32)],
        scratch_shapes=[pltpu.VMEM((dk, n), F32), pltpu.VMEM((n, dk), BF16), pltpu.VMEM((n, dv), F32)],
        compiler_params=_cparams(("arbitrary", "arbitrary")),
        name="retention_decode",
    )(proj_s, proj_s, proj_s, proj_s, cos, sin, state)


def _mix_kernel(x_ref, ya_ref, yb_ref, ga_ref, gb_ref, mod_ref, g2_ref, wa_ref, wb_ref, wo_ref,
                wr_ref, br_ref, x1_ref, h2_ref, route_ref, *, per_row, tiles_per_seq, n_experts):
    i = pl.program_id(0)
    tm, d = x_ref.shape
    b = i // tiles_per_seq

    merged = (_sigmoid(ga_ref[...].astype(F32)) * _dot(ya_ref[...], wa_ref[...])
              + _sigmoid(gb_ref[...].astype(F32)) * _dot(yb_ref[...], wb_ref[...]))
    mix = _dot(merged.astype(BF16), wo_ref[...])
    x1 = x_ref[...] + _mod_rows(mod_ref, 2, d, per_row, b) * mix
    x1_ref[...] = x1
    h2 = (_rms(x1) * g2_ref[...] * (1.0 + _mod_rows(mod_ref, 4, d, per_row, b))
          + _mod_rows(mod_ref, 3, d, per_row, b))
    h2_ref[...] = h2.astype(BF16)

    logits_t = _dot3_nt(wr_ref[...], h2) + br_ref[:, 0:1]
    row = lax.broadcasted_iota(jnp.int32, (n_experts, tm), 0)
    neg = jnp.float32(-jnp.inf)
    work = logits_t
    vals, idxs = [], []
    for _ in range(TOP_K):
        m = jnp.max(work, axis=0, keepdims=True)
        idx = jnp.min(jnp.where(work == m, row, n_experts), axis=0, keepdims=True)
        vals.append(m)
        idxs.append(idx)
        work = jnp.where(row == idx, neg, work)
    exps = [jnp.exp(v - vals[0]) for v in vals]
    denom = exps[0] + exps[1] + exps[2] + exps[3]
    route_ref[...] = jnp.concatenate([idx.astype(F32) for idx in idxs] + [e / denom for e in exps]
                                     + [jnp.zeros((LANES - 2 * TOP_K, tm), F32)], axis=0)


def _mix(x, ya, yb, proj, mod, g2, wa_bf, wb_bf, wo_bf, wr_pad, br_pad, *, tm, per_row, tiles_per_seq,
         n_experts, ga_blk):
    n, d = x.shape
    gb_blk = ga_blk + 1
    kern = functools.partial(_mix_kernel, per_row=per_row, tiles_per_seq=tiles_per_seq, n_experts=n_experts)
    mod_spec = (pl.BlockSpec((tm, mod.shape[1]), lambda i: (i, 0)) if per_row
                else pl.BlockSpec(mod.shape, lambda i: (0, 0)))
    full = lambda a: pl.BlockSpec(a.shape, lambda i: (0,) * a.ndim)
    return pl.pallas_call(
        kern,
        grid=(n // tm,),
        in_specs=[pl.BlockSpec((tm, d), lambda i: (i, 0)),
                  pl.BlockSpec((tm, ya.shape[1]), lambda i: (i, 0)),
                  pl.BlockSpec((tm, yb.shape[1]), lambda i: (i, 0)),
                  pl.BlockSpec((tm, d), lambda i: (i, ga_blk)),
                  pl.BlockSpec((tm, d), lambda i: (i, gb_blk)),
                  mod_spec,
                  pl.BlockSpec((1, d), lambda i: (0, 0)),
                  full(wa_bf), full(wb_bf), full(wo_bf), full(wr_pad), full(br_pad)],
        out_specs=[pl.BlockSpec((tm, d), lambda i: (i, 0)),
                   pl.BlockSpec((tm, d), lambda i: (i, 0)),
                   pl.BlockSpec((LANES, tm), lambda i: (0, i))],
        out_shape=[jax.ShapeDtypeStruct((n, d), F32),
                   jax.ShapeDtypeStruct((n, d), BF16),
                   jax.ShapeDtypeStruct((LANES, n), F32)],
        compiler_params=_cparams(("arbitrary",)),
        name="mix_route_per_row" if per_row else "mix_route",
    )(x, ya, yb, proj, proj, mod, g2.reshape(1, d), wa_bf, wb_bf, wo_bf, wr_pad, br_pad)


def _sorted_rows(tt, n_experts):
    return _round_up(tt * TOP_K + n_experts * (ROW_ALIGN - 1) + ROW_ALIGN, SORT_BLOCK)


def _dispatch_kernel(hp_ref, rp_ref, hd_ref, rd_ref, before_ref, below_ref, xs_ref, r2_ref, cnt_ref, *, lmax):
    tt = before_ref.shape[0]
    is_dec = pl.program_id(0) == pl.num_programs(0) - 1
    for sub in range(hp_ref.shape[0] // tt):
        tok = slice(sub * tt, (sub + 1) * tt)
        _sort_tile(jnp.where(is_dec, rd_ref[:, tok], rp_ref[:, tok]),
                   jnp.where(is_dec, hd_ref[tok, :], hp_ref[tok, :]),
                   before_ref, below_ref, xs_ref.at[sub * lmax:(sub + 1) * lmax, :], r2_ref.at[tok, :],
                   cnt_ref.at[sub * below_ref.shape[0]:(sub + 1) * below_ref.shape[0], :], lmax=lmax)


def _sort_tile(route_t, h_tile, before_ref, below_ref, xs_ref, r2_ref, cnt_ref, *, lmax):
    tt = h_tile.shape[0]
    n_e = cnt_ref.shape[0]
    expert = lax.broadcasted_iota(jnp.int32, (n_e, tt), 0).astype(F32)
    hits = [expert == route_t[kk:kk + 1, :] for kk in range(TOP_K)]
    sel = jnp.zeros((n_e, tt), F32)
    for hit in hits:
        sel = sel + jnp.where(hit, 1.0, 0.0)
    cnt = jnp.sum(sel, axis=1, keepdims=True)
    seg = jnp.broadcast_to(jnp.ceil(cnt / ROW_ALIGN) * ROW_ALIGN, (n_e, LANES))
    off = _dot3(below_ref[...], seg)[:, 0:1]
    place = _dot(sel.astype(BF16), before_ref[...]) + off
    w_hi, w_lo, pos_a, pos_b, pos_all = [], [], [], [], []
    for kk in range(TOP_K):
        found = jnp.sum(jnp.where(hits[kk], 1.0, 0.0), axis=0, keepdims=True)
        pos = jnp.where(found > 0.0, jnp.sum(jnp.where(hits[kk], place, 0.0), axis=0, keepdims=True), -1.0)
        w = route_t[TOP_K + kk:TOP_K + kk + 1, :]
        hi = w.astype(BF16).astype(F32)
        a = jnp.floor(pos / POS_SPLIT)
        w_hi.append(hi)
        w_lo.append(w - hi)
        pos_a.append(a)
        pos_b.append(pos - POS_SPLIT * a)
        pos_all.append(pos)
    info_t = jnp.concatenate(w_hi + w_lo + pos_a + pos_b + pos_all
                             + [jnp.zeros((LANES - 5 * TOP_K, tt), F32)], axis=0)
    side = info_t.T
    r2_ref[...] = side
    cnt_ref[...] = jnp.broadcast_to(cnt, cnt_ref.shape)
    r2t = info_t.astype(jnp.int32).astype(jnp.int16)
    h = jnp.concatenate([h_tile, side.astype(BF16)], axis=-1)
    d = h_tile.shape[1]
    out_lane = lax.broadcasted_iota(jnp.int32, (SORT_BLOCK, LANES), 1)
    one = jnp.ones((SORT_BLOCK, tt), BF16)
    zero = jnp.zeros((SORT_BLOCK, tt), BF16)
    for rb in range(lmax // SORT_BLOCK):
        rio = (lax.broadcasted_iota(jnp.int32, (SORT_BLOCK, tt), 0) + rb * SORT_BLOCK).astype(jnp.int16)
        hit = rio == r2t[POS_LANE:POS_LANE + 1, :]
        for kk in range(1, TOP_K):
            hit = hit | (rio == r2t[POS_LANE + kk:POS_LANE + kk + 1, :])
        perm = jnp.where(hit, one, zero)
        res = _dot(perm, h)
        info = res[:, d:]
        row = (lax.broadcasted_iota(jnp.int32, (SORT_BLOCK, LANES), 0) + rb * SORT_BLOCK).astype(F32)
        pos = (POS_SPLIT * pltpu.roll(info, LANES - 2 * TOP_K, 1) + pltpu.roll(info, LANES - 3 * TOP_K, 1))
        mine = jnp.where((pos == row) & (out_lane < TOP_K), 1.0, 0.0)
        mine = mine + pltpu.roll(mine, TOP_K, 1)
        rows = slice(rb * SORT_BLOCK, (rb + 1) * SORT_BLOCK)
        xs_ref[rows, :d] = res[:, :d].astype(BF16)
        xs_ref[rows, d:] = (info * mine).astype(BF16)


def _dispatch(h2_p, route_p, h2_d, route_d, *, tt, lmax, n_experts):
    n, d = h2_p.shape
    group = DISPATCH_STEP_TILES if (n // tt) % DISPATCH_STEP_TILES == 0 else 1
    rows = group * tt
    p_steps = n // rows
    tiles = (p_steps + 1) * group
    pad = rows - h2_d.shape[0]
    h2_d = jnp.pad(h2_d, ((0, pad), (0, 0)))
    route_d = jnp.pad(route_d, ((0, 0), (0, pad)), constant_values=-1.0)
    kern = functools.partial(_dispatch_kernel, lmax=lmax)
    before = jnp.asarray(np.triu(np.ones((tt, tt), np.float32), 1), BF16)
    below = jnp.asarray(np.tril(np.ones((n_experts, n_experts), np.float32), -1), F32)
    return (tiles,) + tuple(pl.pallas_call(
        kern,
        grid=(p_steps + 1,),
        in_specs=[pl.BlockSpec((rows, d), lambda i: (jnp.minimum(i, p_steps - 1), 0)),
                  pl.BlockSpec((LANES, rows), lambda i: (0, jnp.minimum(i, p_steps - 1))),
                  pl.BlockSpec((rows, d), lambda i: (0, 0)),
                  pl.BlockSpec((LANES, rows), lambda i: (0, 0)),
                  pl.BlockSpec((tt, tt), lambda i: (0, 0)),
                  pl.BlockSpec((n_experts, n_experts), lambda i: (0, 0))],
        out_specs=[pl.BlockSpec((group * lmax, d + LANES), lambda i: (i, 0)),
                   pl.BlockSpec((rows, LANES), lambda i: (i, 0)),
                   pl.BlockSpec((group * n_experts, LANES), lambda i: (i, 0))],
        out_shape=[jax.ShapeDtypeStruct((tiles * lmax, d + LANES), BF16),
                   jax.ShapeDtypeStruct((tiles * tt, LANES), F32),
                   jax.ShapeDtypeStruct((tiles * n_experts, LANES), F32)],
        compiler_params=_cparams(("arbitrary",)),
        name="dispatch_sort",
    )(h2_p, route_p, h2_d, route_d, before, below))


def _tables_kernel(cnt_ref, te_ref, na_ref, nxt_ref, par_ref, msrc_ref, csrc_ref,
                   before_ref, start_ref, len_ref, first_ref, end_ref, total_ref, *, tiles, n_e, cpt, cstride,
                   lmax, n_tiles, zero_row):
    def nblk(i, e):
        return len_ref[i * n_e + e]

    def per_tile(i, carry):
        def per_expert(e, run):
            blocks = lax.shift_right_logical(cnt_ref[i * n_e + e] + (ROW_ALIGN - 1), ROW_ALIGN.bit_length() - 1)
            len_ref[i * n_e + e] = blocks
            start_ref[i * n_e + e] = run
            return run + blocks
        lax.fori_loop(0, n_e, per_expert, 0)
        return carry

    lax.fori_loop(0, tiles, per_tile, 0)

    def per_expert(e, carry):
        def per_tile(i, run):
            before_ref[i * n_e + e] = run
            return run + nblk(i, e)
        total_ref[e] = lax.fori_loop(0, tiles, per_tile, 0)
        return carry

    lax.fori_loop(0, n_e, per_expert, 0)

    def expert_tiles(e, carry):
        t_first, run = carry
        t_last = t_first + lax.shift_right_logical(total_ref[e] + (cpt - 1), cpt.bit_length() - 1)
        first_ref[e] = t_first
        end_ref[e] = t_last

        def mark(t, carry):
            te_ref[t] = e
            par_ref[t] = run % 2
            return carry

        lax.fori_loop(t_first, t_last, mark, 0)
        return t_last, jnp.where(t_last > t_first, run + 1, run)

    n_act, _ = lax.fori_loop(0, n_e, expert_tiles, (jnp.int32(0), jnp.int32(0)))
    na_ref[0] = n_act
    last = te_ref[jnp.maximum(n_act - 1, 0)]

    def mark_rest(t, carry):
        te_ref[t] = last
        par_ref[t] = 0
        nxt_ref[t] = -1
        return carry

    lax.fori_loop(n_act, n_tiles, mark_rest, 0)

    def mark_next(s, nxt):
        t = n_act - 1 - s
        after = te_ref[jnp.minimum(t + 1, n_tiles - 1)]
        nxt = jnp.where((t + 1 < n_act) & (after != te_ref[t]), after, nxt)
        nxt_ref[t] = nxt
        return nxt

    lax.fori_loop(0, n_act, mark_next, jnp.int32(-1))

    shape = msrc_ref.shape
    n_idx = (lax.broadcasted_iota(jnp.int32, shape, 0) * LANES + lax.broadcasted_iota(jnp.int32, shape, 1))
    m_tile = lax.shift_right_logical(n_idx, cpt.bit_length() - 1)
    expert = jnp.zeros(shape, jnp.int32)
    for e in range(n_e):
        expert = expert + (end_ref[e] <= m_tile).astype(jnp.int32)
    t_first = jnp.zeros(shape, jnp.int32)
    total = jnp.zeros(shape, jnp.int32)
    for e in range(n_e):
        t_first = jnp.where(expert == e, first_ref[e], t_first)
        total = jnp.where(expert == e, total_ref[e], total)
    g = (m_tile - t_first) * cpt + (n_idx & (cpt - 1))
    src = jnp.full(shape, zero_row, jnp.int32)
    for i in range(tiles):
        seg_first = jnp.zeros(shape, jnp.int32)
        seg_len = jnp.zeros(shape, jnp.int32)
        seg_local = jnp.zeros(shape, jnp.int32)
        for e in range(n_e):
            mine = expert == e
            seg_first = jnp.where(mine, before_ref[i * n_e + e], seg_first)
            seg_len = jnp.where(mine, nblk(i, e), seg_len)
            seg_local = jnp.where(mine, start_ref[i * n_e + e], seg_local)
        inside = (g >= seg_first) & (g < seg_first + seg_len) & (g < total) & (m_tile < n_act)
        src = jnp.where(inside, i * lmax + (seg_local + g - seg_first) * ROW_ALIGN, src)
    msrc_ref[...] = src

    rows_per_tile = cstride // LANES
    c_idx = (lax.broadcasted_iota(jnp.int32, (rows_per_tile, LANES), 0) * LANES
             + lax.broadcasted_iota(jnp.int32, (rows_per_tile, LANES), 1))
    for i in range(tiles):
        owner = jnp.zeros((rows_per_tile, LANES), jnp.int32)
        for e in range(n_e):
            owner = owner + ((start_ref[i * n_e + e] + nblk(i, e)) <= c_idx).astype(jnp.int32)
        base = jnp.zeros((rows_per_tile, LANES), jnp.int32)
        for e in range(n_e):
            base = jnp.where(owner == e, first_ref[e] * cpt + before_ref[i * n_e + e] - start_ref[i * n_e + e],
                             base)
        csrc_ref[i * rows_per_tile:(i + 1) * rows_per_tile, :] = jnp.where(owner < n_e,
                                                                         (base + c_idx) * ROW_ALIGN, 0)


def _routing_tables(cnt_flat, *, tiles, n_e, tmo, n_tiles, cstride, lmax):
    cpt = tmo // ROW_ALIGN
    assert (n_tiles * cpt) % LANES == 0 and cstride % LANES == 0
    kern = functools.partial(_tables_kernel, tiles=tiles, n_e=n_e, cpt=cpt, cstride=cstride, lmax=lmax,
                             n_tiles=n_tiles, zero_row=lmax - ROW_ALIGN)
    smem = pl.BlockSpec(memory_space=pltpu.SMEM)
    vmem = pl.BlockSpec(memory_space=pltpu.VMEM)
    te, na, nxt, par, msrc, csrc = pl.pallas_call(
        kern,
        in_specs=[smem],
        out_specs=[smem, smem, smem, smem, vmem, vmem],
        out_shape=[jax.ShapeDtypeStruct((n_tiles,), jnp.int32),
                   jax.ShapeDtypeStruct((1,), jnp.int32),
                   jax.ShapeDtypeStruct((n_tiles,), jnp.int32),
                   jax.ShapeDtypeStruct((n_tiles,), jnp.int32),
                   jax.ShapeDtypeStruct((n_tiles * cpt // LANES, LANES), jnp.int32),
                   jax.ShapeDtypeStruct((tiles * cstride // LANES, LANES), jnp.int32)],
        scratch_shapes=[pltpu.SMEM((tiles * n_e,), jnp.int32), pltpu.SMEM((tiles * n_e,), jnp.int32),
                        pltpu.SMEM((tiles * n_e,), jnp.int32),
                        pltpu.SMEM((n_e,), jnp.int32), pltpu.SMEM((n_e,), jnp.int32),
                        pltpu.SMEM((n_e,), jnp.int32)],
        name="routing_tables",
    )(cnt_flat)
    return te, na, msrc.reshape(-1), csrc.reshape(-1), nxt, par


def _moe_kernel(tile_expert_ref, n_active_ref, src_ref, nxt_ref, par_ref, xs_hbm, wi_hbm, bi_ref, wo_hbm, bo_ref,
                y_ref, xbuf, wi_f32, wo_f32, wi_bf, wo_bf, sem, wsem):
    j = pl.program_id(0)
    tm = xbuf.shape[1]
    d = y_ref.shape[1]
    f = wo_bf.shape[0]
    cpt = tm // ROW_ALIGN
    slot = j % 2
    n_active = n_active_ref[0]
    expert = tile_expert_ref[j]

    def start_tile(tile, s):
        for c in range(cpt):
            row = pl.multiple_of(src_ref[tile * cpt + c], ROW_ALIGN)
            pltpu.make_async_copy(xs_hbm.at[pl.ds(row, ROW_ALIGN), :],
                                  xbuf.at[s, pl.ds(c * ROW_ALIGN, ROW_ALIGN), :], sem.at[s]).start()

    def wait_tile(s):
        pltpu.make_async_copy(xs_hbm.at[pl.ds(0, tm), :], xbuf.at[s], sem.at[s]).wait()

    def weight_copies(e, s):
        return (pltpu.make_async_copy(wi_hbm.at[e], wi_f32.at[s], wsem.at[0, s]),
                pltpu.make_async_copy(wo_hbm.at[e], wo_f32.at[s], wsem.at[1, s]))

    @pl.when(j == 0)
    def _():
        start_tile(0, 0)
        for cp in weight_copies(expert, par_ref[0]):
            cp.start()

    @pl.when(j + 1 < n_active)
    def _():
        start_tile(j + 1, 1 - slot)

    @pl.when(j < n_active)
    def _():
        @pl.when((j == 0) | (expert != tile_expert_ref[jnp.maximum(j - 1, 0)]))
        def _():
            ws = par_ref[j]
            for cp in weight_copies(expert, ws):
                cp.wait()

            @pl.when(nxt_ref[j] >= 0)
            def _():
                for cp in weight_copies(nxt_ref[j], 1 - ws):
                    cp.start()

            wi_bf[...] = wi_f32[ws].astype(BF16)
            wo_bf[...] = wo_f32[ws].astype(BF16)

        wait_tile(slot)
        rows = xbuf[slot]
        x = rows[:, :d]
        out = None
        for c0 in range(0, f, MOE_HIDDEN_SLICE):
            cols = slice(c0, c0 + MOE_HIDDEN_SLICE)
            gate = jnp.minimum(_dot(x, wi_bf[:, cols]) + bi_ref[0][:, cols], SWIGLU_LIMIT)
            lin = jnp.clip(_dot(x, wi_bf[:, f + c0:f + c0 + MOE_HIDDEN_SLICE])
                           + bi_ref[0][:, f + c0:f + c0 + MOE_HIDDEN_SLICE], -SWIGLU_LIMIT, SWIGLU_LIMIT)
            half_gate = 0.5 * gate
            act = (lin + 1.0) * (half_gate + half_gate * jnp.tanh((0.5 * SWIGLU_ALPHA) * gate))
            part = _dot(act.astype(BF16), wo_bf[cols, :])
            out = part if out is None else out + part
        weight = jnp.sum(rows[:, d:].astype(F32), axis=-1, keepdims=True)
        y_ref[...] = ((out + bo_ref[0]) * weight).astype(BF16)

    @pl.when(j >= n_active)
    def _():
        y_ref[...] = jnp.zeros_like(y_ref)


def _moe(xs, tables, w_in, b_in, w_out, b_out, *, tm, n_tiles):
    tile_expert, n_active, moe_src, _, nxt, par = tables
    e, d, f2 = w_in.shape
    f = w_out.shape[1]
    bias = lambda width: pl.BlockSpec((1, 1, width), lambda j, te, *_: (te[j], 0, 0))
    grid_spec = pltpu.PrefetchScalarGridSpec(
        num_scalar_prefetch=5,
        grid=(n_tiles,),
        in_specs=[pl.BlockSpec(memory_space=pl.ANY),
                  pl.BlockSpec(memory_space=pl.ANY),
                  bias(f2),
                  pl.BlockSpec(memory_space=pl.ANY),
                  bias(d)],
        out_specs=pl.BlockSpec((tm, d), lambda j, *_: (j, 0)),
        scratch_shapes=[pltpu.VMEM((2, tm, xs.shape[1]), BF16),
                        pltpu.VMEM((2, d, f2), F32),
                        pltpu.VMEM((2, f, d), F32),
                        pltpu.VMEM((d, f2), BF16),
                        pltpu.VMEM((f, d), BF16),
                        pltpu.SemaphoreType.DMA((2,)),
                        pltpu.SemaphoreType.DMA((2, 2))],
    )
    return pl.pallas_call(
        _moe_kernel,
        grid_spec=grid_spec,
        out_shape=jax.ShapeDtypeStruct((n_tiles * tm, d), BF16),
        compiler_params=_cparams(("arbitrary",)),
        name="moe_experts",
    )(tile_expert, n_active, moe_src, nxt, par, xs, w_in, b_in.reshape(e, 1, f2), w_out, b_out.reshape(e, 1, d))


def _combine_kernel(src_ref, x1_ref, r2_ref, mod_ref, gf_ref, y_hbm, o_ref, ybuf, sem,
                    *, per_row, tiles_per_seq, tile0, stride, single):
    i = pl.program_id(0)
    nt = pl.num_programs(0)
    tt, d = x1_ref.shape
    lmax = ybuf.shape[1]
    nch = lmax // ROW_ALIGN
    slot = i % 2
    b = i // tiles_per_seq

    def start_tile(tile, s):
        for c in range(nch):
            row = pl.multiple_of(src_ref[(tile0 + tile) * stride + c], ROW_ALIGN)
            pltpu.make_async_copy(y_hbm.at[pl.ds(row, ROW_ALIGN), :],
                                  ybuf.at[s, pl.ds(c * ROW_ALIGN, ROW_ALIGN), :], sem.at[s]).start()

    def wait_tile(s):
        pltpu.make_async_copy(y_hbm.at[pl.ds(0, lmax), :], ybuf.at[s], sem.at[s]).wait()

    @pl.when(i == 0)
    def _():
        start_tile(0, 0)

    if not single:
        start_tile(jnp.minimum(i + 1, nt - 1), 1 - slot)
    wait_tile(slot)
    r2 = r2_ref[...].astype(jnp.int32).astype(jnp.int16)
    one = jnp.ones((tt, SORT_BLOCK), BF16)
    zero = jnp.zeros((tt, SORT_BLOCK), BF16)
    moe = jnp.zeros((tt, d), F32)
    for cb in range(lmax // SORT_BLOCK):
        col = (lax.broadcasted_iota(jnp.int32, (tt, SORT_BLOCK), 1) + cb * SORT_BLOCK).astype(jnp.int16)
        hit = col == r2[:, POS_LANE:POS_LANE + 1]
        for kk in range(1, TOP_K):
            hit = hit | (col == r2[:, POS_LANE + kk:POS_LANE + kk + 1])
        moe = moe + _dot(jnp.where(hit, one, zero), ybuf[slot, cb * SORT_BLOCK:(cb + 1) * SORT_BLOCK, :])
    x2 = x1_ref[...] + _mod_rows(mod_ref, 5, d, per_row, b) * moe
    o_ref[...] = _rms(x2) * gf_ref[...]

    if not single:
        @pl.when(i == nt - 1)
        def _():
            wait_tile(1 - slot)


def _combine(comb_src, x1, r2, mod, gf, ys, *, tt, lmax, per_row, tiles_per_seq, tile0, stride, r2_block0):
    n, d = x1.shape
    kern = functools.partial(_combine_kernel, per_row=per_row, tiles_per_seq=tiles_per_seq, tile0=tile0,
                             stride=stride, single=(n // tt == 1))
    mod_spec = (pl.BlockSpec((tt, mod.shape[1]), lambda i, s: (i, 0)) if per_row
                else pl.BlockSpec(mod.shape, lambda i, s: (0, 0)))
    grid_spec = pltpu.PrefetchScalarGridSpec(
        num_scalar_prefetch=1,
        grid=(n // tt,),
        in_specs=[pl.BlockSpec((tt, d), lambda i, s: (i, 0)),
                  pl.BlockSpec((tt, LANES), lambda i, s: (r2_block0 + i, 0)),
                  mod_spec,
                  pl.BlockSpec((1, d), lambda i, s: (0, 0)),
                  pl.BlockSpec(memory_space=pl.ANY)],
        out_specs=pl.BlockSpec((tt, d), lambda i, s: (i, 0)),
        scratch_shapes=[pltpu.VMEM((2, lmax, d), BF16), pltpu.SemaphoreType.DMA((2,))],
    )
    return pl.pallas_call(
        kern,
        grid_spec=grid_spec,
        out_shape=jax.ShapeDtypeStruct((n, d), F32),
        compiler_params=_cparams(("arbitrary",)),
        name="combine_final_per_row" if per_row else "combine_final",
    )(comb_src, x1, r2, mod, gf.reshape(1, d), ys)


def kernel(x_prompt, x_sample, c_prompt, c_sample, state_pool, state_ret, w_ada, b_ada, norm1_g, w_in,
           w_pool, pool_scale, w_up_pool, w_up_ret, w_out, norm2_g, w_router, b_router, w_expert_in,
           b_expert_in, w_expert_out, b_expert_out, final_norm_g):
    batch, seq, d = x_prompt.shape
    n_dec = x_sample.shape[0]
    n_prompt = batch * seq
    n_experts = w_router.shape[-1]
    xp = x_prompt.reshape(n_prompt, d)
    xs = x_sample.reshape(n_dec, d)

    w_in_bf = w_in[0].astype(BF16)
    wa_bf = w_up_pool[0].astype(BF16)
    wb_bf = w_up_ret[0].astype(BF16)
    wo_bf = w_out[0].astype(BF16)
    assert n_experts % 8 == 0
    wr_pad = w_router[0].T
    br_pad = jnp.broadcast_to(b_router[0][:, None], (n_experts, LANES))

    c_all = jnp.concatenate([c_prompt, c_sample], axis=0)
    c_all = jnp.pad(c_all, ((0, -c_all.shape[0] % 16), (0, 0)))
    mod = _modulation(c_all, w_ada[0], b_ada[0])
    mod_p, mod_s = mod[:batch], mod[batch:batch + n_dec]

    tm = min(IN_PROJ_ROWS, seq)
    u_p, qkvg_p = _in_proj(xp, mod_p, norm1_g[0], w_in_bf, tm=tm, per_row=False, tiles_per_seq=seq // tm,
                           split=True)
    proj_s = _in_proj(xs, mod_s, norm1_g[0], w_in_bf, tm=n_dec, per_row=True, tiles_per_seq=1, split=False)

    ya_p, tail_p = _pool_prompt(u_p, w_pool[0], pool_scale[0], batch=batch, seq=seq, tt=min(POOL_ROWS, seq))
    ya_s, pool_s_t = _pool_decode(jnp.transpose(state_pool[0], (1, 0, 2)), proj_s, w_pool[0], pool_scale[0])
    yb_p, ret_p = _ret_prompt(qkvg_p, batch=batch, seq=seq, d_model=d, chunk=min(RET_CHUNK, seq))
    yb_s, ret_s = _ret_decode(proj_s, state_ret[0], d_model=d, bs=min(RET_DECODE_SEQS, n_dec))

    tmix = min(MIX_ROWS, seq)
    x1_p, h2_p, route_p = _mix(xp, ya_p, yb_p, qkvg_p, mod_p, norm2_g[0], wa_bf, wb_bf, wo_bf, wr_pad, br_pad,
                               tm=tmix, per_row=False, tiles_per_seq=seq // tmix, n_experts=n_experts,
                               ga_blk=6)
    x1_s, h2_s, route_s = _mix(xs, ya_s, yb_s, proj_s, mod_s, norm2_g[0], wa_bf, wb_bf, wo_bf, wr_pad, br_pad,
                               tm=n_dec, per_row=True, tiles_per_seq=1, n_experts=n_experts, ga_blk=7)

    tt = min(TOKEN_TILE, seq)
    assert n_dec <= tt
    p_tiles = n_prompt // tt
    lmax = _sorted_rows(tt, n_experts)
    tiles, xs_buf, r2, cnt_t = _dispatch(h2_p, route_p, h2_s, route_s, tt=tt, lmax=lmax, n_experts=n_experts)

    tmo = MOE_ROW_TILE
    cnt = cnt_t[:, 0].astype(jnp.int32)
    max_rows = (n_prompt + n_dec) * TOP_K + tiles * n_experts * (ROW_ALIGN - 1)
    n_tiles = _round_up(max_rows // tmo + n_experts, LANES * ROW_ALIGN // tmo)
    stride = _round_up(lmax // ROW_ALIGN, LANES)
    tables = _routing_tables(cnt, tiles=tiles, n_e=n_experts, tmo=tmo, n_tiles=n_tiles, cstride=stride,
                             lmax=lmax)
    comb_src = tables[3]

    ys = _moe(xs_buf, tables, w_expert_in[0], b_expert_in[0], w_expert_out[0], b_expert_out[0], tm=tmo,
              n_tiles=n_tiles)

    assert n_prompt % n_dec == 0
    y_p = _combine(comb_src, x1_p, r2, mod_p, final_norm_g, ys, tt=tt, lmax=lmax, per_row=False,
                   tiles_per_seq=seq // tt, tile0=0, stride=stride, r2_block0=0)
    y_s = _combine(comb_src, x1_s, r2, mod_s, final_norm_g, ys, tt=n_dec, lmax=min(lmax, _sorted_rows(n_dec, n_experts)),
                   per_row=True, tiles_per_seq=1, tile0=p_tiles, stride=stride, r2_block0=n_prompt // n_dec)

    return (y_p.reshape(batch, seq, d),
            y_s.reshape(n_dec, 1, d),
            tail_p[None, :, 1:, :],
            ret_p[None],
            jnp.transpose(pool_s_t, (1, 0, 2))[None],
            ret_s[None])
```
